```python
import math
import jax, jax.numpy as jnp
from jax import lax
import numpy as np

D_MODEL = 1024
BATCH = 8
SEQ = 2048
DEPTH = 1
DEC_BATCH = 128
DEC_SEQ = 4
PAST_LEN = 16384
PAGE_SIZE = 128

GM_WIDTH = D_MODEL
GM_HEADS = 8
GM_HEAD_DIM = GM_WIDTH // GM_HEADS
GM_CHUNK = 128
SSM_WIDTH = D_MODEL
SSM_HEAD_DIM = 64
SSM_HEADS = SSM_WIDTH // SSM_HEAD_DIM
SSM_GROUPS = 2
SSM_STATE = 128
SSM_CHUNK = 128
CONV_WIDTH = 4
CONV_DIM = SSM_WIDTH + 2 * SSM_GROUPS * SSM_STATE
MEM_LEN = 256
MEM_HEADS = 4
MEM_WIDTH = D_MODEL
MEM_HEAD_DIM = MEM_WIDTH // MEM_HEADS

MIX_WIDTH = GM_WIDTH + SSM_WIDTH + MEM_WIDTH
SPLIT_SIZES = (2 * GM_WIDTH, GM_WIDTH, SSM_WIDTH, CONV_DIM, SSM_HEADS, MEM_WIDTH, MEM_WIDTH)
IN_WIDTH = sum(SPLIT_SIZES)
SPLIT_POINTS = tuple(int(v) for v in np.cumsum(SPLIT_SIZES)[:-1])
EPS = 1e-6

kernel_name = 'hymba_gmlp_ssd_memory_step'


def rmsnorm(x, g):
    xf = x.astype(jnp.float32)
    y = xf * lax.rsqrt(jnp.mean(xf * xf, axis=-1, keepdims=True) + EPS)
    return (y * g.astype(jnp.float32)).astype(x.dtype)


def layernorm(x, g, b):
    xf = x.astype(jnp.float32)
    mu = jnp.mean(xf, axis=-1, keepdims=True)
    xc = xf - mu
    var = jnp.mean(xc * xc, axis=-1, keepdims=True)
    y = xc * lax.rsqrt(var + EPS) * g.astype(jnp.float32) + b.astype(jnp.float32)
    return y.astype(x.dtype)


def gmlp_branch(uv, gm_norm_g, gm_norm_b, w_s, b_s):
    bsz, t, _ = uv.shape
    lc = min(t, GM_CHUNK)
    uv = jax.nn.gelu(uv, approximate=False)
    u, v = jnp.split(uv, 2, axis=-1)
    v = layernorm(v, gm_norm_g, gm_norm_b)
    causal = jnp.tril(jnp.ones((lc, lc), dtype=bool))
    w = jnp.where(causal, w_s[:, :lc, :lc], 0)
    vc = v.reshape(bsz, t // lc, lc, GM_HEADS, GM_HEAD_DIM)
    mixed = jnp.einsum('hts,bcshd->bcthd', w, vc) + b_s[:, :lc].T[None, None, :, :, None]
    out = u * mixed.reshape(bsz, t, GM_WIDTH)
    return out, v


def causal_conv(xbc, conv_state, conv_w, conv_b):
    xp = jnp.concatenate([conv_state.astype(xbc.dtype), xbc], axis=1)
    y = lax.conv_general_dilated(
        xp, conv_w[:, None, :].astype(xbc.dtype), window_strides=(1,), padding='VALID',
        dimension_numbers=('NWC', 'WIO', 'NWC'), feature_group_count=CONV_DIM)
    new_state = xp[:, xp.shape[1] - (CONV_WIDTH - 1):]
    return jax.nn.silu(y + conv_b), new_state


def ssd_scan(x, dt, a, bm, cm, h0):
    f32 = jnp.float32
    bsz, t = x.shape[:2]
    lc = math.gcd(t, SSM_CHUNK)
    nc = t // lc
    hg = SSM_HEADS // SSM_GROUPS
    x = x.astype(f32).reshape(bsz, nc, lc, SSM_GROUPS, hg, SSM_HEAD_DIM)
    dt = dt.reshape(bsz, nc, lc, SSM_GROUPS, hg)
    bm = bm.astype(f32).reshape(bsz, nc, lc, SSM_GROUPS, SSM_STATE)
    cm = cm.astype(f32).reshape(bsz, nc, lc, SSM_GROUPS, SSM_STATE)
    acum = jnp.cumsum(dt * a.reshape(SSM_GROUPS, hg), axis=2)
    diff = acum[:, :, :, None] - acum[:, :, None, :]
    causal = jnp.tril(jnp.ones((lc, lc), dtype=bool))[:, :, None, None]
    decay = jnp.exp(jnp.where(causal, diff, -jnp.inf))
    xdt = x * dt[..., None]
    cb = jnp.einsum('bclgn,bcsgn->bclsg', cm, bm)
    y_diag = jnp.einsum('bclsg,bclsgk,bcsgkp->bclgkp', cb, decay, xdt)
    to_end = jnp.exp(acum[:, :, -1:] - acum)
    chunk_states = jnp.einsum('bclgn,bclgk,bclgkp->bcgkpn', bm, to_end, xdt)
    chunk_decay = jnp.exp(acum[:, :, -1])

    def step(h, inp):
        s_c, d_c = inp
        return h * d_c[..., None, None] + s_c, h

    h_init = h0.astype(f32).reshape(bsz, SSM_GROUPS, hg, SSM_HEAD_DIM, SSM_STATE)
    h_final, h_prev = lax.scan(step, h_init,
                               (jnp.moveaxis(chunk_states, 1, 0), jnp.moveaxis(chunk_decay, 1, 0)))
    h_prev = jnp.moveaxis(h_prev, 0, 1)
    y_off = jnp.einsum('bclgn,bcgkpn,bclgk->bclgkp', cm, h_prev, jnp.exp(acum))
    y = (y_diag + y_off).reshape(bsz, t, SSM_HEADS, SSM_HEAD_DIM)
    return y, h_final.reshape(bsz, SSM_HEADS, SSM_HEAD_DIM, SSM_STATE)


def mamba2_branch(z, xbc, dt_raw, conv_state, ssm_state, conv_w, conv_b, dt_bias, a_log,
                  d_skip, ssm_norm_g):
    f32 = jnp.float32
    bsz, t, _ = z.shape
    xbc, new_conv = causal_conv(xbc, conv_state, conv_w, conv_b)
    xs, bm, cm = jnp.split(xbc, [SSM_WIDTH, SSM_WIDTH + SSM_GROUPS * SSM_STATE], axis=-1)
    xs = xs.reshape(bsz, t, SSM_HEADS, SSM_HEAD_DIM)
    bm = bm.reshape(bsz, t, SSM_GROUPS, SSM_STATE)
    cm = cm.reshape(bsz, t, SSM_GROUPS, SSM_STATE)
    dt = jax.nn.softplus(dt_raw.astype(f32) + dt_bias.astype(f32))
    a = -jnp.exp(a_log.astype(f32))
    y, h_new = ssd_scan(xs, dt, a, bm, cm, ssm_state)
    y = y + d_skip.astype(f32)[:, None] * xs.astype(f32)
    y = y.reshape(bsz, t, SSM_WIDTH) * jax.nn.silu(z.astype(f32))
    yg = y.reshape(bsz, t, SSM_GROUPS, SSM_WIDTH // SSM_GROUPS)
    yg = yg * lax.rsqrt(jnp.mean(yg * yg, axis=-1, keepdims=True) + EPS)
    y = yg.reshape(bsz, t, SSM_WIDTH) * ssm_norm_g.astype(f32)
    return y.astype(z.dtype), h_new.astype(ssm_state.dtype), new_conv


def memory_kv(mem, mem_norm_g, w_mem_k, w_mem_v):
    bsz = mem.shape[0]
    m = rmsnorm(mem, mem_norm_g)
    k = jnp.einsum('bmd,de->bme', m, w_mem_k).reshape(bsz, MEM_LEN, MEM_HEADS, MEM_HEAD_DIM)
    v = jnp.einsum('bmd,de->bme', m, w_mem_v).reshape(bsz, MEM_LEN, MEM_HEADS, MEM_HEAD_DIM)
    return k, v


def memory_branch(q, gate, mem_k, mem_v):
    bsz, t, _ = q.shape
    qh = q.reshape(bsz, t, MEM_HEADS, MEM_HEAD_DIM).astype(jnp.float32)
    s = jnp.einsum('bthd,bmhd->bhtm', qh, mem_k.astype(jnp.float32)) * (MEM_HEAD_DIM ** -0.5)
    p = jax.nn.softmax(s, axis=-1)
    o = jnp.einsum('bhtm,bmhd->bthd', p.astype(mem_v.dtype), mem_v).reshape(bsz, t, MEM_WIDTH)
    return (o * jax.nn.silu(gate)).astype(q.dtype)


def hybrid_layer(x, conv_state, ssm_state, mem_k, mem_v, norm_g, w_in, gm_norm_g, gm_norm_b,
                 gm_w_spatial, gm_b_spatial, conv_w, conv_b, dt_bias, a_log, d_skip,
                 ssm_norm_g, w_out):
    hn = rmsnorm(x, norm_g)
    proj = jnp.einsum('btd,de->bte', hn, w_in)
    gm_uv, gm_gate, ssm_z, ssm_xbc, ssm_dt, mem_q, mem_gate = jnp.split(proj, SPLIT_POINTS, axis=-1)
    a_out, v_rows = gmlp_branch(gm_uv, gm_norm_g, gm_norm_b, gm_w_spatial, gm_b_spatial)
    a_out = a_out * jax.nn.silu(gm_gate)
    b_out, ssm_new, conv_new = mamba2_branch(ssm_z, ssm_xbc, ssm_dt, conv_state, ssm_state,
                                             conv_w, conv_b, dt_bias, a_log, d_skip, ssm_norm_g)
    c_out = memory_branch(mem_q, mem_gate, mem_k, mem_v)
    mixed = jnp.concatenate([a_out, b_out, c_out], axis=-1)
    y = x + jnp.einsum('bte,ed->btd', mixed, w_out)
    return y, ssm_new, conv_new, v_rows


def setup_inputs(seed: int = 0) -> dict:
    key = jax.random.key(seed)
    ks = jax.random.split(key, 26)
    f32 = jnp.float32

    def nrm(k, shape, scale):
        return jax.random.normal(k, shape, f32) * scale

    u_dt = jax.random.uniform(ks[14], (DEPTH, SSM_HEADS), f32)
    dt0 = jnp.exp(u_dt * (math.log(0.1) - math.log(0.001)) + math.log(0.001))
    return {
        'x_prompt': nrm(ks[0], (BATCH, SEQ, D_MODEL), 1.0),
        'x_sample': nrm(ks[1], (DEC_BATCH, DEC_SEQ, D_MODEL), 1.0),
        'mem_prompt': nrm(ks[2], (BATCH, MEM_LEN, D_MODEL), 1.0),
        'state_ssm': nrm(ks[3], (DEPTH, DEC_BATCH, SSM_HEADS, SSM_HEAD_DIM, SSM_STATE), 0.1),
        'state_conv': nrm(ks[4], (DEPTH, DEC_BATCH, CONV_WIDTH - 1, CONV_DIM), 1.0),
        'cache_mem_k': nrm(ks[5], (DEPTH, DEC_BATCH, MEM_LEN, MEM_HEADS, MEM_HEAD_DIM), 1.0),
        'cache_mem_v': nrm(ks[6], (DEPTH, DEC_BATCH, MEM_LEN, MEM_HEADS, MEM_HEAD_DIM), 1.0),
        'norm_g': 1.0 + nrm(ks[7], (DEPTH, D_MODEL), 0.02),
        'w_in': nrm(ks[8], (DEPTH, D_MODEL, IN_WIDTH), D_MODEL ** -0.5),
        'gm_norm_g': 1.0 + nrm(ks[9], (DEPTH, GM_WIDTH), 0.02),
        'gm_norm_b': nrm(ks[10], (DEPTH, GM_WIDTH), 0.02),
        'gm_w_spatial': nrm(ks[11], (DEPTH, GM_HEADS, GM_CHUNK, GM_CHUNK), 0.05),
        'gm_b_spatial': 1.0 + nrm(ks[12], (DEPTH, GM_HEADS, GM_CHUNK), 0.02),
        'conv_w': nrm(ks[13], (DEPTH, CONV_WIDTH, CONV_DIM), CONV_WIDTH ** -0.5),
        'conv_b': nrm(ks[15], (DEPTH, CONV_DIM), 0.02),
        'dt_bias': dt0 + jnp.log(-jnp.expm1(-dt0)),
        'a_log': jnp.log(jax.random.uniform(ks[16], (DEPTH, SSM_HEADS), f32, minval=1.0, maxval=16.0)),
        'd_skip': 1.0 + nrm(ks[17], (DEPTH, SSM_HEADS), 0.02),
        'ssm_norm_g': 1.0 + nrm(ks[18], (DEPTH, SSM_WIDTH), 0.02),
        'mem_norm_g': 1.0 + nrm(ks[19], (DEPTH, D_MODEL), 0.02),
        'w_mem_k': nrm(ks[20], (DEPTH, D_MODEL, MEM_WIDTH), D_MODEL ** -0.5),
        'w_mem_v': nrm(ks[21], (DEPTH, D_MODEL, MEM_WIDTH), D_MODEL ** -0.5),
        'w_out': nrm(ks[22], (DEPTH, MIX_WIDTH, D_MODEL), MIX_WIDTH ** -0.5),
        'final_norm_g': 1.0 + nrm(ks[23], (D_MODEL,), 0.02),
    }


def reference(x_prompt, x_sample, mem_prompt, state_ssm, state_conv, cache_mem_k, cache_mem_v,
              norm_g, w_in, gm_norm_g, gm_norm_b, gm_w_spatial, gm_b_spatial, conv_w, conv_b,
              dt_bias, a_log, d_skip, ssm_norm_g, mem_norm_g, w_mem_k, w_mem_v, w_out,
              final_norm_g):
    bp = x_prompt.shape[0]
    h_p = x_prompt
    h_s = x_sample
    ssm_p, conv_p, mk_p, mv_p = [], [], [], []
    ssm_s, conv_s, gv_s = [], [], []
    for i in range(DEPTH):
        lw = (norm_g[i], w_in[i], gm_norm_g[i], gm_norm_b[i], gm_w_spatial[i], gm_b_spatial[i],
              conv_w[i], conv_b[i], dt_bias[i], a_log[i], d_skip[i], ssm_norm_g[i], w_out[i])
        mk, mv = memory_kv(mem_prompt, mem_norm_g[i], w_mem_k[i], w_mem_v[i])
        conv0 = jnp.zeros((bp, CONV_WIDTH - 1, CONV_DIM), x_prompt.dtype)
        ssm0 = jnp.zeros((bp, SSM_HEADS, SSM_HEAD_DIM, SSM_STATE), state_ssm.dtype)
        h_p, s_new, c_new, _ = hybrid_layer(h_p, conv0, ssm0, mk, mv, *lw)
        ssm_p.append(s_new)
        conv_p.append(c_new)
        mk_p.append(mk)
        mv_p.append(mv)
        h_s, s_new, c_new, v_rows = hybrid_layer(h_s, state_conv[i], state_ssm[i],
                                                 cache_mem_k[i], cache_mem_v[i], *lw)
        ssm_s.append(s_new)
        conv_s.append(c_new)
        gv_s.append(v_rows)
    y_prompt = rmsnorm(h_p, final_norm_g)
    y_sample = rmsnorm(h_s, final_norm_g)
    return (y_prompt, y_sample, jnp.stack(ssm_p), jnp.stack(conv_p), jnp.stack(mk_p),
            jnp.stack(mv_p), jnp.stack(ssm_s), jnp.stack(conv_s), jnp.stack(gv_s))
```

```python
import functools
import math

import jax
import jax.numpy as jnp
from jax import lax
from jax.experimental import pallas as pl
from jax.experimental.pallas import tpu as pltpu

F32 = jnp.float32
BF16 = jnp.bfloat16

D_MODEL = 1024
GM_WIDTH = 1024
GM_HEADS = 8
GM_HEAD_DIM = 128
CHUNK = 128
SSM_WIDTH = 1024
SSM_HEADS = 16
SSM_HEAD_DIM = 64
SSM_GROUPS = 2
SSM_STATE = 128
HEADS_PER_GROUP = SSM_HEADS // SSM_GROUPS
GROUP_WIDTH = SSM_WIDTH // SSM_GROUPS
CONV_WIDTH = 4
CONV_DIM = SSM_WIDTH + 2 * SSM_GROUPS * SSM_STATE
MEM_LEN = 256
MEM_HEADS = 4
MEM_HEAD_DIM = 256
MEM_WIDTH = 1024
MIX_WIDTH = GM_WIDTH + SSM_WIDTH + MEM_WIDTH
EPS = 1e-6

LANES = 128
SUBLANES = 8

OFF_U = 0
OFF_V = OFF_U + GM_WIDTH
OFF_GATE = OFF_V + GM_WIDTH
OFF_Z = OFF_GATE + GM_WIDTH
OFF_XBC = OFF_Z + SSM_WIDTH
OFF_Q = OFF_XBC + CONV_DIM
OFF_MGATE = OFF_Q + MEM_WIDTH
OFF_DT = OFF_MGATE + MEM_WIDTH
DT_PAD = LANES
IN_PACKED = 8192
PROJ_BLOCK = 1024

PROMPT_TILE = 256
SAMPLE_BATCH_BLOCK = 4
VMEM_LIMIT = 56 * 1024 * 1024


def _rms(x, g):
    return x * lax.rsqrt(jnp.mean(x * x, axis=-1, keepdims=True) + EPS) * g


def _gelu(x):
    return 0.5 * x * (1.0 + lax.erf(x * math.sqrt(0.5)))


def _silu(x):
    return x * jax.nn.sigmoid(x)


def _softplus(x):
    return jnp.maximum(x, 0.0) + jnp.log1p(jnp.exp(-jnp.abs(x)))


def _layernorm(x, g, b):
    mu = jnp.mean(x, axis=-1, keepdims=True)
    xc = x - mu
    var = jnp.mean(xc * xc, axis=-1, keepdims=True)
    return xc * lax.rsqrt(var + EPS) * g + b


def _dot(a, b):
    return jnp.dot(a, b, preferred_element_type=F32)


def _dot_nt(a, b):
    return lax.dot_general(a, b, (((1,), (1,)), ((), ())), preferred_element_type=F32)


def _dot_tn(a, b):
    return lax.dot_general(a, b, (((0,), (0,)), ((), ())), preferred_element_type=F32)


def _group_rmsnorm(y, g):
    halves = []
    for i in range(SSM_GROUPS):
        yg = y[:, i * GROUP_WIDTH:(i + 1) * GROUP_WIDTH]
        halves.append(yg * lax.rsqrt(jnp.mean(yg * yg, axis=-1, keepdims=True) + EPS))
    return jnp.concatenate(halves, axis=-1) * g


def _memory_kv_kernel(mem_ref, g_ref, wk_ref, wv_ref, k_ref, v_ref, kb_ref, vb_ref):
    m = _rms(mem_ref[0], g_ref[...]).astype(BF16)
    k = _dot(m, wk_ref[...])
    v = _dot(m, wv_ref[...])
    k_ref[0] = k
    v_ref[0] = v
    kb_ref[0] = k.astype(BF16)
    vb_ref[0] = v.astype(BF16)


def _memory_kv(mem, g, wk, wv):
    b = mem.shape[0]
    blk = pl.BlockSpec((1, MEM_LEN, D_MODEL), lambda i: (i, 0, 0))
    const = lambda shape: pl.BlockSpec(shape, lambda i: (0,) * len(shape))
    return pl.pallas_call(
        _memory_kv_kernel,
        grid=(b,),
        in_specs=[blk, const((1, D_MODEL)), const((D_MODEL, MEM_WIDTH)), const((D_MODEL, MEM_WIDTH))],
        out_specs=[blk, blk, blk, blk],
        out_shape=[jax.ShapeDtypeStruct((b, MEM_LEN, MEM_WIDTH), F32)] * 2
        + [jax.ShapeDtypeStruct((b, MEM_LEN, MEM_WIDTH), BF16)] * 2,
        compiler_params=pltpu.CompilerParams(dimension_semantics=("arbitrary",),
                                             vmem_limit_bytes=VMEM_LIMIT),
        name="memory_kv",
    )(mem, g, wk, wv)


def _attention(q, kb, vb):
    qb = (q * (MEM_HEAD_DIM ** -0.5)).astype(BF16)
    outs = []
    for h in range(MEM_HEADS):
        hs = slice(h * MEM_HEAD_DIM, (h + 1) * MEM_HEAD_DIM)
        s = _dot_nt(qb[:, hs], kb[:, hs])
        p = jnp.exp(s - jnp.max(s, axis=-1, keepdims=True))
        denom = jnp.sum(p, axis=-1, keepdims=True)
        outs.append(_dot(p.astype(BF16), vb[:, hs]) / denom)
    return jnp.concatenate(outs, axis=-1)


def _prompt_kernel(x_ref, kb_ref, vb_ref, win_ref, wout_ref, ng_ref, gmg_ref, gmb_ref,
                   ws_ref, bsf_ref, cw_ref, cb_ref, dtb_ref, alog_ref, dsk_ref, sng_ref, fg_ref,
                   y_ref, ssm_ref, conv_ref,
                   hn_sc, mix_sc, ug_sc, v_sc, xp_sc, zs_sc, xs_sc, xsb_sc, bm_sc, cm_sc, dt_sc,
                   y_sc, ht_sc, *, tile):
    t = pl.program_id(1)
    nt = pl.num_programs(1)
    nch = tile // CHUNK

    @pl.when(t == 0)
    def _():
        ht_sc[...] = jnp.zeros_like(ht_sc)
        xp_sc[0:SUBLANES, :] = jnp.zeros((SUBLANES, CONV_DIM), F32)

    x = x_ref[0]
    hn = _rms(x, ng_ref[...]).astype(BF16)
    hn_sc[...] = hn

    def proj(off, width):
        return _dot(hn_sc[...], win_ref[:, off:off + width])

    u = _gelu(proj(OFF_U, GM_WIDTH))
    ug_sc[...] = u * _silu(proj(OFF_GATE, GM_WIDTH))
    v = _layernorm(_gelu(proj(OFF_V, GM_WIDTH)), gmg_ref[...], gmb_ref[...])
    v_sc[...] = v.astype(BF16)

    att = _attention(proj(OFF_Q, MEM_WIDTH), kb_ref[0], vb_ref[0])
    mix_sc[:, GM_WIDTH + SSM_WIDTH:MIX_WIDTH] = (att * _silu(proj(OFF_MGATE, MEM_WIDTH))).astype(BF16)

    zs_sc[...] = _silu(proj(OFF_Z, SSM_WIDTH))
    xp_sc[SUBLANES:SUBLANES + tile, :] = proj(OFF_XBC, CONV_DIM)
    acc = cb_ref[...]
    for j in range(CONV_WIDTH):
        lo = SUBLANES - (CONV_WIDTH - 1) + j
        acc = acc + cw_ref[j:j + 1, :] * xp_sc[lo:lo + tile, :]
    xbc = _silu(acc)
    xs = xbc[:, :SSM_WIDTH]
    xs_sc[...] = xs
    xsb_sc[...] = xs.astype(BF16)
    bm_sc[...] = xbc[:, SSM_WIDTH:SSM_WIDTH + SSM_GROUPS * SSM_STATE]
    cm_sc[...] = xbc[:, SSM_WIDTH + SSM_GROUPS * SSM_STATE:]
    tail = xp_sc[tile + SUBLANES - (CONV_WIDTH - 1):tile + SUBLANES, :]
    xp_sc[SUBLANES - (CONV_WIDTH - 1):SUBLANES, :] = tail

    @pl.when(t == nt - 1)
    def _():
        conv_ref[0] = tail

    dt_sc[...] = _softplus(proj(OFF_DT, DT_PAD) + dtb_ref[...])
    a_row = -jnp.exp(alog_ref[...])

    row = lax.broadcasted_iota(jnp.int32, (CHUNK, CHUNK), 0)
    col = lax.broadcasted_iota(jnp.int32, (CHUNK, CHUNK), 1)
    causal = row >= col
    tril = causal.astype(F32)
    head_lane = col < SSM_HEADS
    first_half = col < SSM_HEAD_DIM

    def chunk_body(c, carry):
        r = pl.ds(pl.multiple_of(c * CHUNK, CHUNK), CHUNK)

        for h in range(GM_HEADS):
            hs = slice(h * GM_HEAD_DIM, (h + 1) * GM_HEAD_DIM)
            mixed = _dot(ws_ref[h], v_sc[r, hs]) + bsf_ref[:, hs]
            mix_sc[r, hs] = (ug_sc[r, hs] * mixed).astype(BF16)

        dt = dt_sc[r, :]
        adt = jnp.where(head_lane, dt * a_row, 0.0)
        acum = jnp.dot(tril, adt, precision=lax.Precision.HIGHEST, preferred_element_type=F32)
        acum_t = acum.T
        dt_t = dt.T
        to_end_t = jnp.exp(acum_t[:, CHUNK - 1:CHUNK] - acum_t) * dt_t
        for g in range(SSM_GROUPS):
            gs = slice(g * SSM_STATE, (g + 1) * SSM_STATE)
            bm_g = bm_sc[r, gs]
            cm_g = cm_sc[r, gs]
            cb_g = _dot_nt(cm_g.astype(BF16), bm_g.astype(BF16))
            bm_t = bm_g.T
            for kk in range(0, HEADS_PER_GROUP, 2):
                k0 = g * HEADS_PER_GROUP + kk
                ps = slice(k0 * SSM_HEAD_DIM, (k0 + 2) * SSM_HEAD_DIM)
                lhs_parts, b_parts, decs = [], [], []
                for k in (k0, k0 + 1):
                    a_col = jnp.broadcast_to(acum[:, k:k + 1], (CHUNK, CHUNK))
                    diff = a_col - acum_t[k:k + 1, :]
                    decay = jnp.exp(jnp.where(causal, diff, -jnp.inf))
                    lhs_parts.append((cb_g * decay * dt_t[k:k + 1, :]).astype(BF16))
                    lhs_parts.append((cm_g * jnp.exp(a_col)).astype(BF16))
                    b_parts.append((bm_t * to_end_t[k:k + 1, :]).astype(BF16))
                    decs.append(jnp.exp(a_col[CHUNK - 1:CHUNK, :]))
                xs_pair = xs_sc[r, ps]
                h_pair = ht_sc[:, ps]
                zero = jnp.zeros_like(xs_pair)
                xs_lo = jnp.where(first_half, xs_pair, zero).astype(BF16)
                xs_hi = jnp.where(first_half, zero, xs_pair).astype(BF16)
                h_lo = jnp.where(first_half, h_pair, zero).astype(BF16)
                h_hi = jnp.where(first_half, zero, h_pair).astype(BF16)
                lhs = jnp.concatenate(lhs_parts, axis=1)
                rhs = jnp.concatenate([xs_lo, h_lo, xs_hi, h_hi], axis=0)
                y_sc[:, ps] = _dot(lhs, rhs)
                upd = _dot(jnp.concatenate(b_parts, axis=1), jnp.concatenate([xs_lo, xs_hi], axis=0))
                dec = jnp.where(first_half[0:1, :], decs[0], decs[1])
                ht_sc[:, ps] = h_pair * dec + upd

        xs_c = xs_sc[r, :]
        y = (y_sc[...] + dsk_ref[...] * xs_c) * zs_sc[r, :]
        mix_sc[r, GM_WIDTH:GM_WIDTH + SSM_WIDTH] = _group_rmsnorm(y, sng_ref[...]).astype(BF16)
        return carry

    lax.fori_loop(0, nch, chunk_body, 0)

    @pl.when(t == nt - 1)
    def _():
        ssm_ref[0] = ht_sc[...].T

    out = x + _dot(mix_sc[...], wout_ref[...])
    y_ref[0] = _rms(out, fg_ref[...])


def _prompt_layer(x, kb, vb, w_in, w_out, ng, gmg, gmb, ws, bsf, cw, cb, dtb, alog, dsk, sng, fg):
    b, seq, _ = x.shape
    tile = PROMPT_TILE
    nt = seq // tile
    assert seq % tile == 0 and tile % CHUNK == 0

    def const(shape):
        return pl.BlockSpec(shape, lambda i, j: (0,) * len(shape), pipeline_mode=pl.Buffered(1))

    in_specs = [
        pl.BlockSpec((1, tile, D_MODEL), lambda i, j: (i, j, 0)),
        pl.BlockSpec((1, MEM_LEN, MEM_WIDTH), lambda i, j: (i, 0, 0)),
        pl.BlockSpec((1, MEM_LEN, MEM_WIDTH), lambda i, j: (i, 0, 0)),
        const((D_MODEL, IN_PACKED)),
        const((MIX_WIDTH, D_MODEL)),
        const((1, D_MODEL)), const((1, GM_WIDTH)), const((1, GM_WIDTH)),
        const((GM_HEADS, CHUNK, CHUNK)), const((CHUNK, GM_WIDTH)),
        const((CONV_WIDTH, CONV_DIM)), const((1, CONV_DIM)),
        const((1, DT_PAD)), const((1, DT_PAD)), const((1, SSM_WIDTH)), const((1, SSM_WIDTH)),
        const((1, D_MODEL)),
    ]
    out_specs = [
        pl.BlockSpec((1, tile, D_MODEL), lambda i, j: (i, j, 0)),
        pl.BlockSpec((1, SSM_WIDTH, SSM_STATE), lambda i, j: (i, 0, 0)),
        pl.BlockSpec((1, CONV_WIDTH - 1, CONV_DIM), lambda i, j: (i, 0, 0)),
    ]
    out_shape = [
        jax.ShapeDtypeStruct((b, seq, D_MODEL), F32),
        jax.ShapeDtypeStruct((b, SSM_WIDTH, SSM_STATE), F32),
        jax.ShapeDtypeStruct((b, CONV_WIDTH - 1, CONV_DIM), F32),
    ]
    scratch = [
        pltpu.VMEM((tile, D_MODEL), BF16),
        pltpu.VMEM((tile, MIX_WIDTH), BF16),
        pltpu.VMEM((tile, GM_WIDTH), F32),
        pltpu.VMEM((tile, GM_WIDTH), BF16),
        pltpu.VMEM((tile + SUBLANES, CONV_DIM), F32),
        pltpu.VMEM((tile, SSM_WIDTH), F32),
        pltpu.VMEM((tile, SSM_WIDTH), F32),
        pltpu.VMEM((tile, SSM_WIDTH), BF16),
        pltpu.VMEM((tile, SSM_GROUPS * SSM_STATE), F32),
        pltpu.VMEM((tile, SSM_GROUPS * SSM_STATE), F32),
        pltpu.VMEM((tile, DT_PAD), F32),
        pltpu.VMEM((CHUNK, SSM_WIDTH), F32),
        pltpu.VMEM((SSM_STATE, SSM_WIDTH), F32),
    ]
    return pl.pallas_call(
        functools.partial(_prompt_kernel, tile=tile),
        grid=(b, nt),
        in_specs=in_specs,
        out_specs=out_specs,
        out_shape=out_shape,
        scratch_shapes=scratch,
        compiler_params=pltpu.CompilerParams(dimension_semantics=("arbitrary", "arbitrary"),
                                             vmem_limit_bytes=VMEM_LIMIT),
        name="prompt_layer",
    )(x, kb, vb, w_in, w_out, ng, gmg, gmb, ws, bsf, cw, cb, dtb, alog, dsk, sng, fg)


def _sample_proj_kernel(x_ref, g_ref, w_ref, o_ref):
    hn = _rms(x_ref[...], g_ref[...]).astype(BF16)
    o_ref[...] = _dot(hn, w_ref[...])


def _sample_proj(x, g, w_in):
    m = x.shape[0]
    return pl.pallas_call(
        _sample_proj_kernel,
        grid=(IN_PACKED // PROJ_BLOCK,),
        in_specs=[pl.BlockSpec((m, D_MODEL), lambda j: (0, 0)),
                  pl.BlockSpec((1, D_MODEL), lambda j: (0, 0)),
                  pl.BlockSpec((D_MODEL, PROJ_BLOCK), lambda j: (0, j))],
        out_specs=pl.BlockSpec((m, PROJ_BLOCK), lambda j: (0, j)),
        out_shape=jax.ShapeDtypeStruct((m, IN_PACKED), F32),
        compiler_params=pltpu.CompilerParams(dimension_semantics=("arbitrary",),
                                             vmem_limit_bytes=VMEM_LIMIT),
        name="sample_proj",
    )(x, g, w_in)


def _sample_mix_kernel(p_ref, cst_ref, ssm_ref, k_ref, v_ref, gmg_ref, gmb_ref, gcoef_ref, gbias_ref,
                       cw_ref, cb_ref, dtb_ref, alog_ref, dsk_ref, sng_ref, expand_ref,
                       mix_ref, ssm_out_ref, conv_out_ref, gv_ref,
                       xp_sc, xbc_sc, att_sc, yoff_sc, dec_sc, *, bblk, seq):
    rows = bblk * seq

    def col(off, width):
        return p_ref[:, off:off + width]

    tpos = lax.broadcasted_iota(jnp.int32, (rows, 1), 0) % seq

    def back(a, j):
        return a if j == 0 else pltpu.roll(a, j, 0)

    u = _gelu(col(OFF_U, GM_WIDTH))
    v = _layernorm(_gelu(col(OFF_V, GM_WIDTH)), gmg_ref[...], gmb_ref[...])
    gv_ref[...] = v
    mixed = gbias_ref[...]
    for j in range(seq):
        mixed = mixed + gcoef_ref[j] * back(v, j)
    mix_ref[:, 0:GM_WIDTH] = (u * mixed * _silu(col(OFF_GATE, GM_WIDTH))).astype(BF16)

    xbc_raw = col(OFF_XBC, CONV_DIM)
    for b in range(bblk):
        xp_sc[b, 0:CONV_WIDTH - 1, :] = cst_ref[b]
        xp_sc[b, CONV_WIDTH - 1:CONV_WIDTH - 1 + seq, :] = xbc_raw[b * seq:(b + 1) * seq, :]
    for b in range(bblk):
        acc = jnp.broadcast_to(cb_ref[...], (seq, CONV_DIM))
        for j in range(CONV_WIDTH):
            acc = acc + cw_ref[j:j + 1, :] * xp_sc[b, j:j + seq, :]
        xbc_sc[b * seq:(b + 1) * seq, :] = _silu(acc)
        conv_out_ref[b] = xp_sc[b, seq:seq + CONV_WIDTH - 1, :]
    xs = xbc_sc[:, 0:SSM_WIDTH]
    bm = xbc_sc[:, SSM_WIDTH:SSM_WIDTH + SSM_GROUPS * SSM_STATE]
    cm = xbc_sc[:, SSM_WIDTH + SSM_GROUPS * SSM_STATE:CONV_DIM]

    lane = lax.broadcasted_iota(jnp.int32, (rows, DT_PAD), 1)
    dt = _softplus(col(OFF_DT, DT_PAD) + dtb_ref[...])
    adt = jnp.where(lane < SSM_HEADS, dt * (-jnp.exp(alog_ref[...])), 0.0)
    acum = adt
    for j in range(1, seq):
        acum = acum + jnp.where(tpos >= j, back(adt, j), 0.0)
    a_last = jnp.zeros_like(acum)
    for j in range(seq):
        a_last = a_last + jnp.where(tpos == seq - 1 - j, acum if j == 0 else pltpu.roll(acum, rows - j, 0), 0.0)
    coefs = []
    for j in range(seq):
        cbj = []
        for g in range(SSM_GROUPS):
            gs = slice(g * SSM_STATE, (g + 1) * SSM_STATE)
            cbj.append(jnp.sum(cm[:, gs] * back(bm[:, gs], j), axis=-1, keepdims=True))
        cb_l = jnp.where(lane < HEADS_PER_GROUP, cbj[0], cbj[1])
        valid = tpos >= j
        decay = jnp.exp(jnp.where(valid, acum - back(acum, j), 0.0))
        coefs.append(jnp.where(valid, cb_l * decay * back(dt, j), 0.0))
    coefs.append(jnp.exp(acum))
    coefs.append(dt * jnp.exp(a_last - acum))
    stack = jnp.concatenate(coefs, axis=0)
    hi = stack.astype(BF16)
    lo = (stack - hi.astype(F32)).astype(BF16)
    wide = _dot(hi, expand_ref[...]) + _dot(lo, expand_ref[...])
    y = dsk_ref[...] * xs
    for j in range(seq):
        y = y + wide[j * rows:(j + 1) * rows, :] * back(xs, j)
    e_wide = wide[seq * rows:(seq + 1) * rows, :]
    wx = (xs * wide[(seq + 1) * rows:(seq + 2) * rows, :]).astype(BF16)
    dec_rows = jnp.exp(a_last)

    q = col(OFF_Q, MEM_WIDTH)
    cmb = cm.astype(BF16)
    bmb = bm.astype(BF16)
    for b in range(bblk):
        rs = slice(b * seq, (b + 1) * seq)
        att_sc[rs, :] = _attention(q[rs, :], k_ref[b].astype(BF16), v_ref[b].astype(BF16))
        h0 = ssm_ref[b]
        h0b = h0.astype(BF16)
        dec_sc[...] = jnp.broadcast_to(dec_rows[b * seq + seq - 1:b * seq + seq, :], (LANES, DT_PAD)).T
        for g in range(SSM_GROUPS):
            gs = slice(g * SSM_STATE, (g + 1) * SSM_STATE)
            ws_ = slice(g * GROUP_WIDTH, (g + 1) * GROUP_WIDTH)
            yoff_sc[rs, ws_] = _dot_nt(cmb[rs, gs], h0b[ws_, :])
            upd = _dot_tn(wx[rs, ws_], bmb[rs, gs])
            for kk in range(HEADS_PER_GROUP):
                k = g * HEADS_PER_GROUP + kk
                hs = slice(k * SSM_HEAD_DIM, (k + 1) * SSM_HEAD_DIM)
                ssm_out_ref[b, hs, :] = (h0[hs, :] * dec_sc[k:k + 1, :]
                                         + upd[kk * SSM_HEAD_DIM:(kk + 1) * SSM_HEAD_DIM, :])

    y = (y + yoff_sc[...] * e_wide) * _silu(col(OFF_Z, SSM_WIDTH))
    mix_ref[:, GM_WIDTH:GM_WIDTH + SSM_WIDTH] = _group_rmsnorm(y, sng_ref[...]).astype(BF16)
    mix_ref[:, GM_WIDTH + SSM_WIDTH:MIX_WIDTH] = (att_sc[...] * _silu(col(OFF_MGATE, MEM_WIDTH))).astype(BF16)


def _sample_mix(proj, conv_state, ssm_state, mem_k, mem_v, gmg, gmb, gcoef, gbias, cw, cb, dtb, alog,
                dsk, sng, expand, seq):
    nb = conv_state.shape[0]
    bblk = SAMPLE_BATCH_BLOCK
    rows = bblk * seq
    assert nb % bblk == 0 and rows % (2 * SUBLANES) == 0

    def const(shape):
        return pl.BlockSpec(shape, lambda i: (0,) * len(shape))

    in_specs = [
        pl.BlockSpec((rows, IN_PACKED), lambda i: (i, 0)),
        pl.BlockSpec((bblk, CONV_WIDTH - 1, CONV_DIM), lambda i: (i, 0, 0)),
        pl.BlockSpec((bblk, SSM_WIDTH, SSM_STATE), lambda i: (i, 0, 0)),
        pl.BlockSpec((bblk, MEM_LEN, MEM_WIDTH), lambda i: (i, 0, 0)),
        pl.BlockSpec((bblk, MEM_LEN, MEM_WIDTH), lambda i: (i, 0, 0)),
        const((1, GM_WIDTH)), const((1, GM_WIDTH)),
        const((seq, rows, GM_WIDTH)), const((rows, GM_WIDTH)),
        const((CONV_WIDTH, CONV_DIM)), const((1, CONV_DIM)),
        const((1, DT_PAD)), const((1, DT_PAD)), const((1, SSM_WIDTH)), const((1, SSM_WIDTH)),
        const((DT_PAD, SSM_WIDTH)),
    ]
    out_specs = [
        pl.BlockSpec((rows, MIX_WIDTH), lambda i: (i, 0)),
        pl.BlockSpec((bblk, SSM_WIDTH, SSM_STATE), lambda i: (i, 0, 0)),
        pl.BlockSpec((bblk, CONV_WIDTH - 1, CONV_DIM), lambda i: (i, 0, 0)),
        pl.BlockSpec((rows, GM_WIDTH), lambda i: (i, 0)),
    ]
    out_shape = [
        jax.ShapeDtypeStruct((nb * seq, MIX_WIDTH), BF16),
        jax.ShapeDtypeStruct((nb, SSM_WIDTH, SSM_STATE), F32),
        jax.ShapeDtypeStruct((nb, CONV_WIDTH - 1, CONV_DIM), F32),
        jax.ShapeDtypeStruct((nb * seq, GM_WIDTH), F32),
    ]
    scratch = [
        pltpu.VMEM((bblk, SUBLANES, CONV_DIM), F32),
        pltpu.VMEM((rows, CONV_DIM), F32),
        pltpu.VMEM((rows, MEM_WIDTH), F32),
        pltpu.VMEM((rows, SSM_WIDTH), F32),
        pltpu.VMEM((DT_PAD, LANES), F32),
    ]
    return pl.pallas_call(
        functools.partial(_sample_mix_kernel, bblk=bblk, seq=seq),
        grid=(nb // bblk,),
        in_specs=in_specs,
        out_specs=out_specs,
        out_shape=out_shape,
        scratch_shapes=scratch,
        compiler_params=pltpu.CompilerParams(dimension_semantics=("arbitrary",),
                                             vmem_limit_bytes=VMEM_LIMIT),
        name="sample_mix",
    )(proj, conv_state, ssm_state, mem_k, mem_v, gmg, gmb, gcoef, gbias, cw, cb, dtb, alog, dsk, sng, expand)


def _sample_out_kernel(mix_ref, x_ref, w_ref, g_ref, o_ref):
    out = x_ref[...] + _dot(mix_ref[...], w_ref[...])
    o_ref[...] = _rms(out, g_ref[...])


def _sample_out(mix, x, w_out, g):
    m = x.shape[0]
    full = lambda shape: pl.BlockSpec(shape, lambda i: (0,) * len(shape))
    return pl.pallas_call(
        _sample_out_kernel,
        grid=(1,),
        in_specs=[full((m, MIX_WIDTH)), full((m, D_MODEL)), full((MIX_WIDTH, D_MODEL)), full((1, D_MODEL))],
        out_specs=full((m, D_MODEL)),
        out_shape=jax.ShapeDtypeStruct((m, D_MODEL), F32),
        compiler_params=pltpu.CompilerParams(dimension_semantics=("arbitrary",),
                                             vmem_limit_bytes=VMEM_LIMIT),
        name="sample_out",
    )(mix, x, w_out, g)


def _pack_w_in(w):
    sizes = (2 * GM_WIDTH, GM_WIDTH, SSM_WIDTH, CONV_DIM, SSM_HEADS, MEM_WIDTH, MEM_WIDTH)
    points = [sum(sizes[:i]) for i in range(1, len(sizes))]
    uv, gate, z, xbc, dt, q, mgate = jnp.split(w, points, axis=1)
    pad = jnp.zeros((D_MODEL, IN_PACKED - OFF_DT - SSM_HEADS), w.dtype)
    return jnp.concatenate([uv, gate, z, xbc, q, mgate, dt, pad], axis=1).astype(BF16)


def _pad_heads(a):
    return jnp.pad(a.astype(F32), (0, DT_PAD - SSM_HEADS)).reshape(1, DT_PAD)


def kernel(x_prompt, x_sample, mem_prompt, state_ssm, state_conv, cache_mem_k, cache_mem_v, norm_g, w_in,
           gm_norm_g, gm_norm_b, gm_w_spatial, gm_b_spatial, conv_w, conv_b, dt_bias, a_log, d_skip,
           ssm_norm_g, mem_norm_g, w_mem_k, w_mem_v, w_out, final_norm_g):
    assert norm_g.shape[0] == 1, "single layer"
    bp, seq_p, _ = x_prompt.shape
    bs, seq_s, _ = x_sample.shape
    row = lambda a: a.reshape(1, -1).astype(F32)

    w_in_p = _pack_w_in(w_in[0])
    w_out_b = w_out[0].astype(BF16)
    ng, gmg, gmb = row(norm_g[0]), row(gm_norm_g[0]), row(gm_norm_b[0])
    cw, cb = conv_w[0].astype(F32), row(conv_b[0])
    dtb, alog = _pad_heads(dt_bias[0]), _pad_heads(a_log[0])
    dsk = row(jnp.repeat(d_skip[0], SSM_HEAD_DIM))
    sng, fg = row(ssm_norm_g[0]), row(final_norm_g)
    w_sp = gm_w_spatial[0]
    tril_p = jnp.tril(jnp.ones((CHUNK, CHUNK), bool))
    ws_p = jnp.where(tril_p, w_sp, 0).astype(BF16)
    bsf_p = jnp.repeat(gm_b_spatial[0].T, GM_HEAD_DIM, axis=1).astype(F32)

    mk, mv, mkb, mvb = _memory_kv(mem_prompt, row(mem_norm_g[0]), w_mem_k[0].astype(BF16),
                                  w_mem_v[0].astype(BF16))
    y_p, ssm_p, conv_p = _prompt_layer(x_prompt, mkb, mvb, w_in_p, w_out_b, ng, gmg, gmb, ws_p, bsf_p,
                                       cw, cb, dtb, alog, dsk, sng, fg)

    rows = SAMPLE_BATCH_BLOCK * seq_s
    tpos = jnp.arange(rows) % seq_s
    gcoef = jnp.stack([
        jnp.where((tpos >= j)[:, None],
                  jnp.repeat(w_sp[:, tpos, jnp.maximum(tpos - j, 0)].T, GM_HEAD_DIM, axis=1), 0.0)
        for j in range(seq_s)]).astype(F32)
    gbias = jnp.repeat(gm_b_spatial[0][:, tpos].T, GM_HEAD_DIM, axis=1).astype(F32)
    expand = (jnp.arange(DT_PAD)[:, None] == (jnp.arange(SSM_WIDTH) // SSM_HEAD_DIM)[None, :]).astype(BF16)

    xs2 = x_sample.reshape(bs * seq_s, D_MODEL)
    proj_s = _sample_proj(xs2, ng, w_in_p)
    mix_s, ssm_s, conv_s, gv_s = _sample_mix(
        proj_s, state_conv[0], state_ssm[0].reshape(bs, SSM_WIDTH, SSM_STATE),
        cache_mem_k[0].reshape(bs, MEM_LEN, MEM_WIDTH), cache_mem_v[0].reshape(bs, MEM_LEN, MEM_WIDTH),
        gmg, gmb, gcoef, gbias, cw, cb, dtb, alog, dsk, sng, expand, seq_s)
    y_s = _sample_out(mix_s, xs2, w_out_b, fg)

    return (y_p,
            y_s.reshape(bs, seq_s, D_MODEL),
            ssm_p.reshape(1, bp, SSM_HEADS, SSM_HEAD_DIM, SSM_STATE),
            conv_p[None],
            mk.reshape(1, bp, MEM_LEN, MEM_HEADS, MEM_HEAD_DIM),
            mv.reshape(1, bp, MEM_LEN, MEM_HEADS, MEM_HEAD_DIM),
            ssm_s.reshape(1, bs, SSM_HEADS, SSM_HEAD_DIM, SSM_STATE),
            conv_s[None],
            gv_s.reshape(1, bs, seq_s, GM_WIDTH))
```

```python
import functools
import math

import jax
import jax.numpy as jnp
from jax import lax
from jax.experimental import pallas as pl
from jax.experimental.pallas import tpu as pltpu

F32 = jnp.float32
BF16 = jnp.bfloat16

D_MODEL = 1024
GM_WIDTH = 1024
GM_HEADS = 8
GM_HEAD_DIM = 128
CHUNK = 128
SSM_WIDTH = 1024
SSM_HEADS = 16
SSM_HEAD_DIM = 64
SSM_GROUPS = 2
SSM_STATE = 128
HEADS_PER_GROUP = SSM_HEADS // SSM_GROUPS
GROUP_WIDTH = SSM_WIDTH // SSM_GROUPS
CONV_WIDTH = 4
CONV_DIM = SSM_WIDTH + 2 * SSM_GROUPS * SSM_STATE
MEM_LEN = 256
MEM_HEADS = 4
MEM_HEAD_DIM = 256
MEM_WIDTH = 1024
MIX_WIDTH = GM_WIDTH + SSM_WIDTH + MEM_WIDTH
EPS = 1e-6

LANES = 128
SUBLANES = 8

OFF_U = 0
OFF_V = OFF_U + GM_WIDTH
OFF_GATE = OFF_V + GM_WIDTH
OFF_Z = OFF_GATE + GM_WIDTH
OFF_XBC = OFF_Z + SSM_WIDTH
OFF_Q = OFF_XBC + CONV_DIM
OFF_MGATE = OFF_Q + MEM_WIDTH
OFF_DT = OFF_MGATE + MEM_WIDTH
DT_PAD = LANES
IN_PACKED = 8192
PROJ_BLOCK = 1024

PROMPT_TILE = 256
SAMPLE_BATCH_BLOCK = 4
VMEM_LIMIT = 56 * 1024 * 1024


def _rms(x, g):
    return x * lax.rsqrt(jnp.mean(x * x, axis=-1, keepdims=True) + EPS) * g


def _gelu(x):
    return 0.5 * x * (1.0 + lax.erf(x * math.sqrt(0.5)))


def _silu(x):
    return x * jax.nn.sigmoid(x)


def _softplus(x):
    return jnp.maximum(x, 0.0) + jnp.log1p(jnp.exp(-jnp.abs(x)))


def _layernorm(x, g, b):
    mu = jnp.mean(x, axis=-1, keepdims=True)
    xc = x - mu
    var = jnp.mean(xc * xc, axis=-1, keepdims=True)
    return xc * lax.rsqrt(var + EPS) * g + b


def _dot(a, b):
    return jnp.dot(a, b, preferred_element_type=F32)


def _dot_nt(a, b):
    return lax.dot_general(a, b, (((1,), (1,)), ((), ())), preferred_element_type=F32)


def _dot_tn(a, b):
    return lax.dot_general(a, b, (((0,), (0,)), ((), ())), preferred_element_type=F32)


def _group_rmsnorm(y, g):
    halves = []
    for i in range(SSM_GROUPS):
        yg = y[:, i * GROUP_WIDTH:(i + 1) * GROUP_WIDTH]
        halves.append(yg * lax.rsqrt(jnp.mean(yg * yg, axis=-1, keepdims=True) + EPS))
    return jnp.concatenate(halves, axis=-1) * g


def _memory_kv_kernel(mem_ref, g_ref, wk_ref, wv_ref, k_ref, v_ref, kb_ref, vb_ref):
    m = _rms(mem_ref[0], g_ref[...]).astype(BF16)
    k = _dot(m, wk_ref[...])
    v = _dot(m, wv_ref[...])
    for h in range(MEM_HEADS):
        k_ref[0, 0, :, h, :] = k[:, _head_slice(h)]
        v_ref[0, 0, :, h, :] = v[:, _head_slice(h)]
    kb_ref[0] = k.astype(BF16)
    vb_ref[0] = v.astype(BF16)


def _memory_kv(mem, g, wk, wv):
    b = mem.shape[0]
    blk = pl.BlockSpec((1, MEM_LEN, D_MODEL), lambda i: (i, 0, 0))
    blk5 = pl.BlockSpec((1, 1, MEM_LEN, MEM_HEADS, MEM_HEAD_DIM), lambda i: (0, i, 0, 0, 0))
    const = lambda shape: pl.BlockSpec(shape, lambda i: (0,) * len(shape))
    return pl.pallas_call(
        _memory_kv_kernel,
        grid=(b,),
        in_specs=[blk, const((1, D_MODEL)), const((D_MODEL, MEM_WIDTH)), const((D_MODEL, MEM_WIDTH))],
        out_specs=[blk5, blk5, blk, blk],
        out_shape=[jax.ShapeDtypeStruct((1, b, MEM_LEN, MEM_HEADS, MEM_HEAD_DIM), F32)] * 2
        + [jax.ShapeDtypeStruct((b, MEM_LEN, MEM_WIDTH), BF16)] * 2,
        compiler_params=pltpu.CompilerParams(dimension_semantics=("arbitrary",),
                                             vmem_limit_bytes=VMEM_LIMIT),
        name="memory_kv",
    )(mem, g, wk, wv)


def _head_slice(h):
    return slice(h * MEM_HEAD_DIM, (h + 1) * MEM_HEAD_DIM)


def _attention(q, k_head, v_head):
    qb = (q * (MEM_HEAD_DIM ** -0.5)).astype(BF16)
    outs = []
    for h in range(MEM_HEADS):
        s = _dot_nt(qb[:, _head_slice(h)], k_head(h))
        p = jnp.exp(s - jnp.max(s, axis=-1, keepdims=True))
        denom = jnp.sum(p, axis=-1, keepdims=True)
        outs.append(_dot(p.astype(BF16), v_head(h)) / denom)
    return jnp.concatenate(outs, axis=-1)


def _attention_interleaved(q, k2, v2):
    t = q.shape[0]
    qb = (q * (MEM_HEAD_DIM ** -0.5)).astype(BF16)
    q2 = jnp.concatenate([qb[:, _head_slice(h)] for h in range(MEM_HEADS)], axis=0)
    s = _dot_nt(q2, k2)
    row_head = lax.broadcasted_iota(jnp.int32, s.shape, 0) // t
    col_head = lax.broadcasted_iota(jnp.int32, s.shape, 1) % MEM_HEADS
    s = jnp.where(row_head == col_head, s, -jnp.inf)
    p = jnp.exp(s - jnp.max(s, axis=-1, keepdims=True))
    denom = jnp.sum(p, axis=-1, keepdims=True)
    return _dot(p.astype(BF16), v2) / denom


def _prompt_kernel(x_ref, kb_ref, vb_ref, win_ref, wout_ref, ng_ref, gmg_ref, gmb_ref,
                   ws_ref, bsf_ref, cw_ref, cb_ref, dtb_ref, alog_ref, dsk_ref, sng_ref, fg_ref,
                   y_ref, ssm_ref, conv_ref,
                   hn_sc, mix_sc, ug_sc, v_sc, xp_sc, zs_sc, xs_sc, xsb_sc, bm_sc, cm_sc, dt_sc,
                   y_sc, ht_sc, *, tile):
    t = pl.program_id(1)
    nt = pl.num_programs(1)
    nch = tile // CHUNK

    @pl.when(t == 0)
    def _():
        ht_sc[...] = jnp.zeros_like(ht_sc)
        xp_sc[0:SUBLANES, :] = jnp.zeros((SUBLANES, CONV_DIM), F32)

    x = x_ref[0]
    hn = _rms(x, ng_ref[...]).astype(BF16)
    hn_sc[...] = hn

    def proj(off, width):
        return _dot(hn_sc[...], win_ref[:, off:off + width])

    u = _gelu(proj(OFF_U, GM_WIDTH))
    ug_sc[...] = u * _silu(proj(OFF_GATE, GM_WIDTH))
    v = _layernorm(_gelu(proj(OFF_V, GM_WIDTH)), gmg_ref[...], gmb_ref[...])
    v_sc[...] = v.astype(BF16)

    att = _attention(proj(OFF_Q, MEM_WIDTH), lambda h: kb_ref[0, :, _head_slice(h)],
                     lambda h: vb_ref[0, :, _head_slice(h)])
    mix_sc[:, GM_WIDTH + SSM_WIDTH:MIX_WIDTH] = (att * _silu(proj(OFF_MGATE, MEM_WIDTH))).astype(BF16)

    zs_sc[...] = _silu(proj(OFF_Z, SSM_WIDTH))
    xp_sc[SUBLANES:SUBLANES + tile, :] = proj(OFF_XBC, CONV_DIM)
    acc = cb_ref[...]
    for j in range(CONV_WIDTH):
        lo = SUBLANES - (CONV_WIDTH - 1) + j
        acc = acc + cw_ref[j:j + 1, :] * xp_sc[lo:lo + tile, :]
    xbc = _silu(acc)
    xs = xbc[:, :SSM_WIDTH]
    xs_sc[...] = xs
    xsb_sc[...] = xs.astype(BF16)
    bm_sc[...] = xbc[:, SSM_WIDTH:SSM_WIDTH + SSM_GROUPS * SSM_STATE]
    cm_sc[...] = xbc[:, SSM_WIDTH + SSM_GROUPS * SSM_STATE:]
    tail = xp_sc[tile + SUBLANES - (CONV_WIDTH - 1):tile + SUBLANES, :]
    xp_sc[SUBLANES - (CONV_WIDTH - 1):SUBLANES, :] = tail

    @pl.when(t == nt - 1)
    def _():
        conv_ref[0] = tail

    dt_sc[...] = _softplus(proj(OFF_DT, DT_PAD) + dtb_ref[...])
    a_row = -jnp.exp(alog_ref[...])

    row = lax.broadcasted_iota(jnp.int32, (CHUNK, CHUNK), 0)
    col = lax.broadcasted_iota(jnp.int32, (CHUNK, CHUNK), 1)
    causal = row >= col
    tril = causal.astype(F32)
    head_lane = col < SSM_HEADS
    first_half = col < SSM_HEAD_DIM

    def chunk_body(c, carry):
        r = pl.ds(pl.multiple_of(c * CHUNK, CHUNK), CHUNK)

        for h in range(GM_HEADS):
            hs = slice(h * GM_HEAD_DIM, (h + 1) * GM_HEAD_DIM)
            mixed = _dot(ws_ref[h], v_sc[r, hs]) + bsf_ref[:, hs]
            mix_sc[r, hs] = (ug_sc[r, hs] * mixed).astype(BF16)

        dt = dt_sc[r, :]
        adt = jnp.where(head_lane, dt * a_row, 0.0)
        acum = jnp.dot(tril, adt, precision=lax.Precision.HIGHEST, preferred_element_type=F32)
        acum_t = acum.T
        dt_t = dt.T
        to_end_t = jnp.exp(acum_t[:, CHUNK - 1:CHUNK] - acum_t) * dt_t
        for g in range(SSM_GROUPS):
            gs = slice(g * SSM_STATE, (g + 1) * SSM_STATE)
            bm_g = bm_sc[r, gs]
            cm_g = cm_sc[r, gs]
            cb_g = _dot_nt(cm_g.astype(BF16), bm_g.astype(BF16))
            bm_t = bm_g.T
            for kk in range(0, HEADS_PER_GROUP, 2):
                k0 = g * HEADS_PER_GROUP + kk
                ps = slice(k0 * SSM_HEAD_DIM, (k0 + 2) * SSM_HEAD_DIM)
                lhs_parts, b_parts, decs = [], [], []
                for k in (k0, k0 + 1):
                    a_col = jnp.broadcast_to(acum[:, k:k + 1], (CHUNK, CHUNK))
                    diff = a_col - acum_t[k:k + 1, :]
                    decay = jnp.exp(jnp.where(causal, diff, -jnp.inf))
                    lhs_parts.append((cb_g * decay * dt_t[k:k + 1, :]).astype(BF16))
                    lhs_parts.append((cm_g * jnp.exp(a_col)).astype(BF16))
                    b_parts.append((bm_t * to_end_t[k:k + 1, :]).astype(BF16))
                    decs.append(jnp.exp(a_col[CHUNK - 1:CHUNK, :]))
                xs_pair = xs_sc[r, ps]
                h_pair = ht_sc[:, ps]
                zero = jnp.zeros_like(xs_pair)
                xs_lo = jnp.where(first_half, xs_pair, zero).astype(BF16)
                xs_hi = jnp.where(first_half, zero, xs_pair).astype(BF16)
                h_lo = jnp.where(first_half, h_pair, zero).astype(BF16)
                h_hi = jnp.where(first_half, zero, h_pair).astype(BF16)
                lhs = jnp.concatenate(lhs_parts, axis=1)
                rhs = jnp.concatenate([xs_lo, h_lo, xs_hi, h_hi], axis=0)
                y_sc[:, ps] = _dot(lhs, rhs)
                upd = _dot(jnp.concatenate(b_parts, axis=1), jnp.concatenate([xs_lo, xs_hi], axis=0))
                dec = jnp.where(first_half[0:1, :], decs[0], decs[1])
                ht_sc[:, ps] = h_pair * dec + upd

        xs_c = xs_sc[r, :]
        y = (y_sc[...] + dsk_ref[...] * xs_c) * zs_sc[r, :]
        mix_sc[r, GM_WIDTH:GM_WIDTH + SSM_WIDTH] = _group_rmsnorm(y, sng_ref[...]).astype(BF16)
        return carry

    lax.fori_loop(0, nch, chunk_body, 0)

    @pl.when(t == nt - 1)
    def _():
        ssm_ref[0] = ht_sc[...].T

    out = x + _dot(mix_sc[...], wout_ref[...])
    y_ref[0] = _rms(out, fg_ref[...])


def _prompt_layer(x, kb, vb, w_in, w_out, ng, gmg, gmb, ws, bsf, cw, cb, dtb, alog, dsk, sng, fg):
    b, seq, _ = x.shape
    tile = PROMPT_TILE
    nt = seq // tile
    assert seq % tile == 0 and tile % CHUNK == 0

    def const(shape):
        return pl.BlockSpec(shape, lambda i, j: (0,) * len(shape), pipeline_mode=pl.Buffered(1))

    in_specs = [
        pl.BlockSpec((1, tile, D_MODEL), lambda i, j: (i, j, 0)),
        pl.BlockSpec((1, MEM_LEN, MEM_WIDTH), lambda i, j: (i, 0, 0)),
        pl.BlockSpec((1, MEM_LEN, MEM_WIDTH), lambda i, j: (i, 0, 0)),
        const((D_MODEL, IN_PACKED)),
        const((MIX_WIDTH, D_MODEL)),
        const((1, D_MODEL)), const((1, GM_WIDTH)), const((1, GM_WIDTH)),
        const((GM_HEADS, CHUNK, CHUNK)), const((CHUNK, GM_WIDTH)),
        const((CONV_WIDTH, CONV_DIM)), const((1, CONV_DIM)),
        const((1, DT_PAD)), const((1, DT_PAD)), const((1, SSM_WIDTH)), const((1, SSM_WIDTH)),
        const((1, D_MODEL)),
    ]
    out_specs = [
        pl.BlockSpec((1, tile, D_MODEL), lambda i, j: (i, j, 0)),
        pl.BlockSpec((1, SSM_WIDTH, SSM_STATE), lambda i, j: (i, 0, 0)),
        pl.BlockSpec((1, CONV_WIDTH - 1, CONV_DIM), lambda i, j: (i, 0, 0)),
    ]
    out_shape = [
        jax.ShapeDtypeStruct((b, seq, D_MODEL), F32),
        jax.ShapeDtypeStruct((b, SSM_WIDTH, SSM_STATE), F32),
        jax.ShapeDtypeStruct((b, CONV_WIDTH - 1, CONV_DIM), F32),
    ]
    scratch = [
        pltpu.VMEM((tile, D_MODEL), BF16),
        pltpu.VMEM((tile, MIX_WIDTH), BF16),
        pltpu.VMEM((tile, GM_WIDTH), F32),
        pltpu.VMEM((tile, GM_WIDTH), BF16),
        pltpu.VMEM((tile + SUBLANES, CONV_DIM), F32),
        pltpu.VMEM((tile, SSM_WIDTH), F32),
        pltpu.VMEM((tile, SSM_WIDTH), F32),
        pltpu.VMEM((tile, SSM_WIDTH), BF16),
        pltpu.VMEM((tile, SSM_GROUPS * SSM_STATE), F32),
        pltpu.VMEM((tile, SSM_GROUPS * SSM_STATE), F32),
        pltpu.VMEM((tile, DT_PAD), F32),
        pltpu.VMEM((CHUNK, SSM_WIDTH), F32),
        pltpu.VMEM((SSM_STATE, SSM_WIDTH), F32),
    ]
    return pl.pallas_call(
        functools.partial(_prompt_kernel, tile=tile),
        grid=(b, nt),
        in_specs=in_specs,
        out_specs=out_specs,
        out_shape=out_shape,
        scratch_shapes=scratch,
        compiler_params=pltpu.CompilerParams(dimension_semantics=("arbitrary", "arbitrary"),
                                             vmem_limit_bytes=VMEM_LIMIT),
        name="prompt_layer",
    )(x, kb, vb, w_in, w_out, ng, gmg, gmb, ws, bsf, cw, cb, dtb, alog, dsk, sng, fg)


def _sample_proj_kernel(x_ref, g_ref, w_ref, o_ref):
    hn = _rms(x_ref[...], g_ref[...]).astype(BF16)
    o_ref[...] = _dot(hn, w_ref[...])


def _sample_proj(x, g, w_in):
    m = x.shape[0]
    return pl.pallas_call(
        _sample_proj_kernel,
        grid=(IN_PACKED // PROJ_BLOCK,),
        in_specs=[pl.BlockSpec((m, D_MODEL), lambda j: (0, 0)),
                  pl.BlockSpec((1, D_MODEL), lambda j: (0, 0)),
                  pl.BlockSpec((D_MODEL, PROJ_BLOCK), lambda j: (0, j))],
        out_specs=pl.BlockSpec((m, PROJ_BLOCK), lambda j: (0, j)),
        out_shape=jax.ShapeDtypeStruct((m, IN_PACKED), F32),
        compiler_params=pltpu.CompilerParams(dimension_semantics=("arbitrary",),
                                             vmem_limit_bytes=VMEM_LIMIT),
        name="sample_proj",
    )(x, g, w_in)


def _sample_mix_kernel(p_ref, cst_ref, ssm_ref, k_ref, v_ref, gmg_ref, gmb_ref, gcoef_ref, gbias_ref,
                       cw_ref, cb_ref, dtb_ref, alog_ref, dsk_ref, sng_ref, expand_ref,
                       mix_ref, ssm_out_ref, conv_out_ref, gv_ref,
                       xp_sc, xbc_sc, att_sc, yoff_sc, dec_sc, *, bblk, seq):
    rows = bblk * seq

    def col(off, width):
        return p_ref[:, off:off + width]

    tpos = lax.broadcasted_iota(jnp.int32, (rows, 1), 0) % seq

    def back(a, j):
        return a if j == 0 else pltpu.roll(a, j, 0)

    u = _gelu(col(OFF_U, GM_WIDTH))
    v = _layernorm(_gelu(col(OFF_V, GM_WIDTH)), gmg_ref[...], gmb_ref[...])
    gv_ref[...] = v
    mixed = gbias_ref[...]
    for j in range(seq):
        mixed = mixed + gcoef_ref[j] * back(v, j)
    mix_ref[:, 0:GM_WIDTH] = (u * mixed * _silu(col(OFF_GATE, GM_WIDTH))).astype(BF16)

    xbc_raw = col(OFF_XBC, CONV_DIM)
    for b in range(bblk):
        xp_sc[b, 0:CONV_WIDTH - 1, :] = cst_ref[b]
        xp_sc[b, CONV_WIDTH - 1:CONV_WIDTH - 1 + seq, :] = xbc_raw[b * seq:(b + 1) * seq, :]
    for b in range(bblk):
        acc = jnp.broadcast_to(cb_ref[...], (seq, CONV_DIM))
        for j in range(CONV_WIDTH):
            acc = acc + cw_ref[j:j + 1, :] * xp_sc[b, j:j + seq, :]
        xbc_sc[b * seq:(b + 1) * seq, :] = _silu(acc)
        conv_out_ref[b] = xp_sc[b, seq:seq + CONV_WIDTH - 1, :]
    xs = xbc_sc[:, 0:SSM_WIDTH]
    bm = xbc_sc[:, SSM_WIDTH:SSM_WIDTH + SSM_GROUPS * SSM_STATE]
    cm = xbc_sc[:, SSM_WIDTH + SSM_GROUPS * SSM_STATE:CONV_DIM]

    lane = lax.broadcasted_iota(jnp.int32, (rows, DT_PAD), 1)
    dt = _softplus(col(OFF_DT, DT_PAD) + dtb_ref[...])
    adt = jnp.where(lane < SSM_HEADS, dt * (-jnp.exp(alog_ref[...])), 0.0)
    acum = adt
    for j in range(1, seq):
        acum = acum + jnp.where(tpos >= j, back(adt, j), 0.0)
    a_last = jnp.zeros_like(acum)
    for j in range(seq):
        a_last = a_last + jnp.where(tpos == seq - 1 - j, acum if j == 0 else pltpu.roll(acum, rows - j, 0), 0.0)
    coefs = []
    for j in range(seq):
        cbj = []
        for g in range(SSM_GROUPS):
            gs = slice(g * SSM_STATE, (g + 1) * SSM_STATE)
            cbj.append(jnp.sum(cm[:, gs] * back(bm[:, gs], j), axis=-1, keepdims=True))
        cb_l = jnp.where(lane < HEADS_PER_GROUP, cbj[0], cbj[1])
        valid = tpos >= j
        decay = jnp.exp(jnp.where(valid, acum - back(acum, j), 0.0))
        coefs.append(jnp.where(valid, cb_l * decay * back(dt, j), 0.0))
    coefs.append(jnp.exp(acum))
    coefs.append(dt * jnp.exp(a_last - acum))
    stack = jnp.concatenate(coefs, axis=0)
    hi = stack.astype(BF16)
    lo = (stack - hi.astype(F32)).astype(BF16)
    wide = _dot(hi, expand_ref[...]) + _dot(lo, expand_ref[...])
    y = dsk_ref[...] * xs
    for j in range(seq):
        y = y + wide[j * rows:(j + 1) * rows, :] * back(xs, j)
    e_wide = wide[seq * rows:(seq + 1) * rows, :]
    wx = (xs * wide[(seq + 1) * rows:(seq + 2) * rows, :]).astype(BF16)
    dec_rows = jnp.exp(a_last)

    q = col(OFF_Q, MEM_WIDTH)
    cmb = cm.astype(BF16)
    bmb = bm.astype(BF16)
    for b in range(bblk):
        rs = slice(b * seq, (b + 1) * seq)
        att = _attention_interleaved(
            q[rs, :],
            k_ref[0, b].reshape(MEM_LEN * MEM_HEADS, MEM_HEAD_DIM).astype(BF16),
            v_ref[0, b].reshape(MEM_LEN * MEM_HEADS, MEM_HEAD_DIM).astype(BF16))
        for h in range(MEM_HEADS):
            att_sc[rs, _head_slice(h)] = att[h * seq:(h + 1) * seq, :]
        h0 = ssm_ref[b]
        h0b = h0.astype(BF16)
        dec_sc[...] = jnp.broadcast_to(dec_rows[b * seq + seq - 1:b * seq + seq, :], (LANES, DT_PAD)).T
        for g in range(SSM_GROUPS):
            gs = slice(g * SSM_STATE, (g + 1) * SSM_STATE)
            ws_ = slice(g * GROUP_WIDTH, (g + 1) * GROUP_WIDTH)
            yoff_sc[rs, ws_] = _dot_nt(cmb[rs, gs], h0b[ws_, :])
            upd = _dot_tn(wx[rs, ws_], bmb[rs, gs])
            for kk in range(HEADS_PER_GROUP):
                k = g * HEADS_PER_GROUP + kk
                hs = slice(k * SSM_HEAD_DIM, (k + 1) * SSM_HEAD_DIM)
                ssm_out_ref[b, hs, :] = (h0[hs, :] * dec_sc[k:k + 1, :]
                                         + upd[kk * SSM_HEAD_DIM:(kk + 1) * SSM_HEAD_DIM, :])

    y = (y + yoff_sc[...] * e_wide) * _silu(col(OFF_Z, SSM_WIDTH))
    mix_ref[:, GM_WIDTH:GM_WIDTH + SSM_WIDTH] = _group_rmsnorm(y, sng_ref[...]).astype(BF16)
    mix_ref[:, GM_WIDTH + SSM_WIDTH:MIX_WIDTH] = (att_sc[...] * _silu(col(OFF_MGATE, MEM_WIDTH))).astype(BF16)


def _sample_mix(proj, conv_state, ssm_state, mem_k, mem_v, gmg, gmb, gcoef, gbias, cw, cb, dtb, alog,
                dsk, sng, expand, seq):
    nb = conv_state.shape[0]
    bblk = SAMPLE_BATCH_BLOCK
    rows = bblk * seq
    assert nb % bblk == 0 and rows % (2 * SUBLANES) == 0

    def const(shape):
        return pl.BlockSpec(shape, lambda i: (0,) * len(shape))

    in_specs = [
        pl.BlockSpec((rows, IN_PACKED), lambda i: (i, 0)),
        pl.BlockSpec((bblk, CONV_WIDTH - 1, CONV_DIM), lambda i: (i, 0, 0)),
        pl.BlockSpec((bblk, SSM_WIDTH, SSM_STATE), lambda i: (i, 0, 0)),
        pl.BlockSpec((1, bblk, MEM_LEN, MEM_HEADS, MEM_HEAD_DIM), lambda i: (0, i, 0, 0, 0)),
        pl.BlockSpec((1, bblk, MEM_LEN, MEM_HEADS, MEM_HEAD_DIM), lambda i: (0, i, 0, 0, 0)),
        const((1, GM_WIDTH)), const((1, GM_WIDTH)),
        const((seq, rows, GM_WIDTH)), const((rows, GM_WIDTH)),
        const((CONV_WIDTH, CONV_DIM)), const((1, CONV_DIM)),
        const((1, DT_PAD)), const((1, DT_PAD)), const((1, SSM_WIDTH)), const((1, SSM_WIDTH)),
        const((DT_PAD, SSM_WIDTH)),
    ]
    out_specs = [
        pl.BlockSpec((rows, MIX_WIDTH), lambda i: (i, 0)),
        pl.BlockSpec((bblk, SSM_WIDTH, SSM_STATE), lambda i: (i, 0, 0)),
        pl.BlockSpec((bblk, CONV_WIDTH - 1, CONV_DIM), lambda i: (i, 0, 0)),
        pl.BlockSpec((rows, GM_WIDTH), lambda i: (i, 0)),
    ]
    out_shape = [
        jax.ShapeDtypeStruct((nb * seq, MIX_WIDTH), BF16),
        jax.ShapeDtypeStruct((nb, SSM_WIDTH, SSM_STATE), F32),
        jax.ShapeDtypeStruct((nb, CONV_WIDTH - 1, CONV_DIM), F32),
        jax.ShapeDtypeStruct((nb * seq, GM_WIDTH), F32),
    ]
    scratch = [
        pltpu.VMEM((bblk, SUBLANES, CONV_DIM), F32),
        pltpu.VMEM((rows, CONV_DIM), F32),
        pltpu.VMEM((rows, MEM_WIDTH), F32),
        pltpu.VMEM((rows, SSM_WIDTH), F32),
        pltpu.VMEM((DT_PAD, LANES), F32),
    ]
    return pl.pallas_call(
        functools.partial(_sample_mix_kernel, bblk=bblk, seq=seq),
        grid=(nb // bblk,),
        in_specs=in_specs,
        out_specs=out_specs,
        out_shape=out_shape,
        scratch_shapes=scratch,
        compiler_params=pltpu.CompilerParams(dimension_semantics=("arbitrary",),
                                             vmem_limit_bytes=VMEM_LIMIT),
        name="sample_mix",
    )(proj, conv_state, ssm_state, mem_k, mem_v, gmg, gmb, gcoef, gbias, cw, cb, dtb, alog, dsk, sng, expand)


def _sample_out_kernel(mix_ref, x_ref, w_ref, g_ref, o_ref):
    out = x_ref[...] + _dot(mix_ref[...], w_ref[...])
    o_ref[...] = _rms(out, g_ref[...])


def _sample_out(mix, x, w_out, g):
    m = x.shape[0]
    full = lambda shape: pl.BlockSpec(shape, lambda i: (0,) * len(shape))
    return pl.pallas_call(
        _sample_out_kernel,
        grid=(1,),
        in_specs=[full((m, MIX_WIDTH)), full((m, D_MODEL)), full((MIX_WIDTH, D_MODEL)), full((1, D_MODEL))],
        out_specs=full((m, D_MODEL)),
        out_shape=jax.ShapeDtypeStruct((m, D_MODEL), F32),
        compiler_params=pltpu.CompilerParams(dimension_semantics=("arbitrary",),
                                             vmem_limit_bytes=VMEM_LIMIT),
        name="sample_out",
    )(mix, x, w_out, g)


def _pack_w_in(w):
    sizes = (2 * GM_WIDTH, GM_WIDTH, SSM_WIDTH, CONV_DIM, SSM_HEADS, MEM_WIDTH, MEM_WIDTH)
    points = [sum(sizes[:i]) for i in range(1, len(sizes))]
    uv, gate, z, xbc, dt, q, mgate = jnp.split(w, points, axis=1)
    pad = jnp.zeros((D_MODEL, IN_PACKED - OFF_DT - SSM_HEADS), w.dtype)
    return jnp.concatenate([uv, gate, z, xbc, q, mgate, dt, pad], axis=1).astype(BF16)


def _pad_heads(a):
    return jnp.pad(a.astype(F32), (0, DT_PAD - SSM_HEADS)).reshape(1, DT_PAD)


def kernel(x_prompt, x_sample, mem_prompt, state_ssm, state_conv, cache_mem_k, cache_mem_v, norm_g, w_in,
           gm_norm_g, gm_norm_b, gm_w_spatial, gm_b_spatial, conv_w, conv_b, dt_bias, a_log, d_skip,
           ssm_norm_g, mem_norm_g, w_mem_k, w_mem_v, w_out, final_norm_g):
    assert norm_g.shape[0] == 1, "single layer"
    bp, seq_p, _ = x_prompt.shape
    bs, seq_s, _ = x_sample.shape
    row = lambda a: a.reshape(1, -1).astype(F32)

    w_in_p = _pack_w_in(w_in[0])
    w_out_b = w_out[0].astype(BF16)
    ng, gmg, gmb = row(norm_g[0]), row(gm_norm_g[0]), row(gm_norm_b[0])
    cw, cb = conv_w[0].astype(F32), row(conv_b[0])
    dtb, alog = _pad_heads(dt_bias[0]), _pad_heads(a_log[0])
    dsk = row(jnp.repeat(d_skip[0], SSM_HEAD_DIM))
    sng, fg = row(ssm_norm_g[0]), row(final_norm_g)
    w_sp = gm_w_spatial[0]
    tril_p = jnp.tril(jnp.ones((CHUNK, CHUNK), bool))
    ws_p = jnp.where(tril_p, w_sp, 0).astype(BF16)
    bsf_p = jnp.repeat(gm_b_spatial[0].T, GM_HEAD_DIM, axis=1).astype(F32)

    mk, mv, mkb, mvb = _memory_kv(mem_prompt, row(mem_norm_g[0]), w_mem_k[0].astype(BF16),
                                  w_mem_v[0].astype(BF16))
    y_p, ssm_p, conv_p = _prompt_layer(x_prompt, mkb, mvb, w_in_p, w_out_b, ng, gmg, gmb, ws_p, bsf_p,
                                       cw, cb, dtb, alog, dsk, sng, fg)

    rows = SAMPLE_BATCH_BLOCK * seq_s
    tpos = jnp.arange(rows) % seq_s
    gcoef = jnp.stack([
        jnp.where((tpos >= j)[:, None],
                  jnp.repeat(w_sp[:, tpos, jnp.maximum(tpos - j, 0)].T, GM_HEAD_DIM, axis=1), 0.0)
        for j in range(seq_s)]).astype(F32)
    gbias = jnp.repeat(gm_b_spatial[0][:, tpos].T, GM_HEAD_DIM, axis=1).astype(F32)
    expand = (jnp.arange(DT_PAD)[:, None] == (jnp.arange(SSM_WIDTH) // SSM_HEAD_DIM)[None, :]).astype(BF16)

    xs2 = x_sample.reshape(bs * seq_s, D_MODEL)
    proj_s = _sample_proj(xs2, ng, w_in_p)
    mix_s, ssm_s, conv_s, gv_s = _sample_mix(
        proj_s, state_conv[0], state_ssm[0].reshape(bs, SSM_WIDTH, SSM_STATE),
        cache_mem_k, cache_mem_v,
        gmg, gmb, gcoef, gbias, cw, cb, dtb, alog, dsk, sng, expand, seq_s)
    y_s = _sample_out(mix_s, xs2, w_out_b, fg)

    return (y_p,
            y_s.reshape(bs, seq_s, D_MODEL),
            ssm_p.reshape(1, bp, SSM_HEADS, SSM_HEAD_DIM, SSM_STATE),
            conv_p[None],
            mk,
            mv,
            ssm_s.reshape(1, bs, SSM_HEADS, SSM_HEAD_DIM, SSM_STATE),
            conv_s[None],
            gv_s.reshape(1, bs, seq_s, GM_WIDTH))
```

```python
import functools
import math

import jax
import jax.numpy as jnp
from jax import lax
from jax.experimental import pallas as pl
from jax.experimental.pallas import tpu as pltpu

F32 = jnp.float32
BF16 = jnp.bfloat16

D_MODEL = 1024
GM_WIDTH = 1024
GM_HEADS = 8
GM_HEAD_DIM = 128
CHUNK = 128
SSM_WIDTH = 1024
SSM_HEADS = 16
SSM_HEAD_DIM = 64
SSM_GROUPS = 2
SSM_STATE = 128
HEADS_PER_GROUP = SSM_HEADS // SSM_GROUPS
GROUP_WIDTH = SSM_WIDTH // SSM_GROUPS
CONV_WIDTH = 4
CONV_DIM = SSM_WIDTH + 2 * SSM_GROUPS * SSM_STATE
MEM_LEN = 256
MEM_HEADS = 4
MEM_HEAD_DIM = 256
MEM_WIDTH = 1024
MIX_WIDTH = GM_WIDTH + SSM_WIDTH + MEM_WIDTH
EPS = 1e-6

LANES = 128
SUBLANES = 8

OFF_U = 0
OFF_V = OFF_U + GM_WIDTH
OFF_GATE = OFF_V + GM_WIDTH
OFF_Z = OFF_GATE + GM_WIDTH
OFF_XBC = OFF_Z + SSM_WIDTH
OFF_Q = OFF_XBC + CONV_DIM
OFF_MGATE = OFF_Q + MEM_WIDTH
OFF_DT = OFF_MGATE + MEM_WIDTH
DT_PAD = LANES
IN_PACKED = 8192
PROJ_BLOCK = 1024

PROMPT_TILE = 256
SAMPLE_BATCH_BLOCK = 4
VMEM_LIMIT = 56 * 1024 * 1024


def _rms(x, g):
    return x * lax.rsqrt(jnp.mean(x * x, axis=-1, keepdims=True) + EPS) * g


def _gelu(x):
    return 0.5 * x * (1.0 + lax.erf(x * math.sqrt(0.5)))


def _silu(x):
    return x * jax.nn.sigmoid(x)


def _softplus(x):
    return jnp.maximum(x, 0.0) + jnp.log1p(jnp.exp(-jnp.abs(x)))


def _layernorm(x, g, b):
    mu = jnp.mean(x, axis=-1, keepdims=True)
    xc = x - mu
    var = jnp.mean(xc * xc, axis=-1, keepdims=True)
    return xc * lax.rsqrt(var + EPS) * g + b


def _dot(a, b):
    return jnp.dot(a, b, preferred_element_type=F32)


def _dot_nt(a, b):
    return lax.dot_general(a, b, (((1,), (1,)), ((), ())), preferred_element_type=F32)


def _dot_tn(a, b):
    return lax.dot_general(a, b, (((0,), (0,)), ((), ())), preferred_element_type=F32)


def _group_rmsnorm(y, g):
    halves = []
    for i in range(SSM_GROUPS):
        yg = y[:, i * GROUP_WIDTH:(i + 1) * GROUP_WIDTH]
        halves.append(yg * lax.rsqrt(jnp.mean(yg * yg, axis=-1, keepdims=True) + EPS))
    return jnp.concatenate(halves, axis=-1) * g


def _memory_kv_kernel(mem_ref, g_ref, wk_ref, wv_ref, k_ref, v_ref, kb_ref, vb_ref):
    m = _rms(mem_ref[0], g_ref[...]).astype(BF16)
    k = _dot(m, wk_ref[...])
    v = _dot(m, wv_ref[...])
    for h in range(MEM_HEADS):
        k_ref[0, 0, :, h, :] = k[:, _head_slice(h)]
        v_ref[0, 0, :, h, :] = v[:, _head_slice(h)]
    kb_ref[0] = k.astype(BF16)
    vb_ref[0] = v.astype(BF16)


def _memory_kv(mem, g, wk, wv):
    b = mem.shape[0]
    blk = pl.BlockSpec((1, MEM_LEN, D_MODEL), lambda i: (i, 0, 0))
    blk5 = pl.BlockSpec((1, 1, MEM_LEN, MEM_HEADS, MEM_HEAD_DIM), lambda i: (0, i, 0, 0, 0))
    const = lambda shape: pl.BlockSpec(shape, lambda i: (0,) * len(shape))
    return pl.pallas_call(
        _memory_kv_kernel,
        grid=(b,),
        in_specs=[blk, const((1, D_MODEL)), const((D_MODEL, MEM_WIDTH)), const((D_MODEL, MEM_WIDTH))],
        out_specs=[blk5, blk5, blk, blk],
        out_shape=[jax.ShapeDtypeStruct((1, b, MEM_LEN, MEM_HEADS, MEM_HEAD_DIM), F32)] * 2
        + [jax.ShapeDtypeStruct((b, MEM_LEN, MEM_WIDTH), BF16)] * 2,
        compiler_params=pltpu.CompilerParams(dimension_semantics=("arbitrary",),
                                             vmem_limit_bytes=VMEM_LIMIT),
        name="memory_kv",
    )(mem, g, wk, wv)


def _head_slice(h):
    return slice(h * MEM_HEAD_DIM, (h + 1) * MEM_HEAD_DIM)


def _attention(q, k_head, v_head):
    qb = (q * (MEM_HEAD_DIM ** -0.5)).astype(BF16)
    outs = []
    for h in range(MEM_HEADS):
        s = _dot_nt(qb[:, _head_slice(h)], k_head(h))
        p = jnp.exp(s - jnp.max(s, axis=-1, keepdims=True))
        denom = jnp.sum(p, axis=-1, keepdims=True)
        outs.append(_dot(p.astype(BF16), v_head(h)) / denom)
    return jnp.concatenate(outs, axis=-1)


def _attention_interleaved(q, k2, v2):
    t = q.shape[0]
    qb = (q * (MEM_HEAD_DIM ** -0.5)).astype(BF16)
    q2 = jnp.concatenate([qb[:, _head_slice(h)] for h in range(MEM_HEADS)], axis=0)
    s = _dot_nt(q2, k2)
    row_head = lax.broadcasted_iota(jnp.int32, s.shape, 0) // t
    col_head = lax.broadcasted_iota(jnp.int32, s.shape, 1) % MEM_HEADS
    s = jnp.where(row_head == col_head, s, -jnp.inf)
    p = jnp.exp(s - jnp.max(s, axis=-1, keepdims=True))
    denom = jnp.sum(p, axis=-1, keepdims=True)
    return _dot(p.astype(BF16), v2) / denom


class _Handoff:
    N = 8

    def __init__(self, refs):
        (self.mix, self.ug, self.v, self.zs, self.xs, self.bm, self.cm, self.dt) = refs


def _prompt_step(xc_ref, xp_ref, kb_ref, vb_ref, win_ref, wout_ref, ng_ref, gmg_ref, gmb_ref, ws_ref, bsf_ref,
                 cw_ref, cb_ref, dtb_ref, alog_ref, dsk_ref, sng_ref, fg_ref, y_ref,
                 hn_sc, xp_sc, y_sc, ht_sc, ssd_sc, out_sc, new, old, tile):
    a_row = -jnp.exp(alog_ref[...])
    row = lax.broadcasted_iota(jnp.int32, (CHUNK, CHUNK), 0)
    col = lax.broadcasted_iota(jnp.int32, (CHUNK, CHUNK), 1)
    causal = row >= col
    tril = causal.astype(F32)
    head_lane = col < SSM_HEADS
    first_half = col < SSM_HEAD_DIM
    acum_sc, acum_t_sc, dt_t_sc, to_end_t_sc, cb_sc, bm_t_sc = (ssd_sc.at[i] for i in range(6))

    def proj(off, width):
        return _dot(hn_sc[...], win_ref[:, off:off + width])

    def gmlp_mix(c):
        r = slice(c * CHUNK, (c + 1) * CHUNK)
        for h in range(GM_HEADS):
            hs = slice(h * GM_HEAD_DIM, (h + 1) * GM_HEAD_DIM)
            mixed = _dot(ws_ref[h], old.v[r, hs]) + bsf_ref[:, hs]
            old.mix[r, hs] = (old.ug[r, hs] * mixed).astype(BF16)

    def seq_chunk_start(c):
        r = slice(c * CHUNK, (c + 1) * CHUNK)
        dt = old.dt[r, :]
        adt = jnp.where(head_lane, dt * a_row, 0.0)
        acum = jnp.dot(tril, adt, precision=lax.Precision.HIGHEST, preferred_element_type=F32)
        acum_t = acum.T
        dt_t = dt.T
        acum_sc[...] = acum
        acum_t_sc[...] = acum_t
        dt_t_sc[...] = dt_t
        to_end_t_sc[...] = jnp.exp(acum_t[:, CHUNK - 1:CHUNK] - acum_t) * dt_t

    def seq_group(c, g):
        r = slice(c * CHUNK, (c + 1) * CHUNK)
        gs = slice(g * SSM_STATE, (g + 1) * SSM_STATE)
        bm_g = old.bm[r, gs]
        cb_sc[...] = _dot_nt(old.cm[r, gs].astype(BF16), bm_g.astype(BF16))
        bm_t_sc[...] = bm_g.T
        for kk in range(0, HEADS_PER_GROUP, 2):
            k0 = g * HEADS_PER_GROUP + kk
            ps = slice(k0 * SSM_HEAD_DIM, (k0 + 2) * SSM_HEAD_DIM)
            cm_g = old.cm[r, gs]
            lhs_parts, b_parts, decs = [], [], []
            for k in (k0, k0 + 1):
                a_col = jnp.broadcast_to(acum_sc[:, k:k + 1], (CHUNK, CHUNK))
                diff = a_col - acum_t_sc[k:k + 1, :]
                decay = jnp.exp(jnp.where(causal, diff, -jnp.inf))
                lhs_parts.append((cb_sc[...] * decay * dt_t_sc[k:k + 1, :]).astype(BF16))
                lhs_parts.append((cm_g * jnp.exp(a_col)).astype(BF16))
                b_parts.append((bm_t_sc[...] * to_end_t_sc[k:k + 1, :]).astype(BF16))
                decs.append(jnp.exp(a_col[CHUNK - 1:CHUNK, :]))
            xs_pair = old.xs[r, ps]
            h_pair = ht_sc[:, ps]
            zero = jnp.zeros_like(xs_pair)
            xs_lo = jnp.where(first_half, xs_pair, zero).astype(BF16)
            xs_hi = jnp.where(first_half, zero, xs_pair).astype(BF16)
            h_lo = jnp.where(first_half, h_pair, zero).astype(BF16)
            h_hi = jnp.where(first_half, zero, h_pair).astype(BF16)
            lhs = jnp.concatenate(lhs_parts, axis=1)
            rhs = jnp.concatenate([xs_lo, h_lo, xs_hi, h_hi], axis=0)
            y_sc[:, ps] = _dot(lhs, rhs)
            upd = _dot(jnp.concatenate(b_parts, axis=1), jnp.concatenate([xs_lo, xs_hi], axis=0))
            dec = jnp.where(first_half[0:1, :], decs[0], decs[1])
            ht_sc[:, ps] = h_pair * dec + upd

    def seq_chunk_end(c):
        r = slice(c * CHUNK, (c + 1) * CHUNK)
        y = (y_sc[...] + dsk_ref[...] * old.xs[r, :]) * old.zs[r, :]
        old.mix[r, GM_WIDTH:GM_WIDTH + SSM_WIDTH] = _group_rmsnorm(y, sng_ref[...]).astype(BF16)

    def out_piece(j, n):
        cs = slice(j * D_MODEL // n, (j + 1) * D_MODEL // n)
        out_sc[:, cs] = xp_ref[0, :, cs] + _dot(old.mix[...], wout_ref[:, cs])

    assert tile // CHUNK == 2, "the emission order below is written for two chunks per tile"

    hn_sc[...] = _rms(xc_ref[0], ng_ref[...]).astype(BF16)

    new.zs[...] = _silu(proj(OFF_Z, SSM_WIDTH))
    xp_sc[SUBLANES:SUBLANES + tile, :] = proj(OFF_XBC, CONV_DIM)
    acc = cb_ref[...]
    for j in range(CONV_WIDTH):
        lo = SUBLANES - (CONV_WIDTH - 1) + j
        acc = acc + cw_ref[j:j + 1, :] * xp_sc[lo:lo + tile, :]
    xbc = _silu(acc)
    new.xs[...] = xbc[:, :SSM_WIDTH]
    new.bm[...] = xbc[:, SSM_WIDTH:SSM_WIDTH + SSM_GROUPS * SSM_STATE]
    new.cm[...] = xbc[:, SSM_WIDTH + SSM_GROUPS * SSM_STATE:]
    xp_sc[SUBLANES - (CONV_WIDTH - 1):SUBLANES, :] = xp_sc[tile + SUBLANES - (CONV_WIDTH - 1):tile + SUBLANES, :]
    new.dt[...] = _softplus(proj(OFF_DT, DT_PAD) + dtb_ref[...])
    u = _gelu(proj(OFF_U, GM_WIDTH))
    new.ug[...] = u * _silu(proj(OFF_GATE, GM_WIDTH))
    new.v[...] = _layernorm(_gelu(proj(OFF_V, GM_WIDTH)), gmg_ref[...], gmb_ref[...]).astype(BF16)
    gmlp_mix(0)
    gmlp_mix(1)
    seq_chunk_start(0)
    seq_group(0, 0)
    seq_group(0, 1)
    seq_chunk_end(0)
    att = _attention(proj(OFF_Q, MEM_WIDTH), lambda h: kb_ref[0, :, _head_slice(h)], lambda h: vb_ref[0, :, _head_slice(h)])
    new.mix[:, GM_WIDTH + SSM_WIDTH:MIX_WIDTH] = (att * _silu(proj(OFF_MGATE, MEM_WIDTH))).astype(BF16)
    seq_chunk_start(1)
    seq_group(1, 0)
    seq_group(1, 1)
    seq_chunk_end(1)

    n_out = 4
    for j in range(n_out):
        out_piece(j, n_out)
    y_ref[0] = _rms(out_sc[...], fg_ref[...])


def _prompt_kernel(xc_ref, kb_ref, vb_ref, win_ref, wout_ref, ng_ref, gmg_ref, gmb_ref,
                   ws_ref, bsf_ref, cw_ref, cb_ref, dtb_ref, alog_ref, dsk_ref, sng_ref, fg_ref,
                   y_ref, ssm_ref, conv_ref, hn_sc, xpad_sc, y_sc, ht_sc, ssd_sc, out_sc, *handoff, tile, nt, total):
    t = pl.program_id(0) % nt
    tile_set = _Handoff(handoff)

    @pl.when(t == 0)
    def _():
        xpad_sc[0:SUBLANES, :] = jnp.zeros((SUBLANES, CONV_DIM), F32)
        ht_sc[...] = jnp.zeros_like(ht_sc)

    _prompt_step(xc_ref, xc_ref, kb_ref, vb_ref, win_ref, wout_ref, ng_ref, gmg_ref, gmb_ref, ws_ref, bsf_ref,
                 cw_ref, cb_ref, dtb_ref, alog_ref, dsk_ref, sng_ref, fg_ref, y_ref,
                 hn_sc, xpad_sc, y_sc, ht_sc, ssd_sc, out_sc, tile_set, tile_set, tile)

    @pl.when(t == nt - 1)
    def _():
        conv_ref[0] = xpad_sc[SUBLANES - (CONV_WIDTH - 1):SUBLANES, :]
        ssm_ref[0] = ht_sc[...].T


def _prompt_layer(x, kb, vb, w_in, w_out, ng, gmg, gmb, ws, bsf, cw, cb, dtb, alog, dsk, sng, fg):
    b, seq, _ = x.shape
    tile = PROMPT_TILE
    nt = seq // tile
    total = b * nt
    assert seq % tile == 0 and tile % CHUNK == 0

    def const(shape):
        return pl.BlockSpec(shape, lambda s: (0,) * len(shape), pipeline_mode=pl.Buffered(1))

    cur = lambda s: s
    prev = lambda s: s
    in_specs = [
        pl.BlockSpec((1, tile, D_MODEL), lambda s: (cur(s) // nt, cur(s) % nt, 0)),
        pl.BlockSpec((1, MEM_LEN, MEM_WIDTH), lambda s: (cur(s) // nt, 0, 0)),
        pl.BlockSpec((1, MEM_LEN, MEM_WIDTH), lambda s: (cur(s) // nt, 0, 0)),
        const((D_MODEL, IN_PACKED)),
        const((MIX_WIDTH, D_MODEL)),
        const((1, D_MODEL)), const((1, GM_WIDTH)), const((1, GM_WIDTH)),
        const((GM_HEADS, CHUNK, CHUNK)), const((CHUNK, GM_WIDTH)),
        const((CONV_WIDTH, CONV_DIM)), const((1, CONV_DIM)),
        const((1, DT_PAD)), const((1, DT_PAD)), const((1, SSM_WIDTH)), const((1, SSM_WIDTH)),
        const((1, D_MODEL)),
    ]
    out_specs = [
        pl.BlockSpec((1, tile, D_MODEL), lambda s: (prev(s) // nt, prev(s) % nt, 0)),
        pl.BlockSpec((1, SSM_WIDTH, SSM_STATE), lambda s: (prev(s) // nt, 0, 0)),
        pl.BlockSpec((1, CONV_WIDTH - 1, CONV_DIM), lambda s: (cur(s) // nt, 0, 0)),
    ]
    out_shape = [
        jax.ShapeDtypeStruct((b, seq, D_MODEL), F32),
        jax.ShapeDtypeStruct((b, SSM_WIDTH, SSM_STATE), F32),
        jax.ShapeDtypeStruct((b, CONV_WIDTH - 1, CONV_DIM), F32),
    ]
    handoff = [
        pltpu.VMEM((tile, MIX_WIDTH), BF16),
        pltpu.VMEM((tile, GM_WIDTH), F32),
        pltpu.VMEM((tile, GM_WIDTH), BF16),
        pltpu.VMEM((tile, SSM_WIDTH), F32),
        pltpu.VMEM((tile, SSM_WIDTH), F32),
        pltpu.VMEM((tile, SSM_GROUPS * SSM_STATE), F32),
        pltpu.VMEM((tile, SSM_GROUPS * SSM_STATE), F32),
        pltpu.VMEM((tile, DT_PAD), F32),
    ]
    assert len(handoff) == _Handoff.N
    scratch = [
        pltpu.VMEM((tile, D_MODEL), BF16),
        pltpu.VMEM((tile + SUBLANES, CONV_DIM), F32),
        pltpu.VMEM((CHUNK, SSM_WIDTH), F32),
        pltpu.VMEM((SSM_STATE, SSM_WIDTH), F32),
        pltpu.VMEM((6, CHUNK, CHUNK), F32),
        pltpu.VMEM((tile, D_MODEL), F32),
    ] + handoff
    return pl.pallas_call(
        functools.partial(_prompt_kernel, tile=tile, nt=nt, total=total),
        grid=(total,),
        in_specs=in_specs,
        out_specs=out_specs,
        out_shape=out_shape,
        scratch_shapes=scratch,
        compiler_params=pltpu.CompilerParams(dimension_semantics=("arbitrary",),
                                             vmem_limit_bytes=VMEM_LIMIT),
        name="prompt_layer",
    )(x, kb, vb, w_in, w_out, ng, gmg, gmb, ws, bsf, cw, cb, dtb, alog, dsk, sng, fg)


def _sample_proj_kernel(x_ref, g_ref, w_ref, o_ref):
    hn = _rms(x_ref[...], g_ref[...]).astype(BF16)
    o_ref[...] = _dot(hn, w_ref[...])


def _sample_proj(x, g, w_in):
    m = x.shape[0]
    return pl.pallas_call(
        _sample_proj_kernel,
        grid=(IN_PACKED // PROJ_BLOCK,),
        in_specs=[pl.BlockSpec((m, D_MODEL), lambda j: (0, 0)),
                  pl.BlockSpec((1, D_MODEL), lambda j: (0, 0)),
                  pl.BlockSpec((D_MODEL, PROJ_BLOCK), lambda j: (0, j))],
        out_specs=pl.BlockSpec((m, PROJ_BLOCK), lambda j: (0, j)),
        out_shape=jax.ShapeDtypeStruct((m, IN_PACKED), F32),
        compiler_params=pltpu.CompilerParams(dimension_semantics=("arbitrary",),
                                             vmem_limit_bytes=VMEM_LIMIT),
        name="sample_proj",
    )(x, g, w_in)


def _sample_mix_kernel(p_ref, cst_ref, ssm_ref, k_ref, v_ref, gmg_ref, gmb_ref, gcoef_ref, gbias_ref,
                       cw_ref, cb_ref, dtb_ref, alog_ref, dsk_ref, sng_ref, expand_ref,
                       mix_ref, ssm_out_ref, conv_out_ref, gv_ref,
                       xp_sc, xbc_sc, att_sc, yoff_sc, dec_sc, *, bblk, seq):
    rows = bblk * seq

    def col(off, width):
        return p_ref[:, off:off + width]

    tpos = lax.broadcasted_iota(jnp.int32, (rows, 1), 0) % seq

    def back(a, j):
        return a if j == 0 else pltpu.roll(a, j, 0)

    u = _gelu(col(OFF_U, GM_WIDTH))
    v = _layernorm(_gelu(col(OFF_V, GM_WIDTH)), gmg_ref[...], gmb_ref[...])
    gv_ref[...] = v
    mixed = gbias_ref[...]
    for j in range(seq):
        mixed = mixed + gcoef_ref[j] * back(v, j)
    mix_ref[:, 0:GM_WIDTH] = (u * mixed * _silu(col(OFF_GATE, GM_WIDTH))).astype(BF16)

    xbc_raw = col(OFF_XBC, CONV_DIM)
    for b in range(bblk):
        xp_sc[b, 0:CONV_WIDTH - 1, :] = cst_ref[b]
        xp_sc[b, CONV_WIDTH - 1:CONV_WIDTH - 1 + seq, :] = xbc_raw[b * seq:(b + 1) * seq, :]
    for b in range(bblk):
        acc = jnp.broadcast_to(cb_ref[...], (seq, CONV_DIM))
        for j in range(CONV_WIDTH):
            acc = acc + cw_ref[j:j + 1, :] * xp_sc[b, j:j + seq, :]
        xbc_sc[b * seq:(b + 1) * seq, :] = _silu(acc)
        conv_out_ref[b] = xp_sc[b, seq:seq + CONV_WIDTH - 1, :]
    xs = xbc_sc[:, 0:SSM_WIDTH]
    bm = xbc_sc[:, SSM_WIDTH:SSM_WIDTH + SSM_GROUPS * SSM_STATE]
    cm = xbc_sc[:, SSM_WIDTH + SSM_GROUPS * SSM_STATE:CONV_DIM]

    lane = lax.broadcasted_iota(jnp.int32, (rows, DT_PAD), 1)
    dt = _softplus(col(OFF_DT, DT_PAD) + dtb_ref[...])
    adt = jnp.where(lane < SSM_HEADS, dt * (-jnp.exp(alog_ref[...])), 0.0)
    acum = adt
    for j in range(1, seq):
        acum = acum + jnp.where(tpos >= j, back(adt, j), 0.0)
    a_last = jnp.zeros_like(acum)
    for j in range(seq):
        a_last = a_last + jnp.where(tpos == seq - 1 - j, acum if j == 0 else pltpu.roll(acum, rows - j, 0), 0.0)
    coefs = []
    for j in range(seq):
        cbj = []
        for g in range(SSM_GROUPS):
            gs = slice(g * SSM_STATE, (g + 1) * SSM_STATE)
            cbj.append(jnp.sum(cm[:, gs] * back(bm[:, gs], j), axis=-1, keepdims=True))
        cb_l = jnp.where(lane < HEADS_PER_GROUP, cbj[0], cbj[1])
        valid = tpos >= j
        decay = jnp.exp(jnp.where(valid, acum - back(acum, j), 0.0))
        coefs.append(jnp.where(valid, cb_l * decay * back(dt, j), 0.0))
    coefs.append(jnp.exp(acum))
    coefs.append(dt * jnp.exp(a_last - acum))
    stack = jnp.concatenate(coefs, axis=0)
    hi = stack.astype(BF16)
    lo = (stack - hi.astype(F32)).astype(BF16)
    wide = _dot(hi, expand_ref[...]) + _dot(lo, expand_ref[...])
    y = dsk_ref[...] * xs
    for j in range(seq):
        y = y + wide[j * rows:(j + 1) * rows, :] * back(xs, j)
    e_wide = wide[seq * rows:(seq + 1) * rows, :]
    wx = (xs * wide[(seq + 1) * rows:(seq + 2) * rows, :]).astype(BF16)
    dec_rows = jnp.exp(a_last)

    q = col(OFF_Q, MEM_WIDTH)
    cmb = cm.astype(BF16)
    bmb = bm.astype(BF16)
    for b in range(bblk):
        rs = slice(b * seq, (b + 1) * seq)
        att = _attention_interleaved(
            q[rs, :],
            k_ref[0, b].reshape(MEM_LEN * MEM_HEADS, MEM_HEAD_DIM).astype(BF16),
            v_ref[0, b].reshape(MEM_LEN * MEM_HEADS, MEM_HEAD_DIM).astype(BF16))
        for h in range(MEM_HEADS):
            att_sc[rs, _head_slice(h)] = att[h * seq:(h + 1) * seq, :]
        h0 = ssm_ref[b]
        h0b = h0.astype(BF16)
        dec_sc[...] = jnp.broadcast_to(dec_rows[b * seq + seq - 1:b * seq + seq, :], (LANES, DT_PAD)).T
        for g in range(SSM_GROUPS):
            gs = slice(g * SSM_STATE, (g + 1) * SSM_STATE)
            ws_ = slice(g * GROUP_WIDTH, (g + 1) * GROUP_WIDTH)
            yoff_sc[rs, ws_] = _dot_nt(cmb[rs, gs], h0b[ws_, :])
            upd = _dot_tn(wx[rs, ws_], bmb[rs, gs])
            for kk in range(HEADS_PER_GROUP):
                k = g * HEADS_PER_GROUP + kk
                hs = slice(k * SSM_HEAD_DIM, (k + 1) * SSM_HEAD_DIM)
                ssm_out_ref[b, hs, :] = (h0[hs, :] * dec_sc[k:k + 1, :]
                                         + upd[kk * SSM_HEAD_DIM:(kk + 1) * SSM_HEAD_DIM, :])

    y = (y + yoff_sc[...] * e_wide) * _silu(col(OFF_Z, SSM_WIDTH))
    mix_ref[:, GM_WIDTH:GM_WIDTH + SSM_WIDTH] = _group_rmsnorm(y, sng_ref[...]).astype(BF16)
    mix_ref[:, GM_WIDTH + SSM_WIDTH:MIX_WIDTH] = (att_sc[...] * _silu(col(OFF_MGATE, MEM_WIDTH))).astype(BF16)


def _sample_mix(proj, conv_state, ssm_state, mem_k, mem_v, gmg, gmb, gcoef, gbias, cw, cb, dtb, alog,
                dsk, sng, expand, seq):
    nb = conv_state.shape[0]
    bblk = SAMPLE_BATCH_BLOCK
    rows = bblk * seq
    assert nb % bblk == 0 and rows % (2 * SUBLANES) == 0

    def const(shape):
        return pl.BlockSpec(shape, lambda i: (0,) * len(shape))

    in_specs = [
        pl.BlockSpec((rows, IN_PACKED), lambda i: (i, 0)),
        pl.BlockSpec((bblk, CONV_WIDTH - 1, CONV_DIM), lambda i: (i, 0, 0)),
        pl.BlockSpec((bblk, SSM_WIDTH, SSM_STATE), lambda i: (i, 0, 0)),
        pl.BlockSpec((1, bblk, MEM_LEN, MEM_HEADS, MEM_HEAD_DIM), lambda i: (0, i, 0, 0, 0)),
        pl.BlockSpec((1, bblk, MEM_LEN, MEM_HEADS, MEM_HEAD_DIM), lambda i: (0, i, 0, 0, 0)),
        const((1, GM_WIDTH)), const((1, GM_WIDTH)),
        const((seq, rows, GM_WIDTH)), const((rows, GM_WIDTH)),
        const((CONV_WIDTH, CONV_DIM)), const((1, CONV_DIM)),
        const((1, DT_PAD)), const((1, DT_PAD)), const((1, SSM_WIDTH)), const((1, SSM_WIDTH)),
        const((DT_PAD, SSM_WIDTH)),
    ]
    out_specs = [
        pl.BlockSpec((rows, MIX_WIDTH), lambda i: (i, 0)),
        pl.BlockSpec((bblk, SSM_WIDTH, SSM_STATE), lambda i: (i, 0, 0)),
        pl.BlockSpec((bblk, CONV_WIDTH - 1, CONV_DIM), lambda i: (i, 0, 0)),
        pl.BlockSpec((rows, GM_WIDTH), lambda i: (i, 0)),
    ]
    out_shape = [
        jax.ShapeDtypeStruct((nb * seq, MIX_WIDTH), BF16),
        jax.ShapeDtypeStruct((nb, SSM_WIDTH, SSM_STATE), F32),
        jax.ShapeDtypeStruct((nb, CONV_WIDTH - 1, CONV_DIM), F32),
        jax.ShapeDtypeStruct((nb * seq, GM_WIDTH), F32),
    ]
    scratch = [
        pltpu.VMEM((bblk, SUBLANES, CONV_DIM), F32),
        pltpu.VMEM((rows, CONV_DIM), F32),
        pltpu.VMEM((rows, MEM_WIDTH), F32),
        pltpu.VMEM((rows, SSM_WIDTH), F32),
        pltpu.VMEM((DT_PAD, LANES), F32),
    ]
    return pl.pallas_call(
        functools.partial(_sample_mix_kernel, bblk=bblk, seq=seq),
        grid=(nb // bblk,),
        in_specs=in_specs,
        out_specs=out_specs,
        out_shape=out_shape,
        scratch_shapes=scratch,
        compiler_params=pltpu.CompilerParams(dimension_semantics=("arbitrary",),
                                             vmem_limit_bytes=VMEM_LIMIT),
        name="sample_mix",
    )(proj, conv_state, ssm_state, mem_k, mem_v, gmg, gmb, gcoef, gbias, cw, cb, dtb, alog, dsk, sng, expand)


def _sample_out_kernel(mix_ref, x_ref, w_ref, g_ref, o_ref):
    out = x_ref[...] + _dot(mix_ref[...], w_ref[...])
    o_ref[...] = _rms(out, g_ref[...])


def _sample_out(mix, x, w_out, g):
    m = x.shape[0]
    full = lambda shape: pl.BlockSpec(shape, lambda i: (0,) * len(shape))
    return pl.pallas_call(
        _sample_out_kernel,
        grid=(1,),
        in_specs=[full((m, MIX_WIDTH)), full((m, D_MODEL)), full((MIX_WIDTH, D_MODEL)), full((1, D_MODEL))],
        out_specs=full((m, D_MODEL)),
        out_shape=jax.ShapeDtypeStruct((m, D_MODEL), F32),
        compiler_params=pltpu.CompilerParams(dimension_semantics=("arbitrary",),
                                             vmem_limit_bytes=VMEM_LIMIT),
        name="sample_out",
    )(mix, x, w_out, g)


def _pack_w_in(w):
    sizes = (2 * GM_WIDTH, GM_WIDTH, SSM_WIDTH, CONV_DIM, SSM_HEADS, MEM_WIDTH, MEM_WIDTH)
    points = [sum(sizes[:i]) for i in range(1, len(sizes))]
    uv, gate, z, xbc, dt, q, mgate = jnp.split(w, points, axis=1)
    pad = jnp.zeros((D_MODEL, IN_PACKED - OFF_DT - SSM_HEADS), w.dtype)
    return jnp.concatenate([uv, gate, z, xbc, q, mgate, dt, pad], axis=1).astype(BF16)


def _pad_heads(a):
    return jnp.pad(a.astype(F32), (0, DT_PAD - SSM_HEADS)).reshape(1, DT_PAD)


def kernel(x_prompt, x_sample, mem_prompt, state_ssm, state_conv, cache_mem_k, cache_mem_v, norm_g, w_in,
           gm_norm_g, gm_norm_b, gm_w_spatial, gm_b_spatial, conv_w, conv_b, dt_bias, a_log, d_skip,
           ssm_norm_g, mem_norm_g, w_mem_k, w_mem_v, w_out, final_norm_g):
    assert norm_g.shape[0] == 1, "single layer"
    bp, seq_p, _ = x_prompt.shape
    bs, seq_s, _ = x_sample.shape
    row = lambda a: a.reshape(1, -1).astype(F32)

    w_in_p = _pack_w_in(w_in[0])
    w_out_b = w_out[0].astype(BF16)
    ng, gmg, gmb = row(norm_g[0]), row(gm_norm_g[0]), row(gm_norm_b[0])
    cw, cb = conv_w[0].astype(F32), row(conv_b[0])
    dtb, alog = _pad_heads(dt_bias[0]), _pad_heads(a_log[0])
    dsk = row(jnp.repeat(d_skip[0], SSM_HEAD_DIM))
    sng, fg = row(ssm_norm_g[0]), row(final_norm_g)
    w_sp = gm_w_spatial[0]
    tril_p = jnp.tril(jnp.ones((CHUNK, CHUNK), bool))
    ws_p = jnp.where(tril_p, w_sp, 0).astype(BF16)
    bsf_p = jnp.repeat(gm_b_spatial[0].T, GM_HEAD_DIM, axis=1).astype(F32)

    mk, mv, mkb, mvb = _memory_kv(mem_prompt, row(mem_norm_g[0]), w_mem_k[0].astype(BF16),
                                  w_mem_v[0].astype(BF16))
    y_p, ssm_p, conv_p = _prompt_layer(x_prompt, mkb, mvb, w_in_p, w_out_b, ng, gmg, gmb, ws_p, bsf_p,
                                       cw, cb, dtb, alog, dsk, sng, fg)

    rows = SAMPLE_BATCH_BLOCK * seq_s
    tpos = jnp.arange(rows) % seq_s
    gcoef = jnp.stack([
        jnp.where((tpos >= j)[:, None],
                  jnp.repeat(w_sp[:, tpos, jnp.maximum(tpos - j, 0)].T, GM_HEAD_DIM, axis=1), 0.0)
        for j in range(seq_s)]).astype(F32)
    gbias = jnp.repeat(gm_b_spatial[0][:, tpos].T, GM_HEAD_DIM, axis=1).astype(F32)
    expand = (jnp.arange(DT_PAD)[:, None] == (jnp.arange(SSM_WIDTH) // SSM_HEAD_DIM)[None, :]).astype(BF16)

    xs2 = x_sample.reshape(bs * seq_s, D_MODEL)
    proj_s = _sample_proj(xs2, ng, w_in_p)
    mix_s, ssm_s, conv_s, gv_s = _sample_mix(
        proj_s, state_conv[0], state_ssm[0].reshape(bs, SSM_WIDTH, SSM_STATE),
        cache_mem_k, cache_mem_v,
        gmg, gmb, gcoef, gbias, cw, cb, dtb, alog, dsk, sng, expand, seq_s)
    y_s = _sample_out(mix_s, xs2, w_out_b, fg)

    return (y_p,
            y_s.reshape(bs, seq_s, D_MODEL),
            ssm_p.reshape(1, bp, SSM_HEADS, SSM_HEAD_DIM, SSM_STATE),
            conv_p[None],
            mk,
            mv,
            ssm_s.reshape(1, bs, SSM_HEADS, SSM_HEAD_DIM, SSM_STATE),
            conv_s[None],
            gv_s.reshape(1, bs, seq_s, GM_WIDTH))
```

```python
import functools
import math

import jax
import jax.numpy as jnp
from jax import lax
from jax.experimental import pallas as pl
from jax.experimental.pallas import tpu as pltpu

F32 = jnp.float32
BF16 = jnp.bfloat16

D_MODEL = 1024
GM_WIDTH = 1024
GM_HEADS = 8
GM_HEAD_DIM = 128
CHUNK = 128
SSM_WIDTH = 1024
SSM_HEADS = 16
SSM_HEAD_DIM = 64
SSM_GROUPS = 2
SSM_STATE = 128
HEADS_PER_GROUP = SSM_HEADS // SSM_GROUPS
GROUP_WIDTH = SSM_WIDTH // SSM_GROUPS
CONV_WIDTH = 4
CONV_DIM = SSM_WIDTH + 2 * SSM_GROUPS * SSM_STATE
MEM_LEN = 256
MEM_HEADS = 4
MEM_HEAD_DIM = 256
MEM_WIDTH = 1024
MIX_WIDTH = GM_WIDTH + SSM_WIDTH + MEM_WIDTH
EPS = 1e-6

LANES = 128
SUBLANES = 8

OFF_U = 0
OFF_V = OFF_U + GM_WIDTH
OFF_GATE = OFF_V + GM_WIDTH
OFF_Z = OFF_GATE + GM_WIDTH
OFF_XBC = OFF_Z + SSM_WIDTH
OFF_Q = OFF_XBC + CONV_DIM
OFF_MGATE = OFF_Q + MEM_WIDTH
OFF_DT = OFF_MGATE + MEM_WIDTH
DT_PAD = LANES
IN_PACKED = 8192
PROJ_BLOCK = 1024

PROMPT_TILE = 256
SAMPLE_BATCH_BLOCK = 4
VMEM_LIMIT = 56 * 1024 * 1024


def _rms(x, g):
    return x * lax.rsqrt(jnp.mean(x * x, axis=-1, keepdims=True) + EPS) * g


def _gelu(x):
    return 0.5 * x * (1.0 + lax.erf(x * math.sqrt(0.5)))


def _silu(x):
    return x * jax.nn.sigmoid(x)


def _softplus(x):
    return jnp.maximum(x, 0.0) + jnp.log1p(jnp.exp(-jnp.abs(x)))


def _layernorm(x, g, b):
    mu = jnp.mean(x, axis=-1, keepdims=True)
    xc = x - mu
    var = jnp.mean(xc * xc, axis=-1, keepdims=True)
    return xc * lax.rsqrt(var + EPS) * g + b


def _dot(a, b):
    return jnp.dot(a, b, preferred_element_type=F32)


def _pack_rows(w):
    k, n = w.shape
    return lax.bitcast_convert_type(w.astype(BF16).reshape(k // 2, 2, n).swapaxes(1, 2), jnp.int32)


def _unpack_rows(w):
    return pltpu.bitcast(w, BF16)


def _dot_nt(a, b):
    return lax.dot_general(a, b, (((1,), (1,)), ((), ())), preferred_element_type=F32)


def _dot_tn(a, b):
    return lax.dot_general(a, b, (((0,), (0,)), ((), ())), preferred_element_type=F32)


def _group_rmsnorm(y, g):
    halves = []
    for i in range(SSM_GROUPS):
        yg = y[:, i * GROUP_WIDTH:(i + 1) * GROUP_WIDTH]
        halves.append(yg * lax.rsqrt(jnp.mean(yg * yg, axis=-1, keepdims=True) + EPS))
    return jnp.concatenate(halves, axis=-1) * g


def _memory_kv_kernel(mem_ref, g_ref, wk_ref, wv_ref, k_ref, v_ref, kb_ref, vb_ref):
    m = _rms(mem_ref[0], g_ref[...]).astype(BF16)
    k = _dot(m, _unpack_rows(wk_ref[...]))
    v = _dot(m, _unpack_rows(wv_ref[...]))
    for h in range(MEM_HEADS):
        k_ref[0, 0, :, h, :] = k[:, _head_slice(h)]
        v_ref[0, 0, :, h, :] = v[:, _head_slice(h)]
    kb_ref[0] = k.astype(BF16)
    vb_ref[0] = v.astype(BF16)


def _memory_kv(mem, g, wk, wv):
    b = mem.shape[0]
    blk = pl.BlockSpec((1, MEM_LEN, D_MODEL), lambda i: (i, 0, 0))
    blk5 = pl.BlockSpec((1, 1, MEM_LEN, MEM_HEADS, MEM_HEAD_DIM), lambda i: (0, i, 0, 0, 0))
    const = lambda shape: pl.BlockSpec(shape, lambda i: (0,) * len(shape))
    return pl.pallas_call(
        _memory_kv_kernel,
        grid=(b,),
        in_specs=[blk, const((1, D_MODEL)), const((D_MODEL // 2, MEM_WIDTH)), const((D_MODEL // 2, MEM_WIDTH))],
        out_specs=[blk5, blk5, blk, blk],
        out_shape=[jax.ShapeDtypeStruct((1, b, MEM_LEN, MEM_HEADS, MEM_HEAD_DIM), F32)] * 2
        + [jax.ShapeDtypeStruct((b, MEM_LEN, MEM_WIDTH), BF16)] * 2,
        compiler_params=pltpu.CompilerParams(dimension_semantics=("arbitrary",),
                                             vmem_limit_bytes=VMEM_LIMIT),
        name="memory_kv",
    )(mem, g, wk, wv)


def _head_slice(h):
    return slice(h * MEM_HEAD_DIM, (h + 1) * MEM_HEAD_DIM)


def _attention(q, k_head, v_head):
    qb = (q * (MEM_HEAD_DIM ** -0.5)).astype(BF16)
    outs = []
    for h in range(MEM_HEADS):
        s = _dot_nt(qb[:, _head_slice(h)], k_head(h))
        p = jnp.exp(s - jnp.max(s, axis=-1, keepdims=True))
        denom = jnp.sum(p, axis=-1, keepdims=True)
        outs.append(_dot(p.astype(BF16), v_head(h)) / denom)
    return jnp.concatenate(outs, axis=-1)


def _attention_interleaved(q, k2, v2):
    t = q.shape[0]
    qb = (q * (MEM_HEAD_DIM ** -0.5)).astype(BF16)
    q2 = jnp.concatenate([qb[:, _head_slice(h)] for h in range(MEM_HEADS)], axis=0)
    s = _dot_nt(q2, k2)
    row_head = lax.broadcasted_iota(jnp.int32, s.shape, 0) // t
    col_head = lax.broadcasted_iota(jnp.int32, s.shape, 1) % MEM_HEADS
    s = jnp.where(row_head == col_head, s, -jnp.inf)
    p = jnp.exp(s - jnp.max(s, axis=-1, keepdims=True))
    denom = jnp.sum(p, axis=-1, keepdims=True)
    return _dot(p.astype(BF16), v2) / denom


class _Handoff:
    N = 8

    def __init__(self, refs):
        (self.mix, self.ug, self.v, self.zs, self.xs, self.bm, self.cm, self.dt) = refs


def _prompt_step(xc_ref, xp_ref, kb_ref, vb_ref, win_ref, wout_ref, ng_ref, gmg_ref, gmb_ref, ws_ref, bsf_ref,
                 cw_ref, cb_ref, dtb_ref, alog_ref, dsk_ref, sng_ref, fg_ref, y_ref,
                 hn_sc, xp_sc, y_sc, ht_sc, ssd_sc, out_sc, new, old, tile):
    a_row = -jnp.exp(alog_ref[...])
    row = lax.broadcasted_iota(jnp.int32, (CHUNK, CHUNK), 0)
    col = lax.broadcasted_iota(jnp.int32, (CHUNK, CHUNK), 1)
    causal = row >= col
    tril = causal.astype(F32)
    head_lane = col < SSM_HEADS
    first_half = col < SSM_HEAD_DIM
    acum_sc, acum_t_sc, dt_t_sc, to_end_t_sc, cb_sc, bm_t_sc = (ssd_sc.at[i] for i in range(6))

    def proj(off, width):
        return _dot(hn_sc[...], _unpack_rows(win_ref[:, off:off + width]))

    def gmlp_mix(c):
        r = slice(c * CHUNK, (c + 1) * CHUNK)
        for h in range(GM_HEADS):
            hs = slice(h * GM_HEAD_DIM, (h + 1) * GM_HEAD_DIM)
            mixed = _dot(ws_ref[h], old.v[r, hs]) + bsf_ref[:, hs]
            old.mix[r, hs] = (old.ug[r, hs] * mixed).astype(BF16)

    def seq_chunk_start(c):
        r = slice(c * CHUNK, (c + 1) * CHUNK)
        dt = old.dt[r, :]
        adt = jnp.where(head_lane, dt * a_row, 0.0)
        acum = jnp.dot(tril, adt, precision=lax.Precision.HIGHEST, preferred_element_type=F32)
        acum_t = acum.T
        dt_t = dt.T
        acum_sc[...] = acum
        acum_t_sc[...] = acum_t
        dt_t_sc[...] = dt_t
        to_end_t_sc[...] = jnp.exp(acum_t[:, CHUNK - 1:CHUNK] - acum_t) * dt_t

    def seq_group(c, g):
        r = slice(c * CHUNK, (c + 1) * CHUNK)
        gs = slice(g * SSM_STATE, (g + 1) * SSM_STATE)
        bm_g = old.bm[r, gs]
        cb_sc[...] = _dot_nt(old.cm[r, gs].astype(BF16), bm_g.astype(BF16))
        bm_t_sc[...] = bm_g.T
        for kk in range(0, HEADS_PER_GROUP, 2):
            k0 = g * HEADS_PER_GROUP + kk
            ps = slice(k0 * SSM_HEAD_DIM, (k0 + 2) * SSM_HEAD_DIM)
            cm_g = old.cm[r, gs]
            lhs_parts, b_parts, decs = [], [], []
            for k in (k0, k0 + 1):
                a_col = jnp.broadcast_to(acum_sc[:, k:k + 1], (CHUNK, CHUNK))
                diff = a_col - acum_t_sc[k:k + 1, :]
                decay = jnp.exp(jnp.where(causal, diff, -jnp.inf))
                lhs_parts.append((cb_sc[...] * decay * dt_t_sc[k:k + 1, :]).astype(BF16))
                lhs_parts.append((cm_g * jnp.exp(a_col)).astype(BF16))
                b_parts.append((bm_t_sc[...] * to_end_t_sc[k:k + 1, :]).astype(BF16))
                decs.append(jnp.exp(a_col[CHUNK - 1:CHUNK, :]))
            xs_pair = old.xs[r, ps]
            h_pair = ht_sc[:, ps]
            zero = jnp.zeros_like(xs_pair)
            xs_lo = jnp.where(first_half, xs_pair, zero).astype(BF16)
            xs_hi = jnp.where(first_half, zero, xs_pair).astype(BF16)
            h_lo = jnp.where(first_half, h_pair, zero).astype(BF16)
            h_hi = jnp.where(first_half, zero, h_pair).astype(BF16)
            lhs = jnp.concatenate(lhs_parts, axis=1)
            rhs = jnp.concatenate([xs_lo, h_lo, xs_hi, h_hi], axis=0)
            y_sc[:, ps] = _dot(lhs, rhs)
            upd = _dot(jnp.concatenate(b_parts, axis=1), jnp.concatenate([xs_lo, xs_hi], axis=0))
            dec = jnp.where(first_half[0:1, :], decs[0], decs[1])
            ht_sc[:, ps] = h_pair * dec + upd

    def seq_chunk_end(c):
        r = slice(c * CHUNK, (c + 1) * CHUNK)
        y = (y_sc[...] + dsk_ref[...] * old.xs[r, :]) * old.zs[r, :]
        old.mix[r, GM_WIDTH:GM_WIDTH + SSM_WIDTH] = _group_rmsnorm(y, sng_ref[...]).astype(BF16)

    def out_piece(j, n):
        cs = slice(j * D_MODEL // n, (j + 1) * D_MODEL // n)
        out_sc[:, cs] = xp_ref[0, :, cs] + _dot(old.mix[...], _unpack_rows(wout_ref[:, cs]))

    assert tile // CHUNK == 2, "the emission order below is written for two chunks per tile"

    hn_sc[...] = _rms(xc_ref[0], ng_ref[...]).astype(BF16)

    new.zs[...] = _silu(proj(OFF_Z, SSM_WIDTH))
    xp_sc[SUBLANES:SUBLANES + tile, :] = proj(OFF_XBC, CONV_DIM)
    acc = cb_ref[...]
    for j in range(CONV_WIDTH):
        lo = SUBLANES - (CONV_WIDTH - 1) + j
        acc = acc + cw_ref[j:j + 1, :] * xp_sc[lo:lo + tile, :]
    xbc = _silu(acc)
    new.xs[...] = xbc[:, :SSM_WIDTH]
    new.bm[...] = xbc[:, SSM_WIDTH:SSM_WIDTH + SSM_GROUPS * SSM_STATE]
    new.cm[...] = xbc[:, SSM_WIDTH + SSM_GROUPS * SSM_STATE:]
    xp_sc[SUBLANES - (CONV_WIDTH - 1):SUBLANES, :] = xp_sc[tile + SUBLANES - (CONV_WIDTH - 1):tile + SUBLANES, :]
    new.dt[...] = _softplus(proj(OFF_DT, DT_PAD) + dtb_ref[...])
    u = _gelu(proj(OFF_U, GM_WIDTH))
    new.ug[...] = u * _silu(proj(OFF_GATE, GM_WIDTH))
    new.v[...] = _layernorm(_gelu(proj(OFF_V, GM_WIDTH)), gmg_ref[...], gmb_ref[...]).astype(BF16)
    gmlp_mix(0)
    gmlp_mix(1)
    seq_chunk_start(0)
    seq_group(0, 0)
    seq_group(0, 1)
    seq_chunk_end(0)
    att = _attention(proj(OFF_Q, MEM_WIDTH), lambda h: kb_ref[0, :, _head_slice(h)], lambda h: vb_ref[0, :, _head_slice(h)])
    new.mix[:, GM_WIDTH + SSM_WIDTH:MIX_WIDTH] = (att * _silu(proj(OFF_MGATE, MEM_WIDTH))).astype(BF16)
    seq_chunk_start(1)
    seq_group(1, 0)
    seq_group(1, 1)
    seq_chunk_end(1)

    n_out = 4
    for j in range(n_out):
        out_piece(j, n_out)
    y_ref[0] = _rms(out_sc[...], fg_ref[...])


def _prompt_kernel(xc_ref, kb_ref, vb_ref, win_ref, wout_ref, ng_ref, gmg_ref, gmb_ref,
                   ws_ref, bsf_ref, cw_ref, cb_ref, dtb_ref, alog_ref, dsk_ref, sng_ref, fg_ref,
                   y_ref, ssm_ref, conv_ref, hn_sc, xpad_sc, y_sc, ht_sc, ssd_sc, out_sc, *handoff, tile, nt, total):
    t = pl.program_id(0) % nt
    tile_set = _Handoff(handoff)

    @pl.when(t == 0)
    def _():
        xpad_sc[0:SUBLANES, :] = jnp.zeros((SUBLANES, CONV_DIM), F32)
        ht_sc[...] = jnp.zeros_like(ht_sc)

    _prompt_step(xc_ref, xc_ref, kb_ref, vb_ref, win_ref, wout_ref, ng_ref, gmg_ref, gmb_ref, ws_ref, bsf_ref,
                 cw_ref, cb_ref, dtb_ref, alog_ref, dsk_ref, sng_ref, fg_ref, y_ref,
                 hn_sc, xpad_sc, y_sc, ht_sc, ssd_sc, out_sc, tile_set, tile_set, tile)

    @pl.when(t == nt - 1)
    def _():
        conv_ref[0] = xpad_sc[SUBLANES - (CONV_WIDTH - 1):SUBLANES, :]
        ssm_ref[0] = ht_sc[...].T


def _prompt_layer(x, kb, vb, w_in, w_out, ng, gmg, gmb, ws, bsf, cw, cb, dtb, alog, dsk, sng, fg):
    b, seq, _ = x.shape
    tile = PROMPT_TILE
    nt = seq // tile
    total = b * nt
    assert seq % tile == 0 and tile % CHUNK == 0

    def const(shape):
        return pl.BlockSpec(shape, lambda s: (0,) * len(shape), pipeline_mode=pl.Buffered(1))

    cur = lambda s: s
    prev = lambda s: s
    in_specs = [
        pl.BlockSpec((1, tile, D_MODEL), lambda s: (cur(s) // nt, cur(s) % nt, 0)),
        pl.BlockSpec((1, MEM_LEN, MEM_WIDTH), lambda s: (cur(s) // nt, 0, 0)),
        pl.BlockSpec((1, MEM_LEN, MEM_WIDTH), lambda s: (cur(s) // nt, 0, 0)),
        const((D_MODEL // 2, IN_PACKED)),
        const((MIX_WIDTH // 2, D_MODEL)),
        const((1, D_MODEL)), const((1, GM_WIDTH)), const((1, GM_WIDTH)),
        const((GM_HEADS, CHUNK, CHUNK)), const((CHUNK, GM_WIDTH)),
        const((CONV_WIDTH, CONV_DIM)), const((1, CONV_DIM)),
        const((1, DT_PAD)), const((1, DT_PAD)), const((1, SSM_WIDTH)), const((1, SSM_WIDTH)),
        const((1, D_MODEL)),
    ]
    out_specs = [
        pl.BlockSpec((1, tile, D_MODEL), lambda s: (prev(s) // nt, prev(s) % nt, 0)),
        pl.BlockSpec((1, SSM_WIDTH, SSM_STATE), lambda s: (prev(s) // nt, 0, 0)),
        pl.BlockSpec((1, CONV_WIDTH - 1, CONV_DIM), lambda s: (cur(s) // nt, 0, 0)),
    ]
    out_shape = [
        jax.ShapeDtypeStruct((b, seq, D_MODEL), F32),
        jax.ShapeDtypeStruct((b, SSM_WIDTH, SSM_STATE), F32),
        jax.ShapeDtypeStruct((b, CONV_WIDTH - 1, CONV_DIM), F32),
    ]
    handoff = [
        pltpu.VMEM((tile, MIX_WIDTH), BF16),
        pltpu.VMEM((tile, GM_WIDTH), F32),
        pltpu.VMEM((tile, GM_WIDTH), BF16),
        pltpu.VMEM((tile, SSM_WIDTH), F32),
        pltpu.VMEM((tile, SSM_WIDTH), F32),
        pltpu.VMEM((tile, SSM_GROUPS * SSM_STATE), F32),
        pltpu.VMEM((tile, SSM_GROUPS * SSM_STATE), F32),
        pltpu.VMEM((tile, DT_PAD), F32),
    ]
    assert len(handoff) == _Handoff.N
    scratch = [
        pltpu.VMEM((tile, D_MODEL), BF16),
        pltpu.VMEM((tile + SUBLANES, CONV_DIM), F32),
        pltpu.VMEM((CHUNK, SSM_WIDTH), F32),
        pltpu.VMEM((SSM_STATE, SSM_WIDTH), F32),
        pltpu.VMEM((6, CHUNK, CHUNK), F32),
        pltpu.VMEM((tile, D_MODEL), F32),
    ] + handoff
    return pl.pallas_call(
        functools.partial(_prompt_kernel, tile=tile, nt=nt, total=total),
        grid=(total,),
        in_specs=in_specs,
        out_specs=out_specs,
        out_shape=out_shape,
        scratch_shapes=scratch,
        compiler_params=pltpu.CompilerParams(dimension_semantics=("arbitrary",),
                                             vmem_limit_bytes=VMEM_LIMIT),
        name="prompt_layer",
    )(x, kb, vb, w_in, w_out, ng, gmg, gmb, ws, bsf, cw, cb, dtb, alog, dsk, sng, fg)


def _sample_proj_kernel(x_ref, g_ref, w_ref, o_ref):
    hn = _rms(x_ref[...], g_ref[...]).astype(BF16)
    o_ref[...] = _dot(hn, _unpack_rows(w_ref[...]))


def _sample_proj(x, g, w_in):
    m = x.shape[0]
    return pl.pallas_call(
        _sample_proj_kernel,
        grid=(IN_PACKED // PROJ_BLOCK,),
        in_specs=[pl.BlockSpec((m, D_MODEL), lambda j: (0, 0)),
                  pl.BlockSpec((1, D_MODEL), lambda j: (0, 0)),
                  pl.BlockSpec((D_MODEL // 2, PROJ_BLOCK), lambda j: (0, j))],
        out_specs=pl.BlockSpec((m, PROJ_BLOCK), lambda j: (0, j)),
        out_shape=jax.ShapeDtypeStruct((m, IN_PACKED), F32),
        compiler_params=pltpu.CompilerParams(dimension_semantics=("arbitrary",),
                                             vmem_limit_bytes=VMEM_LIMIT),
        name="sample_proj",
    )(x, g, w_in)


def _sample_mix_kernel(p_ref, cst_ref, ssm_ref, k_ref, v_ref, gmg_ref, gmb_ref, gcoef_ref, gbias_ref,
                       cw_ref, cb_ref, dtb_ref, alog_ref, dsk_ref, sng_ref, expand_ref,
                       mix_ref, ssm_out_ref, conv_out_ref, gv_ref,
                       xp_sc, xbc_sc, att_sc, yoff_sc, dec_sc, *, bblk, seq):
    rows = bblk * seq

    def col(off, width):
        return p_ref[:, off:off + width]

    tpos = lax.broadcasted_iota(jnp.int32, (rows, 1), 0) % seq

    def back(a, j):
        return a if j == 0 else pltpu.roll(a, j, 0)

    u = _gelu(col(OFF_U, GM_WIDTH))
    v = _layernorm(_gelu(col(OFF_V, GM_WIDTH)), gmg_ref[...], gmb_ref[...])
    gv_ref[...] = v
    mixed = gbias_ref[...]
    for j in range(seq):
        mixed = mixed + gcoef_ref[j] * back(v, j)
    mix_ref[:, 0:GM_WIDTH] = (u * mixed * _silu(col(OFF_GATE, GM_WIDTH))).astype(BF16)

    xbc_raw = col(OFF_XBC, CONV_DIM)
    for b in range(bblk):
        xp_sc[b, 0:CONV_WIDTH - 1, :] = cst_ref[b]
        xp_sc[b, CONV_WIDTH - 1:CONV_WIDTH - 1 + seq, :] = xbc_raw[b * seq:(b + 1) * seq, :]
    for b in range(bblk):
        acc = jnp.broadcast_to(cb_ref[...], (seq, CONV_DIM))
        for j in range(CONV_WIDTH):
            acc = acc + cw_ref[j:j + 1, :] * xp_sc[b, j:j + seq, :]
        xbc_sc[b * seq:(b + 1) * seq, :] = _silu(acc)
        conv_out_ref[b] = xp_sc[b, seq:seq + CONV_WIDTH - 1, :]
    xs = xbc_sc[:, 0:SSM_WIDTH]
    bm = xbc_sc[:, SSM_WIDTH:SSM_WIDTH + SSM_GROUPS * SSM_STATE]
    cm = xbc_sc[:, SSM_WIDTH + SSM_GROUPS * SSM_STATE:CONV_DIM]

    lane = lax.broadcasted_iota(jnp.int32, (rows, DT_PAD), 1)
    dt = _softplus(col(OFF_DT, DT_PAD) + dtb_ref[...])
    adt = jnp.where(lane < SSM_HEADS, dt * (-jnp.exp(alog_ref[...])), 0.0)
    acum = adt
    for j in range(1, seq):
        acum = acum + jnp.where(tpos >= j, back(adt, j), 0.0)
    a_last = jnp.zeros_like(acum)
    for j in range(seq):
        a_last = a_last + jnp.where(tpos == seq - 1 - j, acum if j == 0 else pltpu.roll(acum, rows - j, 0), 0.0)
    coefs = []
    for j in range(seq):
        cbj = []
        for g in range(SSM_GROUPS):
            gs = slice(g * SSM_STATE, (g + 1) * SSM_STATE)
            cbj.append(jnp.sum(cm[:, gs] * back(bm[:, gs], j), axis=-1, keepdims=True))
        cb_l = jnp.where(lane < HEADS_PER_GROUP, cbj[0], cbj[1])
        valid = tpos >= j
        decay = jnp.exp(jnp.where(valid, acum - back(acum, j), 0.0))
        coefs.append(jnp.where(valid, cb_l * decay * back(dt, j), 0.0))
    coefs.append(jnp.exp(acum))
    coefs.append(dt * jnp.exp(a_last - acum))
    stack = jnp.concatenate(coefs, axis=0)
    hi = stack.astype(BF16)
    lo = (stack - hi.astype(F32)).astype(BF16)
    wide = _dot(hi, expand_ref[...]) + _dot(lo, expand_ref[...])
    y = dsk_ref[...] * xs
    for j in range(seq):
        y = y + wide[j * rows:(j + 1) * rows, :] * back(xs, j)
    e_wide = wide[seq * rows:(seq + 1) * rows, :]
    wx = (xs * wide[(seq + 1) * rows:(seq + 2) * rows, :]).astype(BF16)
    dec_rows = jnp.exp(a_last)

    q = col(OFF_Q, MEM_WIDTH)
    cmb = cm.astype(BF16)
    bmb = bm.astype(BF16)
    for b in range(bblk):
        rs = slice(b * seq, (b + 1) * seq)
        att = _attention_interleaved(
            q[rs, :],
            k_ref[0, b].reshape(MEM_LEN * MEM_HEADS, MEM_HEAD_DIM).astype(BF16),
            v_ref[0, b].reshape(MEM_LEN * MEM_HEADS, MEM_HEAD_DIM).astype(BF16))
        for h in range(MEM_HEADS):
            att_sc[rs, _head_slice(h)] = att[h * seq:(h + 1) * seq, :]
        h0 = ssm_ref[b]
        h0b = h0.astype(BF16)
        dec_sc[...] = jnp.broadcast_to(dec_rows[b * seq + seq - 1:b * seq + seq, :], (LANES, DT_PAD)).T
        for g in range(SSM_GROUPS):
            gs = slice(g * SSM_STATE, (g + 1) * SSM_STATE)
            ws_ = slice(g * GROUP_WIDTH, (g + 1) * GROUP_WIDTH)
            yoff_sc[rs, ws_] = _dot_nt(cmb[rs, gs], h0b[ws_, :])
            upd = _dot_tn(wx[rs, ws_], bmb[rs, gs])
            for kk in range(HEADS_PER_GROUP):
                k = g * HEADS_PER_GROUP + kk
                hs = slice(k * SSM_HEAD_DIM, (k + 1) * SSM_HEAD_DIM)
                ssm_out_ref[b, hs, :] = (h0[hs, :] * dec_sc[k:k + 1, :]
                                         + upd[kk * SSM_HEAD_DIM:(kk + 1) * SSM_HEAD_DIM, :])

    y = (y + yoff_sc[...] * e_wide) * _silu(col(OFF_Z, SSM_WIDTH))
    mix_ref[:, GM_WIDTH:GM_WIDTH + SSM_WIDTH] = _group_rmsnorm(y, sng_ref[...]).astype(BF16)
    mix_ref[:, GM_WIDTH + SSM_WIDTH:MIX_WIDTH] = (att_sc[...] * _silu(col(OFF_MGATE, MEM_WIDTH))).astype(BF16)


def _sample_mix(proj, conv_state, ssm_state, mem_k, mem_v, gmg, gmb, gcoef, gbias, cw, cb, dtb, alog,
                dsk, sng, expand, seq):
    nb = conv_state.shape[0]
    bblk = SAMPLE_BATCH_BLOCK
    rows = bblk * seq
    assert nb % bblk == 0 and rows % (2 * SUBLANES) == 0

    def const(shape):
        return pl.BlockSpec(shape, lambda i: (0,) * len(shape))

    in_specs = [
        pl.BlockSpec((rows, IN_PACKED), lambda i: (i, 0)),
        pl.BlockSpec((bblk, CONV_WIDTH - 1, CONV_DIM), lambda i: (i, 0, 0)),
        pl.BlockSpec((bblk, SSM_WIDTH, SSM_STATE), lambda i: (i, 0, 0)),
        pl.BlockSpec((1, bblk, MEM_LEN, MEM_HEADS, MEM_HEAD_DIM), lambda i: (0, i, 0, 0, 0)),
        pl.BlockSpec((1, bblk, MEM_LEN, MEM_HEADS, MEM_HEAD_DIM), lambda i: (0, i, 0, 0, 0)),
        const((1, GM_WIDTH)), const((1, GM_WIDTH)),
        const((seq, rows, GM_WIDTH)), const((rows, GM_WIDTH)),
        const((CONV_WIDTH, CONV_DIM)), const((1, CONV_DIM)),
        const((1, DT_PAD)), const((1, DT_PAD)), const((1, SSM_WIDTH)), const((1, SSM_WIDTH)),
        const((DT_PAD, SSM_WIDTH)),
    ]
    out_specs = [
        pl.BlockSpec((rows, MIX_WIDTH), lambda i: (i, 0)),
        pl.BlockSpec((bblk, SSM_WIDTH, SSM_STATE), lambda i: (i, 0, 0)),
        pl.BlockSpec((bblk, CONV_WIDTH - 1, CONV_DIM), lambda i: (i, 0, 0)),
        pl.BlockSpec((rows, GM_WIDTH), lambda i: (i, 0)),
    ]
    out_shape = [
        jax.ShapeDtypeStruct((nb * seq, MIX_WIDTH), BF16),
        jax.ShapeDtypeStruct((nb, SSM_WIDTH, SSM_STATE), F32),
        jax.ShapeDtypeStruct((nb, CONV_WIDTH - 1, CONV_DIM), F32),
        jax.ShapeDtypeStruct((nb * seq, GM_WIDTH), F32),
    ]
    scratch = [
        pltpu.VMEM((bblk, SUBLANES, CONV_DIM), F32),
        pltpu.VMEM((rows, CONV_DIM), F32),
        pltpu.VMEM((rows, MEM_WIDTH), F32),
        pltpu.VMEM((rows, SSM_WIDTH), F32),
        pltpu.VMEM((DT_PAD, LANES), F32),
    ]
    return pl.pallas_call(
        functools.partial(_sample_mix_kernel, bblk=bblk, seq=seq),
        grid=(nb // bblk,),
        in_specs=in_specs,
        out_specs=out_specs,
        out_shape=out_shape,
        scratch_shapes=scratch,
        compiler_params=pltpu.CompilerParams(dimension_semantics=("arbitrary",),
                                             vmem_limit_bytes=VMEM_LIMIT),
        name="sample_mix",
    )(proj, conv_state, ssm_state, mem_k, mem_v, gmg, gmb, gcoef, gbias, cw, cb, dtb, alog, dsk, sng, expand)


def _sample_out_kernel(mix_ref, x_ref, w_ref, g_ref, o_ref):
    out = x_ref[...] + _dot(mix_ref[...], _unpack_rows(w_ref[...]))
    o_ref[...] = _rms(out, g_ref[...])


def _sample_out(mix, x, w_out, g):
    m = x.shape[0]
    full = lambda shape: pl.BlockSpec(shape, lambda i: (0,) * len(shape))
    return pl.pallas_call(
        _sample_out_kernel,
        grid=(1,),
        in_specs=[full((m, MIX_WIDTH)), full((m, D_MODEL)), full((MIX_WIDTH // 2, D_MODEL)), full((1, D_MODEL))],
        out_specs=full((m, D_MODEL)),
        out_shape=jax.ShapeDtypeStruct((m, D_MODEL), F32),
        compiler_params=pltpu.CompilerParams(dimension_semantics=("arbitrary",),
                                             vmem_limit_bytes=VMEM_LIMIT),
        name="sample_out",
    )(mix, x, w_out, g)


def _pack_w_in(w):
    dt_at = 2 * GM_WIDTH + GM_WIDTH + SSM_WIDTH + CONV_DIM
    assert dt_at == OFF_Q
    pad = jnp.zeros((D_MODEL, IN_PACKED - OFF_DT - SSM_HEADS), w.dtype)
    return _pack_rows(jnp.concatenate([w[:, :dt_at], w[:, dt_at + SSM_HEADS:], w[:, dt_at:dt_at + SSM_HEADS], pad],
                                      axis=1))


def _pad_heads(a):
    return jnp.pad(a.astype(F32), (0, DT_PAD - SSM_HEADS)).reshape(1, DT_PAD)


def kernel(x_prompt, x_sample, mem_prompt, state_ssm, state_conv, cache_mem_k, cache_mem_v, norm_g, w_in,
           gm_norm_g, gm_norm_b, gm_w_spatial, gm_b_spatial, conv_w, conv_b, dt_bias, a_log, d_skip,
           ssm_norm_g, mem_norm_g, w_mem_k, w_mem_v, w_out, final_norm_g):
    assert norm_g.shape[0] == 1, "single layer"
    bp, seq_p, _ = x_prompt.shape
    bs, seq_s, _ = x_sample.shape
    row = lambda a: a.reshape(1, -1).astype(F32)

    w_in_p = _pack_w_in(w_in[0])
    w_out_b = _pack_rows(w_out[0])
    ng, gmg, gmb = row(norm_g[0]), row(gm_norm_g[0]), row(gm_norm_b[0])
    cw, cb = conv_w[0].astype(F32), row(conv_b[0])
    dtb, alog = _pad_heads(dt_bias[0]), _pad_heads(a_log[0])
    dsk = row(jnp.repeat(d_skip[0], SSM_HEAD_DIM))
    sng, fg = row(ssm_norm_g[0]), row(final_norm_g)
    w_sp = gm_w_spatial[0]
    tril_p = jnp.tril(jnp.ones((CHUNK, CHUNK), bool))
    ws_p = jnp.where(tril_p, w_sp, 0).astype(BF16)
    bsf_p = jnp.repeat(gm_b_spatial[0].T, GM_HEAD_DIM, axis=1).astype(F32)

    mk, mv, mkb, mvb = _memory_kv(mem_prompt, row(mem_norm_g[0]), _pack_rows(w_mem_k[0]), _pack_rows(w_mem_v[0]))
    y_p, ssm_p, conv_p = _prompt_layer(x_prompt, mkb, mvb, w_in_p, w_out_b, ng, gmg, gmb, ws_p, bsf_p,
                                       cw, cb, dtb, alog, dsk, sng, fg)

    rows = SAMPLE_BATCH_BLOCK * seq_s
    tpos = jnp.arange(rows) % seq_s
    gcoef = jnp.stack([
        jnp.where((tpos >= j)[:, None],
                  jnp.repeat(w_sp[:, tpos, jnp.maximum(tpos - j, 0)].T, GM_HEAD_DIM, axis=1), 0.0)
        for j in range(seq_s)]).astype(F32)
    gbias = jnp.repeat(gm_b_spatial[0][:, tpos].T, GM_HEAD_DIM, axis=1).astype(F32)
    expand = (jnp.arange(DT_PAD)[:, None] == (jnp.arange(SSM_WIDTH) // SSM_HEAD_DIM)[None, :]).astype(BF16)

    xs2 = x_sample.reshape(bs * seq_s, D_MODEL)
    proj_s = _sample_proj(xs2, ng, w_in_p)
    mix_s, ssm_s, conv_s, gv_s = _sample_mix(
        proj_s, state_conv[0], state_ssm[0].reshape(bs, SSM_WIDTH, SSM_STATE),
        cache_mem_k, cache_mem_v,
        gmg, gmb, gcoef, gbias, cw, cb, dtb, alog, dsk, sng, expand, seq_s)
    y_s = _sample_out(mix_s, xs2, w_out_b, fg)

    return (y_p,
            y_s.reshape(bs, seq_s, D_MODEL),
            ssm_p.reshape(1, bp, SSM_HEADS, SSM_HEAD_DIM, SSM_STATE),
            conv_p[None],
            mk,
            mv,
            ssm_s.reshape(1, bs, SSM_HEADS, SSM_HEAD_DIM, SSM_STATE),
            conv_s[None],
            gv_s.reshape(1, bs, seq_s, GM_WIDTH))
```

```python
import functools
import math

import jax
import jax.numpy as jnp
from jax import lax
from jax.experimental import pallas as pl
from jax.experimental.pallas import tpu as pltpu

F32 = jnp.float32
BF16 = jnp.bfloat16

D_MODEL = 1024
GM_WIDTH = 1024
GM_HEADS = 8
GM_HEAD_DIM = 128
CHUNK = 128
SSM_WIDTH = 1024
SSM_HEADS = 16
SSM_HEAD_DIM = 64
SSM_GROUPS = 2
SSM_STATE = 128
HEADS_PER_GROUP = SSM_HEADS // SSM_GROUPS
GROUP_WIDTH = SSM_WIDTH // SSM_GROUPS
CONV_WIDTH = 4
CONV_DIM = SSM_WIDTH + 2 * SSM_GROUPS * SSM_STATE
MEM_LEN = 256
MEM_HEADS = 4
MEM_HEAD_DIM = 256
MEM_WIDTH = 1024
MIX_WIDTH = GM_WIDTH + SSM_WIDTH + MEM_WIDTH
EPS = 1e-6

LANES = 128
SUBLANES = 8

DT_PAD = LANES
OFF_U = (0, 0)
OFF_V = (0, GM_WIDTH)
OFF_GATE = (0, 2 * GM_WIDTH)
OFF_Z = (0, 3 * GM_WIDTH)
OFF_XBC = (0, 3 * GM_WIDTH + SSM_WIDTH)
WIDTH_A = 3 * GM_WIDTH + SSM_WIDTH + CONV_DIM
OFF_Q = (1, 0)
OFF_MGATE = (1, MEM_WIDTH)
OFF_DT = (1, 2 * MEM_WIDTH)
WIDTH_B = 2 * MEM_WIDTH + DT_PAD
PROJ_BLOCK = 512

PROMPT_TILE = 256
SAMPLE_BATCH_BLOCK = 4
VMEM_LIMIT = 56 * 1024 * 1024


def _rms(x, g):
    return x * lax.rsqrt(jnp.mean(x * x, axis=-1, keepdims=True) + EPS) * g


def _gelu(x):
    return 0.5 * x * (1.0 + lax.erf(x * math.sqrt(0.5)))


def _silu(x):
    return x * jax.nn.sigmoid(x)


def _softplus(x):
    return jnp.maximum(x, 0.0) + jnp.log1p(jnp.exp(-jnp.abs(x)))


def _layernorm(x, g, b):
    mu = jnp.mean(x, axis=-1, keepdims=True)
    xc = x - mu
    var = jnp.mean(xc * xc, axis=-1, keepdims=True)
    return xc * lax.rsqrt(var + EPS) * g + b


def _dot(a, b):
    return jnp.dot(a, b, preferred_element_type=F32)


def _dot_nt(a, b):
    return lax.dot_general(a, b, (((1,), (1,)), ((), ())), preferred_element_type=F32)


def _dot_tn(a, b):
    return lax.dot_general(a, b, (((0,), (0,)), ((), ())), preferred_element_type=F32)


def _group_rmsnorm(y, g):
    halves = []
    for i in range(SSM_GROUPS):
        yg = y[:, i * GROUP_WIDTH:(i + 1) * GROUP_WIDTH]
        halves.append(yg * lax.rsqrt(jnp.mean(yg * yg, axis=-1, keepdims=True) + EPS))
    return jnp.concatenate(halves, axis=-1) * g


def _memory_kv_kernel(mem_ref, g_ref, wk_ref, wv_ref, k_ref, v_ref, kb_ref, vb_ref):
    m = _rms(mem_ref[0], g_ref[...]).astype(BF16)
    k = _dot(m, wk_ref[...])
    v = _dot(m, wv_ref[...])
    for h in range(MEM_HEADS):
        k_ref[0, 0, :, h, :] = k[:, _head_slice(h)]
        v_ref[0, 0, :, h, :] = v[:, _head_slice(h)]
    kb_ref[0] = k.astype(BF16)
    vb_ref[0] = v.astype(BF16)


def _memory_kv(mem, g, wk, wv):
    b = mem.shape[0]
    blk = pl.BlockSpec((1, MEM_LEN, D_MODEL), lambda i: (i, 0, 0))
    blk5 = pl.BlockSpec((1, 1, MEM_LEN, MEM_HEADS, MEM_HEAD_DIM), lambda i: (0, i, 0, 0, 0))
    const = lambda shape: pl.BlockSpec(shape, lambda i: (0,) * len(shape))
    return pl.pallas_call(
        _memory_kv_kernel,
        grid=(b,),
        in_specs=[blk, const((1, D_MODEL)), const((D_MODEL, MEM_WIDTH)), const((D_MODEL, MEM_WIDTH))],
        out_specs=[blk5, blk5, blk, blk],
        out_shape=[jax.ShapeDtypeStruct((1, b, MEM_LEN, MEM_HEADS, MEM_HEAD_DIM), F32)] * 2
        + [jax.ShapeDtypeStruct((b, MEM_LEN, MEM_WIDTH), BF16)] * 2,
        compiler_params=pltpu.CompilerParams(dimension_semantics=("arbitrary",),
                                             vmem_limit_bytes=VMEM_LIMIT),
        name="memory_kv",
    )(mem, g, wk, wv)


def _head_slice(h):
    return slice(h * MEM_HEAD_DIM, (h + 1) * MEM_HEAD_DIM)


def _attention(q, k_head, v_head):
    qb = (q * (MEM_HEAD_DIM ** -0.5)).astype(BF16)
    outs = []
    for h in range(MEM_HEADS):
        s = _dot_nt(qb[:, _head_slice(h)], k_head(h))
        p = jnp.exp(s - jnp.max(s, axis=-1, keepdims=True))
        denom = jnp.sum(p, axis=-1, keepdims=True)
        outs.append(_dot(p.astype(BF16), v_head(h)) / denom)
    return jnp.concatenate(outs, axis=-1)


def _attention_interleaved(q, k2, v2):
    t = q.shape[0]
    qb = (q * (MEM_HEAD_DIM ** -0.5)).astype(BF16)
    q2 = jnp.concatenate([qb[:, _head_slice(h)] for h in range(MEM_HEADS)], axis=0)
    s = _dot_nt(q2, k2)
    row_head = lax.broadcasted_iota(jnp.int32, s.shape, 0) // t
    col_head = lax.broadcasted_iota(jnp.int32, s.shape, 1) % MEM_HEADS
    s = jnp.where(row_head == col_head, s, -jnp.inf)
    p = jnp.exp(s - jnp.max(s, axis=-1, keepdims=True))
    denom = jnp.sum(p, axis=-1, keepdims=True)
    return _dot(p.astype(BF16), v2) / denom


class _Handoff:
    N = 8

    def __init__(self, refs):
        (self.mix, self.ug, self.v, self.zs, self.xs, self.bm, self.cm, self.dt) = refs


def _prompt_step(xc_ref, xp_ref, kb_ref, vb_ref, win_refs, wout_ref, ng_ref, gmg_ref, gmb_ref, ws_ref, bsf_ref,
                 cw_ref, cb_ref, dtb_ref, alog_ref, dsk_ref, sng_ref, fg_ref, y_ref,
                 hn_sc, xp_sc, y_sc, ht_sc, ssd_sc, out_sc, new, old, tile):
    a_row = -jnp.exp(alog_ref[...])
    row = lax.broadcasted_iota(jnp.int32, (CHUNK, CHUNK), 0)
    col = lax.broadcasted_iota(jnp.int32, (CHUNK, CHUNK), 1)
    causal = row >= col
    tril = causal.astype(F32)
    head_lane = col < SSM_HEADS
    first_half = col < SSM_HEAD_DIM
    acum_sc, acum_t_sc, dt_t_sc, to_end_t_sc, cb_sc, bm_t_sc = (ssd_sc.at[i] for i in range(6))

    def proj(off, width):
        part, c0 = off
        return _dot(hn_sc[...], win_refs[part][:, c0:c0 + width])

    def gmlp_mix(c):
        r = slice(c * CHUNK, (c + 1) * CHUNK)
        for h in range(GM_HEADS):
            hs = slice(h * GM_HEAD_DIM, (h + 1) * GM_HEAD_DIM)
            mixed = _dot(ws_ref[h], old.v[r, hs]) + bsf_ref[:, hs]
            old.mix[r, hs] = (old.ug[r, hs] * mixed).astype(BF16)

    def seq_chunk_start(c):
        r = slice(c * CHUNK, (c + 1) * CHUNK)
        dt = old.dt[r, :]
        adt = jnp.where(head_lane, dt * a_row, 0.0)
        acum = jnp.dot(tril, adt, precision=lax.Precision.HIGHEST, preferred_element_type=F32)
        acum_t = acum.T
        dt_t = dt.T
        acum_sc[...] = acum
        acum_t_sc[...] = acum_t
        dt_t_sc[...] = dt_t
        to_end_t_sc[...] = jnp.exp(acum_t[:, CHUNK - 1:CHUNK] - acum_t) * dt_t

    def seq_group(c, g):
        r = slice(c * CHUNK, (c + 1) * CHUNK)
        gs = slice(g * SSM_STATE, (g + 1) * SSM_STATE)
        bm_g = old.bm[r, gs]
        cb_sc[...] = _dot_nt(old.cm[r, gs].astype(BF16), bm_g.astype(BF16))
        bm_t_sc[...] = bm_g.T
        for kk in range(0, HEADS_PER_GROUP, 2):
            k0 = g * HEADS_PER_GROUP + kk
            ps = slice(k0 * SSM_HEAD_DIM, (k0 + 2) * SSM_HEAD_DIM)
            cm_g = old.cm[r, gs]
            lhs_parts, b_parts, decs = [], [], []
            for k in (k0, k0 + 1):
                a_col = jnp.broadcast_to(acum_sc[:, k:k + 1], (CHUNK, CHUNK))
                diff = a_col - acum_t_sc[k:k + 1, :]
                decay = jnp.exp(jnp.where(causal, diff, -jnp.inf))
                lhs_parts.append((cb_sc[...] * decay * dt_t_sc[k:k + 1, :]).astype(BF16))
                lhs_parts.append((cm_g * jnp.exp(a_col)).astype(BF16))
                b_parts.append((bm_t_sc[...] * to_end_t_sc[k:k + 1, :]).astype(BF16))
                decs.append(jnp.exp(a_col[CHUNK - 1:CHUNK, :]))
            xs_pair = old.xs[r, ps]
            h_pair = ht_sc[:, ps]
            zero = jnp.zeros_like(xs_pair)
            xs_lo = jnp.where(first_half, xs_pair, zero).astype(BF16)
            xs_hi = jnp.where(first_half, zero, xs_pair).astype(BF16)
            h_lo = jnp.where(first_half, h_pair, zero).astype(BF16)
            h_hi = jnp.where(first_half, zero, h_pair).astype(BF16)
            lhs = jnp.concatenate(lhs_parts, axis=1)
            rhs = jnp.concatenate([xs_lo, h_lo, xs_hi, h_hi], axis=0)
            y_sc[:, ps] = _dot(lhs, rhs)
            upd = _dot(jnp.concatenate(b_parts, axis=1), jnp.concatenate([xs_lo, xs_hi], axis=0))
            dec = jnp.where(first_half[0:1, :], decs[0], decs[1])
            ht_sc[:, ps] = h_pair * dec + upd

    def seq_chunk_end(c):
        r = slice(c * CHUNK, (c + 1) * CHUNK)
        y = (y_sc[...] + dsk_ref[...] * old.xs[r, :]) * old.zs[r, :]
        old.mix[r, GM_WIDTH:GM_WIDTH + SSM_WIDTH] = _group_rmsnorm(y, sng_ref[...]).astype(BF16)

    def out_piece(j, n):
        cs = slice(j * D_MODEL // n, (j + 1) * D_MODEL // n)
        out_sc[:, cs] = xp_ref[0, :, cs] + _dot(old.mix[...], wout_ref[:, cs])

    assert tile // CHUNK == 2, "the emission order below is written for two chunks per tile"

    hn_sc[...] = _rms(xc_ref[0], ng_ref[...]).astype(BF16)

    new.zs[...] = _silu(proj(OFF_Z, SSM_WIDTH))
    xp_sc[SUBLANES:SUBLANES + tile, :] = proj(OFF_XBC, CONV_DIM)
    acc = cb_ref[...]
    for j in range(CONV_WIDTH):
        lo = SUBLANES - (CONV_WIDTH - 1) + j
        acc = acc + cw_ref[j:j + 1, :] * xp_sc[lo:lo + tile, :]
    xbc = _silu(acc)
    new.xs[...] = xbc[:, :SSM_WIDTH]
    new.bm[...] = xbc[:, SSM_WIDTH:SSM_WIDTH + SSM_GROUPS * SSM_STATE]
    new.cm[...] = xbc[:, SSM_WIDTH + SSM_GROUPS * SSM_STATE:]
    xp_sc[SUBLANES - (CONV_WIDTH - 1):SUBLANES, :] = xp_sc[tile + SUBLANES - (CONV_WIDTH - 1):tile + SUBLANES, :]
    new.dt[...] = _softplus(proj(OFF_DT, DT_PAD) + dtb_ref[...])
    u = _gelu(proj(OFF_U, GM_WIDTH))
    new.ug[...] = u * _silu(proj(OFF_GATE, GM_WIDTH))
    new.v[...] = _layernorm(_gelu(proj(OFF_V, GM_WIDTH)), gmg_ref[...], gmb_ref[...]).astype(BF16)
    gmlp_mix(0)
    gmlp_mix(1)
    seq_chunk_start(0)
    seq_group(0, 0)
    seq_group(0, 1)
    seq_chunk_end(0)
    att = _attention(proj(OFF_Q, MEM_WIDTH), lambda h: kb_ref[0, :, _head_slice(h)], lambda h: vb_ref[0, :, _head_slice(h)])
    new.mix[:, GM_WIDTH + SSM_WIDTH:MIX_WIDTH] = (att * _silu(proj(OFF_MGATE, MEM_WIDTH))).astype(BF16)
    seq_chunk_start(1)
    seq_group(1, 0)
    seq_group(1, 1)
    seq_chunk_end(1)

    n_out = 4
    for j in range(n_out):
        out_piece(j, n_out)
    y_ref[0] = _rms(out_sc[...], fg_ref[...])


def _prompt_kernel(xc_ref, kb_ref, vb_ref, wa_ref, wb_ref, wout_ref, ng_ref, gmg_ref, gmb_ref,
                   ws_ref, bsf_ref, cw_ref, cb_ref, dtb_ref, alog_ref, dsk_ref, sng_ref, fg_ref,
                   y_ref, ssm_ref, conv_ref, hn_sc, xpad_sc, y_sc, ht_sc, ssd_sc, out_sc, *handoff, tile, nt, total):
    t = pl.program_id(0) % nt
    tile_set = _Handoff(handoff)

    @pl.when(t == 0)
    def _():
        xpad_sc[0:SUBLANES, :] = jnp.zeros((SUBLANES, CONV_DIM), F32)
        ht_sc[...] = jnp.zeros_like(ht_sc)

    _prompt_step(xc_ref, xc_ref, kb_ref, vb_ref, (wa_ref, wb_ref), wout_ref, ng_ref, gmg_ref, gmb_ref, ws_ref, bsf_ref,
                 cw_ref, cb_ref, dtb_ref, alog_ref, dsk_ref, sng_ref, fg_ref, y_ref,
                 hn_sc, xpad_sc, y_sc, ht_sc, ssd_sc, out_sc, tile_set, tile_set, tile)

    @pl.when(t == nt - 1)
    def _():
        conv_ref[0] = xpad_sc[SUBLANES - (CONV_WIDTH - 1):SUBLANES, :]
        ssm_ref[0] = ht_sc[...].T


def _prompt_layer(x, kb, vb, w_a, w_b, w_out, ng, gmg, gmb, ws, bsf, cw, cb, dtb, alog, dsk, sng, fg):
    b, seq, _ = x.shape
    tile = PROMPT_TILE
    nt = seq // tile
    total = b * nt
    assert seq % tile == 0 and tile % CHUNK == 0

    def const(shape):
        return pl.BlockSpec(shape, lambda s: (0,) * len(shape), pipeline_mode=pl.Buffered(1))

    cur = lambda s: s
    prev = lambda s: s
    in_specs = [
        pl.BlockSpec((1, tile, D_MODEL), lambda s: (cur(s) // nt, cur(s) % nt, 0)),
        pl.BlockSpec((1, MEM_LEN, MEM_WIDTH), lambda s: (cur(s) // nt, 0, 0)),
        pl.BlockSpec((1, MEM_LEN, MEM_WIDTH), lambda s: (cur(s) // nt, 0, 0)),
        const((D_MODEL, WIDTH_A)),
        const((D_MODEL, WIDTH_B)),
        const((MIX_WIDTH, D_MODEL)),
        const((1, D_MODEL)), const((1, GM_WIDTH)), const((1, GM_WIDTH)),
        const((GM_HEADS, CHUNK, CHUNK)), const((CHUNK, GM_WIDTH)),
        const((CONV_WIDTH, CONV_DIM)), const((1, CONV_DIM)),
        const((1, DT_PAD)), const((1, DT_PAD)), const((1, SSM_WIDTH)), const((1, SSM_WIDTH)),
        const((1, D_MODEL)),
    ]
    out_specs = [
        pl.BlockSpec((1, tile, D_MODEL), lambda s: (prev(s) // nt, prev(s) % nt, 0)),
        pl.BlockSpec((1, SSM_WIDTH, SSM_STATE), lambda s: (prev(s) // nt, 0, 0)),
        pl.BlockSpec((1, CONV_WIDTH - 1, CONV_DIM), lambda s: (cur(s) // nt, 0, 0)),
    ]
    out_shape = [
        jax.ShapeDtypeStruct((b, seq, D_MODEL), F32),
        jax.ShapeDtypeStruct((b, SSM_WIDTH, SSM_STATE), F32),
        jax.ShapeDtypeStruct((b, CONV_WIDTH - 1, CONV_DIM), F32),
    ]
    handoff = [
        pltpu.VMEM((tile, MIX_WIDTH), BF16),
        pltpu.VMEM((tile, GM_WIDTH), F32),
        pltpu.VMEM((tile, GM_WIDTH), BF16),
        pltpu.VMEM((tile, SSM_WIDTH), F32),
        pltpu.VMEM((tile, SSM_WIDTH), F32),
        pltpu.VMEM((tile, SSM_GROUPS * SSM_STATE), F32),
        pltpu.VMEM((tile, SSM_GROUPS * SSM_STATE), F32),
        pltpu.VMEM((tile, DT_PAD), F32),
    ]
    assert len(handoff) == _Handoff.N
    scratch = [
        pltpu.VMEM((tile, D_MODEL), BF16),
        pltpu.VMEM((tile + SUBLANES, CONV_DIM), F32),
        pltpu.VMEM((CHUNK, SSM_WIDTH), F32),
        pltpu.VMEM((SSM_STATE, SSM_WIDTH), F32),
        pltpu.VMEM((6, CHUNK, CHUNK), F32),
        pltpu.VMEM((tile, D_MODEL), F32),
    ] + handoff
    return pl.pallas_call(
        functools.partial(_prompt_kernel, tile=tile, nt=nt, total=total),
        grid=(total,),
        in_specs=in_specs,
        out_specs=out_specs,
        out_shape=out_shape,
        scratch_shapes=scratch,
        compiler_params=pltpu.CompilerParams(dimension_semantics=("arbitrary",),
                                             vmem_limit_bytes=VMEM_LIMIT),
        name="prompt_layer",
    )(x, kb, vb, w_a, w_b, w_out, ng, gmg, gmb, ws, bsf, cw, cb, dtb, alog, dsk, sng, fg)


def _sample_proj_kernel(x_ref, g_ref, w_ref, o_ref):
    hn = _rms(x_ref[...], g_ref[...]).astype(BF16)
    o_ref[...] = _dot(hn, w_ref[...])


def _sample_proj(x, g, w, block, name):
    m, n = x.shape[0], w.shape[1]
    assert n % block == 0
    return pl.pallas_call(
        _sample_proj_kernel,
        grid=(n // block,),
        in_specs=[pl.BlockSpec((m, D_MODEL), lambda j: (0, 0)),
                  pl.BlockSpec((1, D_MODEL), lambda j: (0, 0)),
                  pl.BlockSpec((D_MODEL, block), lambda j: (0, j))],
        out_specs=pl.BlockSpec((m, block), lambda j: (0, j)),
        out_shape=jax.ShapeDtypeStruct((m, n), F32),
        compiler_params=pltpu.CompilerParams(dimension_semantics=("arbitrary",),
                                             vmem_limit_bytes=VMEM_LIMIT),
        name=name,
    )(x, g, w)


def _sample_mix_kernel(pa_ref, pb_ref, cst_ref, ssm_ref, k_ref, v_ref, gmg_ref, gmb_ref, gcoef_ref, gbias_ref,
                       cw_ref, cb_ref, dtb_ref, alog_ref, dsk_ref, sng_ref, expand_ref,
                       mix_ref, ssm_out_ref, conv_out_ref, gv_ref,
                       xp_sc, xbc_sc, att_sc, yoff_sc, dec_sc, *, bblk, seq):
    rows = bblk * seq

    def col(off, width):
        part, c0 = off
        return (pa_ref, pb_ref)[part][:, c0:c0 + width]

    tpos = lax.broadcasted_iota(jnp.int32, (rows, 1), 0) % seq

    def back(a, j):
        return a if j == 0 else pltpu.roll(a, j, 0)

    u = _gelu(col(OFF_U, GM_WIDTH))
    v = _layernorm(_gelu(col(OFF_V, GM_WIDTH)), gmg_ref[...], gmb_ref[...])
    gv_ref[...] = v
    mixed = gbias_ref[...]
    for j in range(seq):
        mixed = mixed + gcoef_ref[j] * back(v, j)
    mix_ref[:, 0:GM_WIDTH] = (u * mixed * _silu(col(OFF_GATE, GM_WIDTH))).astype(BF16)

    xbc_raw = col(OFF_XBC, CONV_DIM)
    for b in range(bblk):
        xp_sc[b, 0:CONV_WIDTH - 1, :] = cst_ref[b]
        xp_sc[b, CONV_WIDTH - 1:CONV_WIDTH - 1 + seq, :] = xbc_raw[b * seq:(b + 1) * seq, :]
    for b in range(bblk):
        acc = jnp.broadcast_to(cb_ref[...], (seq, CONV_DIM))
        for j in range(CONV_WIDTH):
            acc = acc + cw_ref[j:j + 1, :] * xp_sc[b, j:j + seq, :]
        xbc_sc[b * seq:(b + 1) * seq, :] = _silu(acc)
        conv_out_ref[b] = xp_sc[b, seq:seq + CONV_WIDTH - 1, :]
    xs = xbc_sc[:, 0:SSM_WIDTH]
    bm = xbc_sc[:, SSM_WIDTH:SSM_WIDTH + SSM_GROUPS * SSM_STATE]
    cm = xbc_sc[:, SSM_WIDTH + SSM_GROUPS * SSM_STATE:CONV_DIM]

    lane = lax.broadcasted_iota(jnp.int32, (rows, DT_PAD), 1)
    dt = _softplus(col(OFF_DT, DT_PAD) + dtb_ref[...])
    adt = jnp.where(lane < SSM_HEADS, dt * (-jnp.exp(alog_ref[...])), 0.0)
    acum = adt
    for j in range(1, seq):
        acum = acum + jnp.where(tpos >= j, back(adt, j), 0.0)
    a_last = jnp.zeros_like(acum)
    for j in range(seq):
        a_last = a_last + jnp.where(tpos == seq - 1 - j, acum if j == 0 else pltpu.roll(acum, rows - j, 0), 0.0)
    coefs = []
    for j in range(seq):
        cbj = []
        for g in range(SSM_GROUPS):
            gs = slice(g * SSM_STATE, (g + 1) * SSM_STATE)
            cbj.append(jnp.sum(cm[:, gs] * back(bm[:, gs], j), axis=-1, keepdims=True))
        cb_l = jnp.where(lane < HEADS_PER_GROUP, cbj[0], cbj[1])
        valid = tpos >= j
        decay = jnp.exp(jnp.where(valid, acum - back(acum, j), 0.0))
        coefs.append(jnp.where(valid, cb_l * decay * back(dt, j), 0.0))
    coefs.append(jnp.exp(acum))
    coefs.append(dt * jnp.exp(a_last - acum))
    stack = jnp.concatenate(coefs, axis=0)
    hi = stack.astype(BF16)
    lo = (stack - hi.astype(F32)).astype(BF16)
    wide = _dot(hi, expand_ref[...]) + _dot(lo, expand_ref[...])
    y = dsk_ref[...] * xs
    for j in range(seq):
        y = y + wide[j * rows:(j + 1) * rows, :] * back(xs, j)
    e_wide = wide[seq * rows:(seq + 1) * rows, :]
    wx = (xs * wide[(seq + 1) * rows:(seq + 2) * rows, :]).astype(BF16)
    dec_rows = jnp.exp(a_last)

    q = col(OFF_Q, MEM_WIDTH)
    cmb = cm.astype(BF16)
    bmb = bm.astype(BF16)
    for b in range(bblk):
        rs = slice(b * seq, (b + 1) * seq)
        att = _attention_interleaved(
            q[rs, :],
            k_ref[0, b].reshape(MEM_LEN * MEM_HEADS, MEM_HEAD_DIM).astype(BF16),
            v_ref[0, b].reshape(MEM_LEN * MEM_HEADS, MEM_HEAD_DIM).astype(BF16))
        for h in range(MEM_HEADS):
            att_sc[rs, _head_slice(h)] = att[h * seq:(h + 1) * seq, :]
        h0 = ssm_ref[b]
        h0b = h0.astype(BF16)
        dec_sc[...] = jnp.broadcast_to(dec_rows[b * seq + seq - 1:b * seq + seq, :], (LANES, DT_PAD)).T
        for g in range(SSM_GROUPS):
            gs = slice(g * SSM_STATE, (g + 1) * SSM_STATE)
            ws_ = slice(g * GROUP_WIDTH, (g + 1) * GROUP_WIDTH)
            yoff_sc[rs, ws_] = _dot_nt(cmb[rs, gs], h0b[ws_, :])
            upd = _dot_tn(wx[rs, ws_], bmb[rs, gs])
            for kk in range(HEADS_PER_GROUP):
                k = g * HEADS_PER_GROUP + kk
                hs = slice(k * SSM_HEAD_DIM, (k + 1) * SSM_HEAD_DIM)
                ssm_out_ref[b, hs, :] = (h0[hs, :] * dec_sc[k:k + 1, :]
                                         + upd[kk * SSM_HEAD_DIM:(kk + 1) * SSM_HEAD_DIM, :])

    y = (y + yoff_sc[...] * e_wide) * _silu(col(OFF_Z, SSM_WIDTH))
    mix_ref[:, GM_WIDTH:GM_WIDTH + SSM_WIDTH] = _group_rmsnorm(y, sng_ref[...]).astype(BF16)
    mix_ref[:, GM_WIDTH + SSM_WIDTH:MIX_WIDTH] = (att_sc[...] * _silu(col(OFF_MGATE, MEM_WIDTH))).astype(BF16)


def _sample_mix(proj_a, proj_b, conv_state, ssm_state, mem_k, mem_v, gmg, gmb, gcoef, gbias, cw, cb, dtb, alog,
                dsk, sng, expand, seq):
    nb = conv_state.shape[0]
    bblk = SAMPLE_BATCH_BLOCK
    rows = bblk * seq
    assert nb % bblk == 0 and rows % (2 * SUBLANES) == 0

    def const(shape):
        return pl.BlockSpec(shape, lambda i: (0,) * len(shape))

    in_specs = [
        pl.BlockSpec((rows, WIDTH_A), lambda i: (i, 0)),
        pl.BlockSpec((rows, WIDTH_B), lambda i: (i, 0)),
        pl.BlockSpec((bblk, CONV_WIDTH - 1, CONV_DIM), lambda i: (i, 0, 0)),
        pl.BlockSpec((bblk, SSM_WIDTH, SSM_STATE), lambda i: (i, 0, 0)),
        pl.BlockSpec((1, bblk, MEM_LEN, MEM_HEADS, MEM_HEAD_DIM), lambda i: (0, i, 0, 0, 0)),
        pl.BlockSpec((1, bblk, MEM_LEN, MEM_HEADS, MEM_HEAD_DIM), lambda i: (0, i, 0, 0, 0)),
        const((1, GM_WIDTH)), const((1, GM_WIDTH)),
        const((seq, rows, GM_WIDTH)), const((rows, GM_WIDTH)),
        const((CONV_WIDTH, CONV_DIM)), const((1, CONV_DIM)),
        const((1, DT_PAD)), const((1, DT_PAD)), const((1, SSM_WIDTH)), const((1, SSM_WIDTH)),
        const((DT_PAD, SSM_WIDTH)),
    ]
    out_specs = [
        pl.BlockSpec((rows, MIX_WIDTH), lambda i: (i, 0)),
        pl.BlockSpec((bblk, SSM_WIDTH, SSM_STATE), lambda i: (i, 0, 0)),
        pl.BlockSpec((bblk, CONV_WIDTH - 1, CONV_DIM), lambda i: (i, 0, 0)),
        pl.BlockSpec((rows, GM_WIDTH), lambda i: (i, 0)),
    ]
    out_shape = [
        jax.ShapeDtypeStruct((nb * seq, MIX_WIDTH), BF16),
        jax.ShapeDtypeStruct((nb, SSM_WIDTH, SSM_STATE), F32),
        jax.ShapeDtypeStruct((nb, CONV_WIDTH - 1, CONV_DIM), F32),
        jax.ShapeDtypeStruct((nb * seq, GM_WIDTH), F32),
    ]
    scratch = [
        pltpu.VMEM((bblk, SUBLANES, CONV_DIM), F32),
        pltpu.VMEM((rows, CONV_DIM), F32),
        pltpu.VMEM((rows, MEM_WIDTH), F32),
        pltpu.VMEM((rows, SSM_WIDTH), F32),
        pltpu.VMEM((DT_PAD, LANES), F32),
    ]
    return pl.pallas_call(
        functools.partial(_sample_mix_kernel, bblk=bblk, seq=seq),
        grid=(nb // bblk,),
        in_specs=in_specs,
        out_specs=out_specs,
        out_shape=out_shape,
        scratch_shapes=scratch,
        compiler_params=pltpu.CompilerParams(dimension_semantics=("arbitrary",),
                                             vmem_limit_bytes=VMEM_LIMIT),
        name="sample_mix",
    )(proj_a, proj_b, conv_state, ssm_state, mem_k, mem_v, gmg, gmb, gcoef, gbias, cw, cb, dtb, alog, dsk, sng, expand)


def _sample_out_kernel(mix_ref, x_ref, w_ref, g_ref, o_ref):
    out = x_ref[...] + _dot(mix_ref[...], w_ref[...])
    o_ref[...] = _rms(out, g_ref[...])


def _sample_out(mix, x, w_out, g):
    m = x.shape[0]
    full = lambda shape: pl.BlockSpec(shape, lambda i: (0,) * len(shape))
    return pl.pallas_call(
        _sample_out_kernel,
        grid=(1,),
        in_specs=[full((m, MIX_WIDTH)), full((m, D_MODEL)), full((MIX_WIDTH, D_MODEL)), full((1, D_MODEL))],
        out_specs=full((m, D_MODEL)),
        out_shape=jax.ShapeDtypeStruct((m, D_MODEL), F32),
        compiler_params=pltpu.CompilerParams(dimension_semantics=("arbitrary",),
                                             vmem_limit_bytes=VMEM_LIMIT),
        name="sample_out",
    )(mix, x, w_out, g)


def _split_w_in(w):
    w_a = w[:, :WIDTH_A].astype(BF16)
    dt = w[:, WIDTH_A:WIDTH_A + SSM_HEADS]
    pad = jnp.zeros((D_MODEL, DT_PAD - SSM_HEADS), w.dtype)
    w_b = jnp.concatenate([w[:, WIDTH_A + SSM_HEADS:], dt, pad], axis=1).astype(BF16)
    return w_a, w_b


def _pad_heads(a):
    return jnp.pad(a.astype(F32), (0, DT_PAD - SSM_HEADS)).reshape(1, DT_PAD)


def kernel(x_prompt, x_sample, mem_prompt, state_ssm, state_conv, cache_mem_k, cache_mem_v, norm_g, w_in,
           gm_norm_g, gm_norm_b, gm_w_spatial, gm_b_spatial, conv_w, conv_b, dt_bias, a_log, d_skip,
           ssm_norm_g, mem_norm_g, w_mem_k, w_mem_v, w_out, final_norm_g):
    assert norm_g.shape[0] == 1, "single layer"
    bp, seq_p, _ = x_prompt.shape
    bs, seq_s, _ = x_sample.shape
    row = lambda a: a.reshape(1, -1).astype(F32)

    w_a, w_b = _split_w_in(w_in[0])
    w_out_b = w_out[0].astype(BF16)
    ng, gmg, gmb = row(norm_g[0]), row(gm_norm_g[0]), row(gm_norm_b[0])
    cw, cb = conv_w[0].astype(F32), row(conv_b[0])
    dtb, alog = _pad_heads(dt_bias[0]), _pad_heads(a_log[0])
    dsk = row(jnp.repeat(d_skip[0], SSM_HEAD_DIM))
    sng, fg = row(ssm_norm_g[0]), row(final_norm_g)
    w_sp = gm_w_spatial[0]
    tril_p = jnp.tril(jnp.ones((CHUNK, CHUNK), bool))
    ws_p = jnp.where(tril_p, w_sp, 0).astype(BF16)
    bsf_p = jnp.repeat(gm_b_spatial[0].T, GM_HEAD_DIM, axis=1).astype(F32)

    mk, mv, mkb, mvb = _memory_kv(mem_prompt, row(mem_norm_g[0]), w_mem_k[0].astype(BF16),
                                  w_mem_v[0].astype(BF16))
    y_p, ssm_p, conv_p = _prompt_layer(x_prompt, mkb, mvb, w_a, w_b, w_out_b, ng, gmg, gmb, ws_p, bsf_p,
                                       cw, cb, dtb, alog, dsk, sng, fg)

    rows = SAMPLE_BATCH_BLOCK * seq_s
    tpos = jnp.arange(rows) % seq_s
    gcoef = jnp.stack([
        jnp.where((tpos >= j)[:, None],
                  jnp.repeat(w_sp[:, tpos, jnp.maximum(tpos - j, 0)].T, GM_HEAD_DIM, axis=1), 0.0)
        for j in range(seq_s)]).astype(F32)
    gbias = jnp.repeat(gm_b_spatial[0][:, tpos].T, GM_HEAD_DIM, axis=1).astype(F32)
    expand = (jnp.arange(DT_PAD)[:, None] == (jnp.arange(SSM_WIDTH) // SSM_HEAD_DIM)[None, :]).astype(BF16)

    xs2 = x_sample.reshape(bs * seq_s, D_MODEL)
    proj_a = _sample_proj(xs2, ng, w_a, PROJ_BLOCK, "sample_proj_a")
    proj_b = _sample_proj(xs2, ng, w_b, WIDTH_B, "sample_proj_b")
    mix_s, ssm_s, conv_s, gv_s = _sample_mix(
        proj_a, proj_b, state_conv[0], state_ssm[0].reshape(bs, SSM_WIDTH, SSM_STATE),
        cache_mem_k, cache_mem_v,
        gmg, gmb, gcoef, gbias, cw, cb, dtb, alog, dsk, sng, expand, seq_s)
    y_s = _sample_out(mix_s, xs2, w_out_b, fg)

    return (y_p,
            y_s.reshape(bs, seq_s, D_MODEL),
            ssm_p.reshape(1, bp, SSM_HEADS, SSM_HEAD_DIM, SSM_STATE),
            conv_p[None],
            mk,
            mv,
            ssm_s.reshape(1, bs, SSM_HEADS, SSM_HEAD_DIM, SSM_STATE),
            conv_s[None],
            gv_s.reshape(1, bs, seq_s, GM_WIDTH))
```

```python
import functools
import math

import jax
import jax.numpy as jnp
from jax import lax
from jax.experimental import pallas as pl
from jax.experimental.pallas import tpu as pltpu

F32 = jnp.float32
BF16 = jnp.bfloat16

D_MODEL = 1024
GM_WIDTH = 1024
GM_HEADS = 8
GM_HEAD_DIM = 128
CHUNK = 128
SSM_WIDTH = 1024
SSM_HEADS = 16
SSM_HEAD_DIM = 64
SSM_GROUPS = 2
SSM_STATE = 128
HEADS_PER_GROUP = SSM_HEADS // SSM_GROUPS
GROUP_WIDTH = SSM_WIDTH // SSM_GROUPS
CONV_WIDTH = 4
CONV_DIM = SSM_WIDTH + 2 * SSM_GROUPS * SSM_STATE
MEM_LEN = 256
MEM_HEADS = 4
MEM_HEAD_DIM = 256
MEM_WIDTH = 1024
MIX_WIDTH = GM_WIDTH + SSM_WIDTH + MEM_WIDTH
EPS = 1e-6

LANES = 128
SUBLANES = 8

DT_PAD = LANES
OFF_U = (0, 0)
OFF_V = (0, GM_WIDTH)
OFF_GATE = (0, 2 * GM_WIDTH)
OFF_Z = (0, 3 * GM_WIDTH)
OFF_XBC = (0, 3 * GM_WIDTH + SSM_WIDTH)
WIDTH_A = 3 * GM_WIDTH + SSM_WIDTH + CONV_DIM
OFF_Q = (1, 0)
OFF_MGATE = (1, MEM_WIDTH)
OFF_DT = (1, 2 * MEM_WIDTH)
WIDTH_B = 2 * MEM_WIDTH + DT_PAD
PROJ_BLOCK = 512

PROMPT_TILE = 256
SAMPLE_BATCH_BLOCK = 4
VMEM_LIMIT = 56 * 1024 * 1024


def _rms(x, g):
    return x * lax.rsqrt(jnp.mean(x * x, axis=-1, keepdims=True) + EPS) * g


def _gelu(x):
    return 0.5 * x * (1.0 + lax.erf(x * math.sqrt(0.5)))


def _silu(x):
    return x * jax.nn.sigmoid(x)


def _softplus(x):
    return jnp.maximum(x, 0.0) + jnp.log1p(jnp.exp(-jnp.abs(x)))


def _layernorm(x, g, b):
    mu = jnp.mean(x, axis=-1, keepdims=True)
    xc = x - mu
    var = jnp.mean(xc * xc, axis=-1, keepdims=True)
    return xc * lax.rsqrt(var + EPS) * g + b


def _dot(a, b):
    return jnp.dot(a, b, preferred_element_type=F32)


def _dot_nt(a, b):
    return lax.dot_general(a, b, (((1,), (1,)), ((), ())), preferred_element_type=F32)


def _dot_tn(a, b):
    return lax.dot_general(a, b, (((0,), (0,)), ((), ())), preferred_element_type=F32)


def _group_rmsnorm(y, g):
    halves = []
    for i in range(SSM_GROUPS):
        yg = y[:, i * GROUP_WIDTH:(i + 1) * GROUP_WIDTH]
        halves.append(yg * lax.rsqrt(jnp.mean(yg * yg, axis=-1, keepdims=True) + EPS))
    return jnp.concatenate(halves, axis=-1) * g


def _memory_kv_kernel(mem_ref, g_ref, wk_ref, wv_ref, k_ref, v_ref, kb_ref, vb_ref):
    m = _rms(mem_ref[0], g_ref[...]).astype(BF16)
    k = _dot(m, wk_ref[...])
    v = _dot(m, wv_ref[...])
    for h in range(MEM_HEADS):
        k_ref[0, 0, :, h, :] = k[:, _head_slice(h)]
        v_ref[0, 0, :, h, :] = v[:, _head_slice(h)]
    kb_ref[0] = k.astype(BF16)
    vb_ref[0] = v.astype(BF16)


def _memory_kv(mem, g, wk, wv):
    b = mem.shape[0]
    blk = pl.BlockSpec((1, MEM_LEN, D_MODEL), lambda i: (i, 0, 0))
    blk5 = pl.BlockSpec((1, 1, MEM_LEN, MEM_HEADS, MEM_HEAD_DIM), lambda i: (0, i, 0, 0, 0))
    const = lambda shape: pl.BlockSpec(shape, lambda i: (0,) * len(shape))
    return pl.pallas_call(
        _memory_kv_kernel,
        grid=(b,),
        in_specs=[blk, const((1, D_MODEL)), const((D_MODEL, MEM_WIDTH)), const((D_MODEL, MEM_WIDTH))],
        out_specs=[blk5, blk5, blk, blk],
        out_shape=[jax.ShapeDtypeStruct((1, b, MEM_LEN, MEM_HEADS, MEM_HEAD_DIM), F32)] * 2
        + [jax.ShapeDtypeStruct((b, MEM_LEN, MEM_WIDTH), BF16)] * 2,
        compiler_params=pltpu.CompilerParams(dimension_semantics=("arbitrary",),
                                             vmem_limit_bytes=VMEM_LIMIT),
        name="memory_kv",
    )(mem, g, wk, wv)


def _head_slice(h):
    return slice(h * MEM_HEAD_DIM, (h + 1) * MEM_HEAD_DIM)


def _attention(q, k_head, v_head):
    qb = (q * (MEM_HEAD_DIM ** -0.5)).astype(BF16)
    outs = []
    for h in range(MEM_HEADS):
        s = _dot_nt(qb[:, _head_slice(h)], k_head(h))
        p = jnp.exp(s - jnp.max(s, axis=-1, keepdims=True))
        denom = jnp.sum(p, axis=-1, keepdims=True)
        outs.append(_dot(p.astype(BF16), v_head(h)) / denom)
    return jnp.concatenate(outs, axis=-1)


def _attention_interleaved(q, k2, v2):
    t = q.shape[0]
    qb = (q * (MEM_HEAD_DIM ** -0.5)).astype(BF16)
    q2 = jnp.concatenate([qb[:, _head_slice(h)] for h in range(MEM_HEADS)], axis=0)
    s = _dot_nt(q2, k2)
    row_head = lax.broadcasted_iota(jnp.int32, s.shape, 0) // t
    col_head = lax.broadcasted_iota(jnp.int32, s.shape, 1) % MEM_HEADS
    s = jnp.where(row_head == col_head, s, -jnp.inf)
    p = jnp.exp(s - jnp.max(s, axis=-1, keepdims=True))
    denom = jnp.sum(p, axis=-1, keepdims=True)
    return _dot(p.astype(BF16), v2) / denom


class _Handoff:
    N = 8

    def __init__(self, refs):
        (self.mix, self.ug, self.v, self.zs, self.xs, self.bm, self.cm, self.dt) = refs


def _prompt_step(xc_ref, xp_ref, kb_ref, vb_ref, win_refs, wout_ref, ng_ref, gmg_ref, gmb_ref, ws_ref, bsf_ref,
                 cw_ref, cb_ref, dtb_ref, alog_ref, dsk_ref, sng_ref, fg_ref, y_ref,
                 hn_sc, xp_sc, y_sc, ht_sc, ssd_sc, out_sc, new, old, tile):
    a_row = -jnp.exp(alog_ref[...])
    row = lax.broadcasted_iota(jnp.int32, (CHUNK, CHUNK), 0)
    col = lax.broadcasted_iota(jnp.int32, (CHUNK, CHUNK), 1)
    causal = row >= col
    tril_b = causal.astype(BF16)
    head_lane = col < SSM_HEADS
    first_half = col < SSM_HEAD_DIM
    acum_sc, acum_t_sc, dt_t_sc, to_end_t_sc, cb_sc, bm_t_sc = (ssd_sc.at[i] for i in range(6))

    def proj(off, width):
        part, c0 = off
        return _dot(hn_sc[...], win_refs[part][:, c0:c0 + width])

    def gmlp_mix(c):
        r = slice(c * CHUNK, (c + 1) * CHUNK)
        for h in range(GM_HEADS):
            hs = slice(h * GM_HEAD_DIM, (h + 1) * GM_HEAD_DIM)
            mixed = _dot(ws_ref[h], old.v[r, hs]) + bsf_ref[:, hs]
            old.mix[r, hs] = (old.ug[r, hs] * mixed).astype(BF16)

    def seq_chunk_start(c):
        r = slice(c * CHUNK, (c + 1) * CHUNK)
        dt = old.dt[r, :]
        adt = jnp.where(head_lane, dt * a_row, 0.0)
        hi = adt.astype(BF16)
        rest = adt - hi.astype(F32)
        mid = rest.astype(BF16)
        lo = (rest - mid.astype(F32)).astype(BF16)
        acum = _dot(tril_b, hi) + _dot(tril_b, mid) + _dot(tril_b, lo)
        acum_t = acum.T
        dt_t = dt.T
        acum_sc[...] = acum
        acum_t_sc[...] = acum_t
        dt_t_sc[...] = dt_t
        to_end_t_sc[...] = jnp.exp(acum_t[:, CHUNK - 1:CHUNK] - acum_t) * dt_t

    def seq_group(c, g):
        r = slice(c * CHUNK, (c + 1) * CHUNK)
        gs = slice(g * SSM_STATE, (g + 1) * SSM_STATE)
        bm_g = old.bm[r, gs]
        cb_sc[...] = _dot_nt(old.cm[r, gs].astype(BF16), bm_g.astype(BF16))
        bm_t_sc[...] = bm_g.T
        for kk in range(0, HEADS_PER_GROUP, 2):
            k0 = g * HEADS_PER_GROUP + kk
            ps = slice(k0 * SSM_HEAD_DIM, (k0 + 2) * SSM_HEAD_DIM)
            cm_g = old.cm[r, gs]
            lhs_parts, b_parts, decs = [], [], []
            for k in (k0, k0 + 1):
                a_col = jnp.broadcast_to(acum_sc[:, k:k + 1], (CHUNK, CHUNK))
                diff = a_col - acum_t_sc[k:k + 1, :]
                decay = jnp.exp(jnp.where(causal, diff, -jnp.inf))
                lhs_parts.append((cb_sc[...] * decay * dt_t_sc[k:k + 1, :]).astype(BF16))
                lhs_parts.append((cm_g * jnp.exp(a_col)).astype(BF16))
                b_parts.append((bm_t_sc[...] * to_end_t_sc[k:k + 1, :]).astype(BF16))
                decs.append(jnp.exp(a_col[CHUNK - 1:CHUNK, :]))
            xs_pair = old.xs[r, ps]
            h_pair = ht_sc[:, ps]
            zero = jnp.zeros_like(xs_pair)
            xs_lo = jnp.where(first_half, xs_pair, zero).astype(BF16)
            xs_hi = jnp.where(first_half, zero, xs_pair).astype(BF16)
            h_lo = jnp.where(first_half, h_pair, zero).astype(BF16)
            h_hi = jnp.where(first_half, zero, h_pair).astype(BF16)
            lhs = jnp.concatenate(lhs_parts, axis=1)
            rhs = jnp.concatenate([xs_lo, h_lo, xs_hi, h_hi], axis=0)
            y_sc[:, ps] = _dot(lhs, rhs)
            upd = _dot(jnp.concatenate(b_parts, axis=1), jnp.concatenate([xs_lo, xs_hi], axis=0))
            dec = jnp.where(first_half[0:1, :], decs[0], decs[1])
            ht_sc[:, ps] = h_pair * dec + upd

    def seq_chunk_end(c):
        r = slice(c * CHUNK, (c + 1) * CHUNK)
        y = (y_sc[...] + dsk_ref[...] * old.xs[r, :]) * old.zs[r, :]
        old.mix[r, GM_WIDTH:GM_WIDTH + SSM_WIDTH] = _group_rmsnorm(y, sng_ref[...]).astype(BF16)

    def out_piece(j, n):
        cs = slice(j * D_MODEL // n, (j + 1) * D_MODEL // n)
        out_sc[:, cs] = xp_ref[0, :, cs] + _dot(old.mix[...], wout_ref[:, cs])

    assert tile // CHUNK == 2, "the emission order below is written for two chunks per tile"

    hn_sc[...] = _rms(xc_ref[0], ng_ref[...]).astype(BF16)

    new.zs[...] = _silu(proj(OFF_Z, SSM_WIDTH))
    xp_sc[SUBLANES:SUBLANES + tile, :] = proj(OFF_XBC, CONV_DIM)
    ext = xp_sc[...].reshape(tile // SUBLANES + 1, SUBLANES, CONV_DIM)
    sub = lax.broadcasted_iota(jnp.int32, (1, SUBLANES, CONV_DIM), 1)
    acc = cb_ref[...] + cw_ref[CONV_WIDTH - 1:CONV_WIDTH, :] * ext[1:]
    for shift in range(1, CONV_WIDTH):
        rot = pltpu.roll(ext, shift, 1)
        shifted = jnp.where(sub >= shift, rot[1:], rot[:-1])
        acc = acc + cw_ref[CONV_WIDTH - 1 - shift:CONV_WIDTH - shift, :] * shifted
    acc = acc.reshape(tile, CONV_DIM)
    xbc = _silu(acc)
    new.xs[...] = xbc[:, :SSM_WIDTH]
    new.bm[...] = xbc[:, SSM_WIDTH:SSM_WIDTH + SSM_GROUPS * SSM_STATE]
    new.cm[...] = xbc[:, SSM_WIDTH + SSM_GROUPS * SSM_STATE:]
    xp_sc[SUBLANES - (CONV_WIDTH - 1):SUBLANES, :] = xp_sc[tile + SUBLANES - (CONV_WIDTH - 1):tile + SUBLANES, :]
    new.dt[...] = _softplus(proj(OFF_DT, DT_PAD) + dtb_ref[...])
    u = _gelu(proj(OFF_U, GM_WIDTH))
    new.ug[...] = u * _silu(proj(OFF_GATE, GM_WIDTH))
    new.v[...] = _layernorm(_gelu(proj(OFF_V, GM_WIDTH)), gmg_ref[...], gmb_ref[...]).astype(BF16)
    gmlp_mix(0)
    gmlp_mix(1)
    seq_chunk_start(0)
    seq_group(0, 0)
    seq_group(0, 1)
    seq_chunk_end(0)
    att = _attention(proj(OFF_Q, MEM_WIDTH), lambda h: kb_ref[0, :, _head_slice(h)], lambda h: vb_ref[0, :, _head_slice(h)])
    new.mix[:, GM_WIDTH + SSM_WIDTH:MIX_WIDTH] = (att * _silu(proj(OFF_MGATE, MEM_WIDTH))).astype(BF16)
    seq_chunk_start(1)
    seq_group(1, 0)
    seq_group(1, 1)
    seq_chunk_end(1)

    n_out = 4
    for j in range(n_out):
        out_piece(j, n_out)
    y_ref[0] = _rms(out_sc[...], fg_ref[...])


def _prompt_kernel(xc_ref, kb_ref, vb_ref, wa_ref, wb_ref, wout_ref, ng_ref, gmg_ref, gmb_ref,
                   ws_ref, bsf_ref, cw_ref, cb_ref, dtb_ref, alog_ref, dsk_ref, sng_ref, fg_ref,
                   y_ref, ssm_ref, conv_ref, hn_sc, xpad_sc, y_sc, ht_sc, ssd_sc, out_sc, *handoff, tile, nt, total):
    t = pl.program_id(0) % nt
    tile_set = _Handoff(handoff)

    @pl.when(t == 0)
    def _():
        xpad_sc[0:SUBLANES, :] = jnp.zeros((SUBLANES, CONV_DIM), F32)
        ht_sc[...] = jnp.zeros_like(ht_sc)

    _prompt_step(xc_ref, xc_ref, kb_ref, vb_ref, (wa_ref, wb_ref), wout_ref, ng_ref, gmg_ref, gmb_ref, ws_ref, bsf_ref,
                 cw_ref, cb_ref, dtb_ref, alog_ref, dsk_ref, sng_ref, fg_ref, y_ref,
                 hn_sc, xpad_sc, y_sc, ht_sc, ssd_sc, out_sc, tile_set, tile_set, tile)

    @pl.when(t == nt - 1)
    def _():
        conv_ref[0] = xpad_sc[SUBLANES - (CONV_WIDTH - 1):SUBLANES, :]
        ssm_ref[0] = ht_sc[...].T


def _prompt_layer(x, kb, vb, w_a, w_b, w_out, ng, gmg, gmb, ws, bsf, cw, cb, dtb, alog, dsk, sng, fg):
    b, seq, _ = x.shape
    tile = PROMPT_TILE
    nt = seq // tile
    total = b * nt
    assert seq % tile == 0 and tile % CHUNK == 0

    def const(shape):
        return pl.BlockSpec(shape, lambda s: (0,) * len(shape), pipeline_mode=pl.Buffered(1))

    cur = lambda s: s
    prev = lambda s: s
    in_specs = [
        pl.BlockSpec((1, tile, D_MODEL), lambda s: (cur(s) // nt, cur(s) % nt, 0)),
        pl.BlockSpec((1, MEM_LEN, MEM_WIDTH), lambda s: (cur(s) // nt, 0, 0)),
        pl.BlockSpec((1, MEM_LEN, MEM_WIDTH), lambda s: (cur(s) // nt, 0, 0)),
        const((D_MODEL, WIDTH_A)),
        const((D_MODEL, WIDTH_B)),
        const((MIX_WIDTH, D_MODEL)),
        const((1, D_MODEL)), const((1, GM_WIDTH)), const((1, GM_WIDTH)),
        const((GM_HEADS, CHUNK, CHUNK)), const((CHUNK, GM_WIDTH)),
        const((CONV_WIDTH, CONV_DIM)), const((1, CONV_DIM)),
        const((1, DT_PAD)), const((1, DT_PAD)), const((1, SSM_WIDTH)), const((1, SSM_WIDTH)),
        const((1, D_MODEL)),
    ]
    out_specs = [
        pl.BlockSpec((1, tile, D_MODEL), lambda s: (prev(s) // nt, prev(s) % nt, 0)),
        pl.BlockSpec((1, SSM_WIDTH, SSM_STATE), lambda s: (prev(s) // nt, 0, 0)),
        pl.BlockSpec((1, CONV_WIDTH - 1, CONV_DIM), lambda s: (cur(s) // nt, 0, 0)),
    ]
    out_shape = [
        jax.ShapeDtypeStruct((b, seq, D_MODEL), F32),
        jax.ShapeDtypeStruct((b, SSM_WIDTH, SSM_STATE), F32),
        jax.ShapeDtypeStruct((b, CONV_WIDTH - 1, CONV_DIM), F32),
    ]
    handoff = [
        pltpu.VMEM((tile, MIX_WIDTH), BF16),
        pltpu.VMEM((tile, GM_WIDTH), F32),
        pltpu.VMEM((tile, GM_WIDTH), BF16),
        pltpu.VMEM((tile, SSM_WIDTH), F32),
        pltpu.VMEM((tile, SSM_WIDTH), F32),
        pltpu.VMEM((tile, SSM_GROUPS * SSM_STATE), F32),
        pltpu.VMEM((tile, SSM_GROUPS * SSM_STATE), F32),
        pltpu.VMEM((tile, DT_PAD), F32),
    ]
    assert len(handoff) == _Handoff.N
    scratch = [
        pltpu.VMEM((tile, D_MODEL), BF16),
        pltpu.VMEM((tile + SUBLANES, CONV_DIM), F32),
        pltpu.VMEM((CHUNK, SSM_WIDTH), F32),
        pltpu.VMEM((SSM_STATE, SSM_WIDTH), F32),
        pltpu.VMEM((6, CHUNK, CHUNK), F32),
        pltpu.VMEM((tile, D_MODEL), F32),
    ] + handoff
    return pl.pallas_call(
        functools.partial(_prompt_kernel, tile=tile, nt=nt, total=total),
        grid=(total,),
        in_specs=in_specs,
        out_specs=out_specs,
        out_shape=out_shape,
        scratch_shapes=scratch,
        compiler_params=pltpu.CompilerParams(dimension_semantics=("arbitrary",),
                                             vmem_limit_bytes=VMEM_LIMIT),
        name="prompt_layer",
    )(x, kb, vb, w_a, w_b, w_out, ng, gmg, gmb, ws, bsf, cw, cb, dtb, alog, dsk, sng, fg)


def _sample_proj_kernel(x_ref, g_ref, w_ref, o_ref):
    hn = _rms(x_ref[...], g_ref[...]).astype(BF16)
    o_ref[...] = _dot(hn, w_ref[...])


def _sample_proj(x, g, w, block, name):
    m, n = x.shape[0], w.shape[1]
    assert n % block == 0
    return pl.pallas_call(
        _sample_proj_kernel,
        grid=(n // block,),
        in_specs=[pl.BlockSpec((m, D_MODEL), lambda j: (0, 0)),
                  pl.BlockSpec((1, D_MODEL), lambda j: (0, 0)),
                  pl.BlockSpec((D_MODEL, block), lambda j: (0, j))],
        out_specs=pl.BlockSpec((m, block), lambda j: (0, j)),
        out_shape=jax.ShapeDtypeStruct((m, n), F32),
        compiler_params=pltpu.CompilerParams(dimension_semantics=("arbitrary",),
                                             vmem_limit_bytes=VMEM_LIMIT),
        name=name,
    )(x, g, w)


def _sample_mix_kernel(pa_ref, pb_ref, cst_ref, ssm_ref, k_ref, v_ref, gmg_ref, gmb_ref, gcoef_ref, gbias_ref,
                       cw_ref, cb_ref, dtb_ref, alog_ref, dsk_ref, sng_ref, expand_ref,
                       mix_ref, ssm_out_ref, conv_out_ref, gv_ref,
                       xp_sc, xbc_sc, att_sc, yoff_sc, dec_sc, *, bblk, seq):
    rows = bblk * seq

    def col(off, width):
        part, c0 = off
        return (pa_ref, pb_ref)[part][:, c0:c0 + width]

    tpos = lax.broadcasted_iota(jnp.int32, (rows, 1), 0) % seq

    def back(a, j):
        return a if j == 0 else pltpu.roll(a, j, 0)

    u = _gelu(col(OFF_U, GM_WIDTH))
    v = _layernorm(_gelu(col(OFF_V, GM_WIDTH)), gmg_ref[...], gmb_ref[...])
    gv_ref[...] = v
    mixed = gbias_ref[...]
    for j in range(seq):
        mixed = mixed + gcoef_ref[j] * back(v, j)
    mix_ref[:, 0:GM_WIDTH] = (u * mixed * _silu(col(OFF_GATE, GM_WIDTH))).astype(BF16)

    xbc_raw = col(OFF_XBC, CONV_DIM)
    for b in range(bblk):
        xp_sc[b, 0:CONV_WIDTH - 1, :] = cst_ref[b]
        xp_sc[b, CONV_WIDTH - 1:CONV_WIDTH - 1 + seq, :] = xbc_raw[b * seq:(b + 1) * seq, :]
    for b in range(bblk):
        acc = jnp.broadcast_to(cb_ref[...], (seq, CONV_DIM))
        for j in range(CONV_WIDTH):
            acc = acc + cw_ref[j:j + 1, :] * xp_sc[b, j:j + seq, :]
        xbc_sc[b * seq:(b + 1) * seq, :] = _silu(acc)
        conv_out_ref[b] = xp_sc[b, seq:seq + CONV_WIDTH - 1, :]
    xs = xbc_sc[:, 0:SSM_WIDTH]
    bm = xbc_sc[:, SSM_WIDTH:SSM_WIDTH + SSM_GROUPS * SSM_STATE]
    cm = xbc_sc[:, SSM_WIDTH + SSM_GROUPS * SSM_STATE:CONV_DIM]

    lane = lax.broadcasted_iota(jnp.int32, (rows, DT_PAD), 1)
    dt = _softplus(col(OFF_DT, DT_PAD) + dtb_ref[...])
    adt = jnp.where(lane < SSM_HEADS, dt * (-jnp.exp(alog_ref[...])), 0.0)
    acum = adt
    for j in range(1, seq):
        acum = acum + jnp.where(tpos >= j, back(adt, j), 0.0)
    a_last = jnp.zeros_like(acum)
    for j in range(seq):
        a_last = a_last + jnp.where(tpos == seq - 1 - j, acum if j == 0 else pltpu.roll(acum, rows - j, 0), 0.0)
    coefs = []
    for j in range(seq):
        cbj = []
        for g in range(SSM_GROUPS):
            gs = slice(g * SSM_STATE, (g + 1) * SSM_STATE)
            cbj.append(jnp.sum(cm[:, gs] * back(bm[:, gs], j), axis=-1, keepdims=True))
        cb_l = jnp.where(lane < HEADS_PER_GROUP, cbj[0], cbj[1])
        valid = tpos >= j
        decay = jnp.exp(jnp.where(valid, acum - back(acum, j), 0.0))
        coefs.append(jnp.where(valid, cb_l * decay * back(dt, j), 0.0))
    coefs.append(jnp.exp(acum))
    coefs.append(dt * jnp.exp(a_last - acum))
    stack = jnp.concatenate(coefs, axis=0)
    hi = stack.astype(BF16)
    lo = (stack - hi.astype(F32)).astype(BF16)
    wide = _dot(hi, expand_ref[...]) + _dot(lo, expand_ref[...])
    y = dsk_ref[...] * xs
    for j in range(seq):
        y = y + wide[j * rows:(j + 1) * rows, :] * back(xs, j)
    e_wide = wide[seq * rows:(seq + 1) * rows, :]
    wx = (xs * wide[(seq + 1) * rows:(seq + 2) * rows, :]).astype(BF16)
    dec_rows = jnp.exp(a_last)

    q = col(OFF_Q, MEM_WIDTH)
    cmb = cm.astype(BF16)
    bmb = bm.astype(BF16)
    for b in range(bblk):
        rs = slice(b * seq, (b + 1) * seq)
        att = _attention_interleaved(
            q[rs, :],
            k_ref[0, b].reshape(MEM_LEN * MEM_HEADS, MEM_HEAD_DIM).astype(BF16),
            v_ref[0, b].reshape(MEM_LEN * MEM_HEADS, MEM_HEAD_DIM).astype(BF16))
        for h in range(MEM_HEADS):
            att_sc[rs, _head_slice(h)] = att[h * seq:(h + 1) * seq, :]
        h0 = ssm_ref[b]
        h0b = h0.astype(BF16)
        dec_sc[...] = jnp.broadcast_to(dec_rows[b * seq + seq - 1:b * seq + seq, :], (LANES, DT_PAD)).T
        for g in range(SSM_GROUPS):
            gs = slice(g * SSM_STATE, (g + 1) * SSM_STATE)
            ws_ = slice(g * GROUP_WIDTH, (g + 1) * GROUP_WIDTH)
            yoff_sc[rs, ws_] = _dot_nt(cmb[rs, gs], h0b[ws_, :])
            upd = _dot_tn(wx[rs, ws_], bmb[rs, gs])
            for kk in range(HEADS_PER_GROUP):
                k = g * HEADS_PER_GROUP + kk
                hs = slice(k * SSM_HEAD_DIM, (k + 1) * SSM_HEAD_DIM)
                ssm_out_ref[b, hs, :] = (h0[hs, :] * dec_sc[k:k + 1, :]
                                         + upd[kk * SSM_HEAD_DIM:(kk + 1) * SSM_HEAD_DIM, :])

    y = (y + yoff_sc[...] * e_wide) * _silu(col(OFF_Z, SSM_WIDTH))
    mix_ref[:, GM_WIDTH:GM_WIDTH + SSM_WIDTH] = _group_rmsnorm(y, sng_ref[...]).astype(BF16)
    mix_ref[:, GM_WIDTH + SSM_WIDTH:MIX_WIDTH] = (att_sc[...] * _silu(col(OFF_MGATE, MEM_WIDTH))).astype(BF16)


def _sample_mix(proj_a, proj_b, conv_state, ssm_state, mem_k, mem_v, gmg, gmb, gcoef, gbias, cw, cb, dtb, alog,
                dsk, sng, expand, seq):
    nb = conv_state.shape[0]
    bblk = SAMPLE_BATCH_BLOCK
    rows = bblk * seq
    assert nb % bblk == 0 and rows % (2 * SUBLANES) == 0

    def const(shape):
        return pl.BlockSpec(shape, lambda i: (0,) * len(shape))

    in_specs = [
        pl.BlockSpec((rows, WIDTH_A), lambda i: (i, 0)),
        pl.BlockSpec((rows, WIDTH_B), lambda i: (i, 0)),
        pl.BlockSpec((bblk, CONV_WIDTH - 1, CONV_DIM), lambda i: (i, 0, 0)),
        pl.BlockSpec((bblk, SSM_WIDTH, SSM_STATE), lambda i: (i, 0, 0)),
        pl.BlockSpec((1, bblk, MEM_LEN, MEM_HEADS, MEM_HEAD_DIM), lambda i: (0, i, 0, 0, 0)),
        pl.BlockSpec((1, bblk, MEM_LEN, MEM_HEADS, MEM_HEAD_DIM), lambda i: (0, i, 0, 0, 0)),
        const((1, GM_WIDTH)), const((1, GM_WIDTH)),
        const((seq, rows, GM_WIDTH)), const((rows, GM_WIDTH)),
        const((CONV_WIDTH, CONV_DIM)), const((1, CONV_DIM)),
        const((1, DT_PAD)), const((1, DT_PAD)), const((1, SSM_WIDTH)), const((1, SSM_WIDTH)),
        const((DT_PAD, SSM_WIDTH)),
    ]
    out_specs = [
        pl.BlockSpec((rows, MIX_WIDTH), lambda i: (i, 0)),
        pl.BlockSpec((bblk, SSM_WIDTH, SSM_STATE), lambda i: (i, 0, 0)),
        pl.BlockSpec((bblk, CONV_WIDTH - 1, CONV_DIM), lambda i: (i, 0, 0)),
        pl.BlockSpec((rows, GM_WIDTH), lambda i: (i, 0)),
    ]
    out_shape = [
        jax.ShapeDtypeStruct((nb * seq, MIX_WIDTH), BF16),
        jax.ShapeDtypeStruct((nb, SSM_WIDTH, SSM_STATE), F32),
        jax.ShapeDtypeStruct((nb, CONV_WIDTH - 1, CONV_DIM), F32),
        jax.ShapeDtypeStruct((nb * seq, GM_WIDTH), F32),
    ]
    scratch = [
        pltpu.VMEM((bblk, SUBLANES, CONV_DIM), F32),
        pltpu.VMEM((rows, CONV_DIM), F32),
        pltpu.VMEM((rows, MEM_WIDTH), F32),
        pltpu.VMEM((rows, SSM_WIDTH), F32),
        pltpu.VMEM((DT_PAD, LANES), F32),
    ]
    return pl.pallas_call(
        functools.partial(_sample_mix_kernel, bblk=bblk, seq=seq),
        grid=(nb // bblk,),
        in_specs=in_specs,
        out_specs=out_specs,
        out_shape=out_shape,
        scratch_shapes=scratch,
        compiler_params=pltpu.CompilerParams(dimension_semantics=("arbitrary",),
                                             vmem_limit_bytes=VMEM_LIMIT),
        name="sample_mix",
    )(proj_a, proj_b, conv_state, ssm_state, mem_k, mem_v, gmg, gmb, gcoef, gbias, cw, cb, dtb, alog, dsk, sng, expand)


def _sample_out_kernel(mix_ref, x_ref, w_ref, g_ref, o_ref):
    out = x_ref[...] + _dot(mix_ref[...], w_ref[...])
    o_ref[...] = _rms(out, g_ref[...])


def _sample_out(mix, x, w_out, g):
    m = x.shape[0]
    full = lambda shape: pl.BlockSpec(shape, lambda i: (0,) * len(shape))
    return pl.pallas_call(
        _sample_out_kernel,
        grid=(1,),
        in_specs=[full((m, MIX_WIDTH)), full((m, D_MODEL)), full((MIX_WIDTH, D_MODEL)), full((1, D_MODEL))],
        out_specs=full((m, D_MODEL)),
        out_shape=jax.ShapeDtypeStruct((m, D_MODEL), F32),
        compiler_params=pltpu.CompilerParams(dimension_semantics=("arbitrary",),
                                             vmem_limit_bytes=VMEM_LIMIT),
        name="sample_out",
    )(mix, x, w_out, g)


def _split_w_in_kernel(w_ref, wa_ref, wb_ref):
    rows = w_ref.shape[0]
    wa_ref[...] = w_ref[:, :WIDTH_A].astype(BF16)
    wb_ref[:, 0:2 * MEM_WIDTH] = w_ref[:, WIDTH_A + SSM_HEADS:].astype(BF16)
    lane = lax.broadcasted_iota(jnp.int32, (rows, DT_PAD), 1)
    dt_tile = w_ref[:, WIDTH_A:WIDTH_A + DT_PAD]
    wb_ref[:, 2 * MEM_WIDTH:WIDTH_B] = jnp.where(lane < SSM_HEADS, dt_tile, 0.0).astype(BF16)


def _split_w_in(w):
    k, n = w.shape
    assert n == WIDTH_A + SSM_HEADS + 2 * MEM_WIDTH
    rows = CHUNK
    return pl.pallas_call(
        _split_w_in_kernel,
        grid=(k // rows,),
        in_specs=[pl.BlockSpec((rows, n), lambda i: (i, 0))],
        out_specs=[pl.BlockSpec((rows, WIDTH_A), lambda i: (i, 0)), pl.BlockSpec((rows, WIDTH_B), lambda i: (i, 0))],
        out_shape=[jax.ShapeDtypeStruct((k, WIDTH_A), BF16), jax.ShapeDtypeStruct((k, WIDTH_B), BF16)],
        compiler_params=pltpu.CompilerParams(dimension_semantics=("arbitrary",),
                                             vmem_limit_bytes=VMEM_LIMIT),
        name="split_w_in",
    )(w)


def _pad_heads(a):
    return jnp.pad(a.astype(F32), (0, DT_PAD - SSM_HEADS)).reshape(1, DT_PAD)


def kernel(x_prompt, x_sample, mem_prompt, state_ssm, state_conv, cache_mem_k, cache_mem_v, norm_g, w_in,
           gm_norm_g, gm_norm_b, gm_w_spatial, gm_b_spatial, conv_w, conv_b, dt_bias, a_log, d_skip,
           ssm_norm_g, mem_norm_g, w_mem_k, w_mem_v, w_out, final_norm_g):
    assert norm_g.shape[0] == 1, "single layer"
    bp, seq_p, _ = x_prompt.shape
    bs, seq_s, _ = x_sample.shape
    row = lambda a: a.reshape(1, -1).astype(F32)

    w_a, w_b = _split_w_in(w_in[0])
    w_out_b = w_out[0].astype(BF16)
    ng, gmg, gmb = row(norm_g[0]), row(gm_norm_g[0]), row(gm_norm_b[0])
    cw, cb = conv_w[0].astype(F32), row(conv_b[0])
    dtb, alog = _pad_heads(dt_bias[0]), _pad_heads(a_log[0])
    dsk = row(jnp.repeat(d_skip[0], SSM_HEAD_DIM))
    sng, fg = row(ssm_norm_g[0]), row(final_norm_g)
    w_sp = gm_w_spatial[0]
    tril_p = jnp.tril(jnp.ones((CHUNK, CHUNK), bool))
    ws_p = jnp.where(tril_p, w_sp, 0).astype(BF16)
    bsf_p = jnp.repeat(gm_b_spatial[0].T, GM_HEAD_DIM, axis=1).astype(F32)

    mk, mv, mkb, mvb = _memory_kv(mem_prompt, row(mem_norm_g[0]), w_mem_k[0].astype(BF16),
                                  w_mem_v[0].astype(BF16))
    y_p, ssm_p, conv_p = _prompt_layer(x_prompt, mkb, mvb, w_a, w_b, w_out_b, ng, gmg, gmb, ws_p, bsf_p,
                                       cw, cb, dtb, alog, dsk, sng, fg)

    rows = SAMPLE_BATCH_BLOCK * seq_s
    tpos = jnp.arange(rows) % seq_s
    gcoef = jnp.stack([
        jnp.where((tpos >= j)[:, None],
                  jnp.repeat(w_sp[:, tpos, jnp.maximum(tpos - j, 0)].T, GM_HEAD_DIM, axis=1), 0.0)
        for j in range(seq_s)]).astype(F32)
    gbias = jnp.repeat(gm_b_spatial[0][:, tpos].T, GM_HEAD_DIM, axis=1).astype(F32)
    expand = (jnp.arange(DT_PAD)[:, None] == (jnp.arange(SSM_WIDTH) // SSM_HEAD_DIM)[None, :]).astype(BF16)

    xs2 = x_sample.reshape(bs * seq_s, D_MODEL)
    proj_a = _sample_proj(xs2, ng, w_a, PROJ_BLOCK, "sample_proj_a")
    proj_b = _sample_proj(xs2, ng, w_b, WIDTH_B, "sample_proj_b")
    mix_s, ssm_s, conv_s, gv_s = _sample_mix(
        proj_a, proj_b, state_conv[0], state_ssm[0].reshape(bs, SSM_WIDTH, SSM_STATE),
        cache_mem_k, cache_mem_v,
        gmg, gmb, gcoef, gbias, cw, cb, dtb, alog, dsk, sng, expand, seq_s)
    y_s = _sample_out(mix_s, xs2, w_out_b, fg)

    return (y_p,
            y_s.reshape(bs, seq_s, D_MODEL),
            ssm_p.reshape(1, bp, SSM_HEADS, SSM_HEAD_DIM, SSM_STATE),
            conv_p[None],
            mk,
            mv,
            ssm_s.reshape(1, bs, SSM_HEADS, SSM_HEAD_DIM, SSM_STATE),
            conv_s[None],
            gv_s.reshape(1, bs, seq_s, GM_WIDTH))
```

```python
import functools
import math

import jax
import jax.numpy as jnp
from jax import lax
from jax.experimental import pallas as pl
from jax.experimental.pallas import tpu as pltpu

F32 = jnp.float32
BF16 = jnp.bfloat16

D_MODEL = 1024
GM_WIDTH = 1024
GM_HEADS = 8
GM_HEAD_DIM = 128
CHUNK = 128
SSM_WIDTH = 1024
SSM_HEADS = 16
SSM_HEAD_DIM = 64
SSM_GROUPS = 2
SSM_STATE = 128
HEADS_PER_GROUP = SSM_HEADS // SSM_GROUPS
GROUP_WIDTH = SSM_WIDTH // SSM_GROUPS
CONV_WIDTH = 4
CONV_DIM = SSM_WIDTH + 2 * SSM_GROUPS * SSM_STATE
MEM_LEN = 256
MEM_HEADS = 4
MEM_HEAD_DIM = 256
MEM_WIDTH = 1024
MIX_WIDTH = GM_WIDTH + SSM_WIDTH + MEM_WIDTH
EPS = 1e-6

LANES = 128
SUBLANES = 8

DT_PAD = LANES
OFF_U = 0
OFF_V = OFF_U + GM_WIDTH
OFF_GATE = OFF_V + GM_WIDTH
OFF_Z = OFF_GATE + GM_WIDTH
OFF_XBC = OFF_Z + SSM_WIDTH
OFF_DT = OFF_XBC + CONV_DIM
OFF_Q = OFF_DT + SSM_HEADS
OFF_MGATE = OFF_Q + MEM_WIDTH
IN_WIDTH = OFF_MGATE + MEM_WIDTH
PROJ_BLOCK = 1024

PROMPT_TILE = 256
SAMPLE_BATCH_BLOCK = 4
VMEM_LIMIT = 56 * 1024 * 1024


def _rms(x, g):
    return x * lax.rsqrt(jnp.mean(x * x, axis=-1, keepdims=True) + EPS) * g


def _gelu(x):
    return 0.5 * x * (1.0 + lax.erf(x * math.sqrt(0.5)))


def _silu(x):
    return x * jax.nn.sigmoid(x)


def _softplus(x):
    return jnp.maximum(x, 0.0) + jnp.log1p(jnp.exp(-jnp.abs(x)))


def _layernorm(x, g, b):
    mu = jnp.mean(x, axis=-1, keepdims=True)
    xc = x - mu
    var = jnp.mean(xc * xc, axis=-1, keepdims=True)
    return xc * lax.rsqrt(var + EPS) * g + b


def _dot(a, b):
    return jnp.dot(a, b, preferred_element_type=F32)


def _dot_nt(a, b):
    return lax.dot_general(a, b, (((1,), (1,)), ((), ())), preferred_element_type=F32)


def _dot_tn(a, b):
    return lax.dot_general(a, b, (((0,), (0,)), ((), ())), preferred_element_type=F32)


def _group_rmsnorm(y, g):
    halves = []
    for i in range(SSM_GROUPS):
        yg = y[:, i * GROUP_WIDTH:(i + 1) * GROUP_WIDTH]
        halves.append(yg * lax.rsqrt(jnp.mean(yg * yg, axis=-1, keepdims=True) + EPS))
    return jnp.concatenate(halves, axis=-1) * g


def _memory_kv_kernel(mem_ref, g_ref, wk_ref, wv_ref, k_ref, v_ref, kb_ref, vb_ref):
    m = _rms(mem_ref[0], g_ref[...]).astype(BF16)
    k = _dot(m, wk_ref[...])
    v = _dot(m, wv_ref[...])
    for h in range(MEM_HEADS):
        k_ref[0, 0, :, h, :] = k[:, _head_slice(h)]
        v_ref[0, 0, :, h, :] = v[:, _head_slice(h)]
    kb_ref[0] = k.astype(BF16)
    vb_ref[0] = v.astype(BF16)


def _memory_kv(mem, g, wk, wv):
    b = mem.shape[0]
    blk = pl.BlockSpec((1, MEM_LEN, D_MODEL), lambda i: (i, 0, 0))
    blk5 = pl.BlockSpec((1, 1, MEM_LEN, MEM_HEADS, MEM_HEAD_DIM), lambda i: (0, i, 0, 0, 0))
    const = lambda shape: pl.BlockSpec(shape, lambda i: (0,) * len(shape))
    return pl.pallas_call(
        _memory_kv_kernel,
        grid=(b,),
        in_specs=[blk, const((1, D_MODEL)), const((D_MODEL, MEM_WIDTH)), const((D_MODEL, MEM_WIDTH))],
        out_specs=[blk5, blk5, blk, blk],
        out_shape=[jax.ShapeDtypeStruct((1, b, MEM_LEN, MEM_HEADS, MEM_HEAD_DIM), F32)] * 2
        + [jax.ShapeDtypeStruct((b, MEM_LEN, MEM_WIDTH), BF16)] * 2,
        compiler_params=pltpu.CompilerParams(dimension_semantics=("arbitrary",),
                                             vmem_limit_bytes=VMEM_LIMIT),
        name="memory_kv",
    )(mem, g, wk, wv)


def _head_slice(h):
    return slice(h * MEM_HEAD_DIM, (h + 1) * MEM_HEAD_DIM)


def _attention(q, k_head, v_head):
    qb = (q * (MEM_HEAD_DIM ** -0.5)).astype(BF16)
    outs = []
    for h in range(MEM_HEADS):
        s = _dot_nt(qb[:, _head_slice(h)], k_head(h))
        p = jnp.exp(s - jnp.max(s, axis=-1, keepdims=True))
        denom = jnp.sum(p, axis=-1, keepdims=True)
        outs.append(_dot(p.astype(BF16), v_head(h)) / denom)
    return jnp.concatenate(outs, axis=-1)


def _attention_interleaved(q, k2, v2):
    t = q.shape[0]
    qb = (q * (MEM_HEAD_DIM ** -0.5)).astype(BF16)
    q2 = jnp.concatenate([qb[:, _head_slice(h)] for h in range(MEM_HEADS)], axis=0)
    s = _dot_nt(q2, k2)
    row_head = lax.broadcasted_iota(jnp.int32, s.shape, 0) // t
    col_head = lax.broadcasted_iota(jnp.int32, s.shape, 1) % MEM_HEADS
    s = jnp.where(row_head == col_head, s, -jnp.inf)
    p = jnp.exp(s - jnp.max(s, axis=-1, keepdims=True))
    denom = jnp.sum(p, axis=-1, keepdims=True)
    return _dot(p.astype(BF16), v2) / denom


class _Handoff:
    N = 8

    def __init__(self, refs):
        (self.mix, self.ug, self.v, self.zs, self.xs, self.bm, self.cm, self.dt) = refs


def _prompt_step(xc_ref, xp_ref, kb_ref, vb_ref, win_ref, wout_ref, ng_ref, gmg_ref, gmb_ref, ws_ref, bsf_ref,
                 cw_ref, cb_ref, dtb_ref, alog_ref, dsk_ref, sng_ref, fg_ref, y_ref,
                 hn_sc, xp_sc, y_sc, ht_sc, ssd_sc, out_sc, new, old, tile):
    a_row = -jnp.exp(alog_ref[...])
    row = lax.broadcasted_iota(jnp.int32, (CHUNK, CHUNK), 0)
    col = lax.broadcasted_iota(jnp.int32, (CHUNK, CHUNK), 1)
    causal = row >= col
    tril_b = causal.astype(BF16)
    head_lane = col < SSM_HEADS
    first_half = col < SSM_HEAD_DIM
    acum_sc, acum_t_sc, dt_t_sc, to_end_t_sc, cb_sc, bm_t_sc = (ssd_sc.at[i] for i in range(6))

    def proj(off, width):
        return _dot_nt(hn_sc[...], win_ref[off:off + width, :])

    def gmlp_mix(c):
        r = slice(c * CHUNK, (c + 1) * CHUNK)
        for h in range(GM_HEADS):
            hs = slice(h * GM_HEAD_DIM, (h + 1) * GM_HEAD_DIM)
            mixed = _dot(ws_ref[h], old.v[r, hs]) + bsf_ref[:, hs]
            old.mix[r, hs] = (old.ug[r, hs] * mixed).astype(BF16)

    def seq_chunk_start(c):
        r = slice(c * CHUNK, (c + 1) * CHUNK)
        dt = old.dt[r, :]
        adt = jnp.where(head_lane, dt * a_row, 0.0)
        hi = adt.astype(BF16)
        rest = adt - hi.astype(F32)
        mid = rest.astype(BF16)
        lo = (rest - mid.astype(F32)).astype(BF16)
        acum = _dot(tril_b, hi) + _dot(tril_b, mid) + _dot(tril_b, lo)
        acum_t = acum.T
        dt_t = dt.T
        acum_sc[...] = acum
        acum_t_sc[...] = acum_t
        dt_t_sc[...] = dt_t
        to_end_t_sc[...] = jnp.exp(acum_t[:, CHUNK - 1:CHUNK] - acum_t) * dt_t

    def seq_group(c, g):
        r = slice(c * CHUNK, (c + 1) * CHUNK)
        gs = slice(g * SSM_STATE, (g + 1) * SSM_STATE)
        bm_g = old.bm[r, gs]
        cb_sc[...] = _dot_nt(old.cm[r, gs].astype(BF16), bm_g.astype(BF16))
        bm_t_sc[...] = bm_g.T
        for kk in range(0, HEADS_PER_GROUP, 2):
            k0 = g * HEADS_PER_GROUP + kk
            ps = slice(k0 * SSM_HEAD_DIM, (k0 + 2) * SSM_HEAD_DIM)
            cm_g = old.cm[r, gs]
            lhs_parts, b_parts, decs = [], [], []
            for k in (k0, k0 + 1):
                a_col = jnp.broadcast_to(acum_sc[:, k:k + 1], (CHUNK, CHUNK))
                diff = a_col - acum_t_sc[k:k + 1, :]
                decay = jnp.exp(jnp.where(causal, diff, -jnp.inf))
                lhs_parts.append((cb_sc[...] * decay * dt_t_sc[k:k + 1, :]).astype(BF16))
                lhs_parts.append((cm_g * jnp.exp(a_col)).astype(BF16))
                b_parts.append((bm_t_sc[...] * to_end_t_sc[k:k + 1, :]).astype(BF16))
                decs.append(jnp.exp(a_col[CHUNK - 1:CHUNK, :]))
            xs_pair = old.xs[r, ps]
            h_pair = ht_sc[:, ps]
            zero = jnp.zeros_like(xs_pair)
            xs_lo = jnp.where(first_half, xs_pair, zero).astype(BF16)
            xs_hi = jnp.where(first_half, zero, xs_pair).astype(BF16)
            h_lo = jnp.where(first_half, h_pair, zero).astype(BF16)
            h_hi = jnp.where(first_half, zero, h_pair).astype(BF16)
            lhs = jnp.concatenate(lhs_parts, axis=1)
            rhs = jnp.concatenate([xs_lo, h_lo, xs_hi, h_hi], axis=0)
            y_sc[:, ps] = _dot(lhs, rhs)
            upd = _dot(jnp.concatenate(b_parts, axis=1), jnp.concatenate([xs_lo, xs_hi], axis=0))
            dec = jnp.where(first_half[0:1, :], decs[0], decs[1])
            ht_sc[:, ps] = h_pair * dec + upd

    def seq_chunk_end(c):
        r = slice(c * CHUNK, (c + 1) * CHUNK)
        y = (y_sc[...] + dsk_ref[...] * old.xs[r, :]) * old.zs[r, :]
        old.mix[r, GM_WIDTH:GM_WIDTH + SSM_WIDTH] = _group_rmsnorm(y, sng_ref[...]).astype(BF16)

    def out_piece(j, n):
        cs = slice(j * D_MODEL // n, (j + 1) * D_MODEL // n)
        out_sc[:, cs] = xp_ref[0, :, cs] + _dot(old.mix[...], wout_ref[:, cs])

    assert tile // CHUNK == 2, "the emission order below is written for two chunks per tile"

    hn_sc[...] = _rms(xc_ref[0], ng_ref[...]).astype(BF16)

    new.zs[...] = _silu(proj(OFF_Z, SSM_WIDTH))
    xp_sc[SUBLANES:SUBLANES + tile, :] = proj(OFF_XBC, CONV_DIM)
    ext = xp_sc[...].reshape(tile // SUBLANES + 1, SUBLANES, CONV_DIM)
    sub = lax.broadcasted_iota(jnp.int32, (1, SUBLANES, CONV_DIM), 1)
    acc = cb_ref[...] + cw_ref[CONV_WIDTH - 1:CONV_WIDTH, :] * ext[1:]
    for shift in range(1, CONV_WIDTH):
        rot = pltpu.roll(ext, shift, 1)
        shifted = jnp.where(sub >= shift, rot[1:], rot[:-1])
        acc = acc + cw_ref[CONV_WIDTH - 1 - shift:CONV_WIDTH - shift, :] * shifted
    acc = acc.reshape(tile, CONV_DIM)
    xbc = _silu(acc)
    new.xs[...] = xbc[:, :SSM_WIDTH]
    new.bm[...] = xbc[:, SSM_WIDTH:SSM_WIDTH + SSM_GROUPS * SSM_STATE]
    new.cm[...] = xbc[:, SSM_WIDTH + SSM_GROUPS * SSM_STATE:]
    xp_sc[SUBLANES - (CONV_WIDTH - 1):SUBLANES, :] = xp_sc[tile + SUBLANES - (CONV_WIDTH - 1):tile + SUBLANES, :]
    new.dt[...] = _softplus(proj(OFF_DT, DT_PAD) + dtb_ref[...])
    u = _gelu(proj(OFF_U, GM_WIDTH))
    new.ug[...] = u * _silu(proj(OFF_GATE, GM_WIDTH))
    new.v[...] = _layernorm(_gelu(proj(OFF_V, GM_WIDTH)), gmg_ref[...], gmb_ref[...]).astype(BF16)
    gmlp_mix(0)
    gmlp_mix(1)
    seq_chunk_start(0)
    seq_group(0, 0)
    seq_group(0, 1)
    seq_chunk_end(0)
    att = _attention(proj(OFF_Q, MEM_WIDTH), lambda h: kb_ref[0, :, _head_slice(h)], lambda h: vb_ref[0, :, _head_slice(h)])
    new.mix[:, GM_WIDTH + SSM_WIDTH:MIX_WIDTH] = (att * _silu(proj(OFF_MGATE, MEM_WIDTH))).astype(BF16)
    seq_chunk_start(1)
    seq_group(1, 0)
    seq_group(1, 1)
    seq_chunk_end(1)

    n_out = 4
    for j in range(n_out):
        out_piece(j, n_out)
    y_ref[0] = _rms(out_sc[...], fg_ref[...])


def _prompt_kernel(xc_ref, kb_ref, vb_ref, win_ref, wout_ref, ng_ref, gmg_ref, gmb_ref,
                   ws_ref, bsf_ref, cw_ref, cb_ref, dtb_ref, alog_ref, dsk_ref, sng_ref, fg_ref,
                   y_ref, ssm_ref, conv_ref, hn_sc, xpad_sc, y_sc, ht_sc, ssd_sc, out_sc, *handoff, tile, nt, total):
    t = pl.program_id(0) % nt
    tile_set = _Handoff(handoff)

    @pl.when(t == 0)
    def _():
        xpad_sc[0:SUBLANES, :] = jnp.zeros((SUBLANES, CONV_DIM), F32)
        ht_sc[...] = jnp.zeros_like(ht_sc)

    _prompt_step(xc_ref, xc_ref, kb_ref, vb_ref, win_ref, wout_ref, ng_ref, gmg_ref, gmb_ref, ws_ref, bsf_ref,
                 cw_ref, cb_ref, dtb_ref, alog_ref, dsk_ref, sng_ref, fg_ref, y_ref,
                 hn_sc, xpad_sc, y_sc, ht_sc, ssd_sc, out_sc, tile_set, tile_set, tile)

    @pl.when(t == nt - 1)
    def _():
        conv_ref[0] = xpad_sc[SUBLANES - (CONV_WIDTH - 1):SUBLANES, :]
        ssm_ref[0] = ht_sc[...].T


def _prompt_layer(x, kb, vb, w_in_t, w_out, ng, gmg, gmb, ws, bsf, cw, cb, dtb, alog, dsk, sng, fg):
    b, seq, _ = x.shape
    tile = PROMPT_TILE
    nt = seq // tile
    total = b * nt
    assert seq % tile == 0 and tile % CHUNK == 0

    def const(shape):
        return pl.BlockSpec(shape, lambda s: (0,) * len(shape), pipeline_mode=pl.Buffered(1))

    cur = lambda s: s
    prev = lambda s: s
    in_specs = [
        pl.BlockSpec((1, tile, D_MODEL), lambda s: (cur(s) // nt, cur(s) % nt, 0)),
        pl.BlockSpec((1, MEM_LEN, MEM_WIDTH), lambda s: (cur(s) // nt, 0, 0)),
        pl.BlockSpec((1, MEM_LEN, MEM_WIDTH), lambda s: (cur(s) // nt, 0, 0)),
        const((IN_WIDTH, D_MODEL)),
        const((MIX_WIDTH, D_MODEL)),
        const((1, D_MODEL)), const((1, GM_WIDTH)), const((1, GM_WIDTH)),
        const((GM_HEADS, CHUNK, CHUNK)), const((CHUNK, GM_WIDTH)),
        const((CONV_WIDTH, CONV_DIM)), const((1, CONV_DIM)),
        const((1, DT_PAD)), const((1, DT_PAD)), const((1, SSM_WIDTH)), const((1, SSM_WIDTH)),
        const((1, D_MODEL)),
    ]
    out_specs = [
        pl.BlockSpec((1, tile, D_MODEL), lambda s: (prev(s) // nt, prev(s) % nt, 0)),
        pl.BlockSpec((1, SSM_WIDTH, SSM_STATE), lambda s: (prev(s) // nt, 0, 0)),
        pl.BlockSpec((1, CONV_WIDTH - 1, CONV_DIM), lambda s: (cur(s) // nt, 0, 0)),
    ]
    out_shape = [
        jax.ShapeDtypeStruct((b, seq, D_MODEL), F32),
        jax.ShapeDtypeStruct((b, SSM_WIDTH, SSM_STATE), F32),
        jax.ShapeDtypeStruct((b, CONV_WIDTH - 1, CONV_DIM), F32),
    ]
    handoff = [
        pltpu.VMEM((tile, MIX_WIDTH), BF16),
        pltpu.VMEM((tile, GM_WIDTH), F32),
        pltpu.VMEM((tile, GM_WIDTH), BF16),
        pltpu.VMEM((tile, SSM_WIDTH), F32),
        pltpu.VMEM((tile, SSM_WIDTH), F32),
        pltpu.VMEM((tile, SSM_GROUPS * SSM_STATE), F32),
        pltpu.VMEM((tile, SSM_GROUPS * SSM_STATE), F32),
        pltpu.VMEM((tile, DT_PAD), F32),
    ]
    assert len(handoff) == _Handoff.N
    scratch = [
        pltpu.VMEM((tile, D_MODEL), BF16),
        pltpu.VMEM((tile + SUBLANES, CONV_DIM), F32),
        pltpu.VMEM((CHUNK, SSM_WIDTH), F32),
        pltpu.VMEM((SSM_STATE, SSM_WIDTH), F32),
        pltpu.VMEM((6, CHUNK, CHUNK), F32),
        pltpu.VMEM((tile, D_MODEL), F32),
    ] + handoff
    return pl.pallas_call(
        functools.partial(_prompt_kernel, tile=tile, nt=nt, total=total),
        grid=(total,),
        in_specs=in_specs,
        out_specs=out_specs,
        out_shape=out_shape,
        scratch_shapes=scratch,
        compiler_params=pltpu.CompilerParams(dimension_semantics=("arbitrary",),
                                             vmem_limit_bytes=VMEM_LIMIT),
        name="prompt_layer",
    )(x, kb, vb, w_in_t, w_out, ng, gmg, gmb, ws, bsf, cw, cb, dtb, alog, dsk, sng, fg)


def _sample_proj_kernel(x_ref, g_ref, w_ref, o_ref):
    hn = _rms(x_ref[...], g_ref[...]).astype(BF16)
    o_ref[...] = _dot_nt(hn, w_ref[...])


def _sample_proj(x, g, w_t):
    m, n = x.shape[0], w_t.shape[0]
    return pl.pallas_call(
        _sample_proj_kernel,
        grid=(pl.cdiv(n, PROJ_BLOCK),),
        in_specs=[pl.BlockSpec((m, D_MODEL), lambda j: (0, 0)),
                  pl.BlockSpec((1, D_MODEL), lambda j: (0, 0)),
                  pl.BlockSpec((PROJ_BLOCK, D_MODEL), lambda j: (j, 0))],
        out_specs=pl.BlockSpec((m, PROJ_BLOCK), lambda j: (0, j)),
        out_shape=jax.ShapeDtypeStruct((m, n), F32),
        compiler_params=pltpu.CompilerParams(dimension_semantics=("arbitrary",),
                                             vmem_limit_bytes=VMEM_LIMIT),
        name="sample_proj",
    )(x, g, w_t)


def _sample_mix_kernel(p_ref, cst_ref, ssm_ref, k_ref, v_ref, gmg_ref, gmb_ref, gcoef_ref, gbias_ref,
                       cw_ref, cb_ref, dtb_ref, alog_ref, dsk_ref, sng_ref, expand_ref,
                       mix_ref, ssm_out_ref, conv_out_ref, gv_ref,
                       xp_sc, xbc_sc, att_sc, yoff_sc, dec_sc, *, bblk, seq):
    rows = bblk * seq

    def col(off, width):
        return p_ref[:, off:off + width]

    tpos = lax.broadcasted_iota(jnp.int32, (rows, 1), 0) % seq

    def back(a, j):
        return a if j == 0 else pltpu.roll(a, j, 0)

    u = _gelu(col(OFF_U, GM_WIDTH))
    v = _layernorm(_gelu(col(OFF_V, GM_WIDTH)), gmg_ref[...], gmb_ref[...])
    gv_ref[...] = v
    mixed = gbias_ref[...]
    for j in range(seq):
        mixed = mixed + gcoef_ref[j] * back(v, j)
    mix_ref[:, 0:GM_WIDTH] = (u * mixed * _silu(col(OFF_GATE, GM_WIDTH))).astype(BF16)

    xbc_raw = col(OFF_XBC, CONV_DIM)
    for b in range(bblk):
        xp_sc[b, 0:CONV_WIDTH - 1, :] = cst_ref[b]
        xp_sc[b, CONV_WIDTH - 1:CONV_WIDTH - 1 + seq, :] = xbc_raw[b * seq:(b + 1) * seq, :]
    for b in range(bblk):
        acc = jnp.broadcast_to(cb_ref[...], (seq, CONV_DIM))
        for j in range(CONV_WIDTH):
            acc = acc + cw_ref[j:j + 1, :] * xp_sc[b, j:j + seq, :]
        xbc_sc[b * seq:(b + 1) * seq, :] = _silu(acc)
        conv_out_ref[b] = xp_sc[b, seq:seq + CONV_WIDTH - 1, :]
    xs = xbc_sc[:, 0:SSM_WIDTH]
    bm = xbc_sc[:, SSM_WIDTH:SSM_WIDTH + SSM_GROUPS * SSM_STATE]
    cm = xbc_sc[:, SSM_WIDTH + SSM_GROUPS * SSM_STATE:CONV_DIM]

    lane = lax.broadcasted_iota(jnp.int32, (rows, DT_PAD), 1)
    dt = _softplus(col(OFF_DT, DT_PAD) + dtb_ref[...])
    adt = jnp.where(lane < SSM_HEADS, dt * (-jnp.exp(alog_ref[...])), 0.0)
    acum = adt
    for j in range(1, seq):
        acum = acum + jnp.where(tpos >= j, back(adt, j), 0.0)
    a_last = jnp.zeros_like(acum)
    for j in range(seq):
        a_last = a_last + jnp.where(tpos == seq - 1 - j, acum if j == 0 else pltpu.roll(acum, rows - j, 0), 0.0)
    coefs = []
    for j in range(seq):
        cbj = []
        for g in range(SSM_GROUPS):
            gs = slice(g * SSM_STATE, (g + 1) * SSM_STATE)
            cbj.append(jnp.sum(cm[:, gs] * back(bm[:, gs], j), axis=-1, keepdims=True))
        cb_l = jnp.where(lane < HEADS_PER_GROUP, cbj[0], cbj[1])
        valid = tpos >= j
        decay = jnp.exp(jnp.where(valid, acum - back(acum, j), 0.0))
        coefs.append(jnp.where(valid, cb_l * decay * back(dt, j), 0.0))
    coefs.append(jnp.exp(acum))
    coefs.append(dt * jnp.exp(a_last - acum))
    stack = jnp.concatenate(coefs, axis=0)
    hi = stack.astype(BF16)
    lo = (stack - hi.astype(F32)).astype(BF16)
    wide = _dot(hi, expand_ref[...]) + _dot(lo, expand_ref[...])
    y = dsk_ref[...] * xs
    for j in range(seq):
        y = y + wide[j * rows:(j + 1) * rows, :] * back(xs, j)
    e_wide = wide[seq * rows:(seq + 1) * rows, :]
    wx = (xs * wide[(seq + 1) * rows:(seq + 2) * rows, :]).astype(BF16)
    dec_rows = jnp.exp(a_last)

    q = col(OFF_Q, MEM_WIDTH)
    cmb = cm.astype(BF16)
    bmb = bm.astype(BF16)
    for b in range(bblk):
        rs = slice(b * seq, (b + 1) * seq)
        att = _attention_interleaved(
            q[rs, :],
            k_ref[0, b].reshape(MEM_LEN * MEM_HEADS, MEM_HEAD_DIM).astype(BF16),
            v_ref[0, b].reshape(MEM_LEN * MEM_HEADS, MEM_HEAD_DIM).astype(BF16))
        for h in range(MEM_HEADS):
            att_sc[rs, _head_slice(h)] = att[h * seq:(h + 1) * seq, :]
        h0 = ssm_ref[b]
        h0b = h0.astype(BF16)
        dec_sc[...] = jnp.broadcast_to(dec_rows[b * seq + seq - 1:b * seq + seq, :], (LANES, DT_PAD)).T
        for g in range(SSM_GROUPS):
            gs = slice(g * SSM_STATE, (g + 1) * SSM_STATE)
            ws_ = slice(g * GROUP_WIDTH, (g + 1) * GROUP_WIDTH)
            yoff_sc[rs, ws_] = _dot_nt(cmb[rs, gs], h0b[ws_, :])
            upd = _dot_tn(wx[rs, ws_], bmb[rs, gs])
            for kk in range(HEADS_PER_GROUP):
                k = g * HEADS_PER_GROUP + kk
                hs = slice(k * SSM_HEAD_DIM, (k + 1) * SSM_HEAD_DIM)
                ssm_out_ref[b, hs, :] = (h0[hs, :] * dec_sc[k:k + 1, :]
                                         + upd[kk * SSM_HEAD_DIM:(kk + 1) * SSM_HEAD_DIM, :])

    y = (y + yoff_sc[...] * e_wide) * _silu(col(OFF_Z, SSM_WIDTH))
    mix_ref[:, GM_WIDTH:GM_WIDTH + SSM_WIDTH] = _group_rmsnorm(y, sng_ref[...]).astype(BF16)
    mix_ref[:, GM_WIDTH + SSM_WIDTH:MIX_WIDTH] = (att_sc[...] * _silu(col(OFF_MGATE, MEM_WIDTH))).astype(BF16)


def _sample_mix(proj, conv_state, ssm_state, mem_k, mem_v, gmg, gmb, gcoef, gbias, cw, cb, dtb, alog,
                dsk, sng, expand, seq):
    nb = conv_state.shape[0]
    bblk = SAMPLE_BATCH_BLOCK
    rows = bblk * seq
    assert nb % bblk == 0 and rows % (2 * SUBLANES) == 0

    def const(shape):
        return pl.BlockSpec(shape, lambda i: (0,) * len(shape))

    in_specs = [
        pl.BlockSpec((rows, IN_WIDTH), lambda i: (i, 0)),
        pl.BlockSpec((bblk, CONV_WIDTH - 1, CONV_DIM), lambda i: (i, 0, 0)),
        pl.BlockSpec((bblk, SSM_WIDTH, SSM_STATE), lambda i: (i, 0, 0)),
        pl.BlockSpec((1, bblk, MEM_LEN, MEM_HEADS, MEM_HEAD_DIM), lambda i: (0, i, 0, 0, 0)),
        pl.BlockSpec((1, bblk, MEM_LEN, MEM_HEADS, MEM_HEAD_DIM), lambda i: (0, i, 0, 0, 0)),
        const((1, GM_WIDTH)), const((1, GM_WIDTH)),
        const((seq, rows, GM_WIDTH)), const((rows, GM_WIDTH)),
        const((CONV_WIDTH, CONV_DIM)), const((1, CONV_DIM)),
        const((1, DT_PAD)), const((1, DT_PAD)), const((1, SSM_WIDTH)), const((1, SSM_WIDTH)),
        const((DT_PAD, SSM_WIDTH)),
    ]
    out_specs = [
        pl.BlockSpec((rows, MIX_WIDTH), lambda i: (i, 0)),
        pl.BlockSpec((bblk, SSM_WIDTH, SSM_STATE), lambda i: (i, 0, 0)),
        pl.BlockSpec((bblk, CONV_WIDTH - 1, CONV_DIM), lambda i: (i, 0, 0)),
        pl.BlockSpec((rows, GM_WIDTH), lambda i: (i, 0)),
    ]
    out_shape = [
        jax.ShapeDtypeStruct((nb * seq, MIX_WIDTH), BF16),
        jax.ShapeDtypeStruct((nb, SSM_WIDTH, SSM_STATE), F32),
        jax.ShapeDtypeStruct((nb, CONV_WIDTH - 1, CONV_DIM), F32),
        jax.ShapeDtypeStruct((nb * seq, GM_WIDTH), F32),
    ]
    scratch = [
        pltpu.VMEM((bblk, SUBLANES, CONV_DIM), F32),
        pltpu.VMEM((rows, CONV_DIM), F32),
        pltpu.VMEM((rows, MEM_WIDTH), F32),
        pltpu.VMEM((rows, SSM_WIDTH), F32),
        pltpu.VMEM((DT_PAD, LANES), F32),
    ]
    return pl.pallas_call(
        functools.partial(_sample_mix_kernel, bblk=bblk, seq=seq),
        grid=(nb // bblk,),
        in_specs=in_specs,
        out_specs=out_specs,
        out_shape=out_shape,
        scratch_shapes=scratch,
        compiler_params=pltpu.CompilerParams(dimension_semantics=("arbitrary",),
                                             vmem_limit_bytes=VMEM_LIMIT),
        name="sample_mix",
    )(proj, conv_state, ssm_state, mem_k, mem_v, gmg, gmb, gcoef, gbias, cw, cb, dtb, alog, dsk, sng, expand)


def _sample_out_kernel(mix_ref, x_ref, w_ref, g_ref, o_ref):
    out = x_ref[...] + _dot(mix_ref[...], w_ref[...])
    o_ref[...] = _rms(out, g_ref[...])


def _sample_out(mix, x, w_out, g):
    m = x.shape[0]
    full = lambda shape: pl.BlockSpec(shape, lambda i: (0,) * len(shape))
    return pl.pallas_call(
        _sample_out_kernel,
        grid=(1,),
        in_specs=[full((m, MIX_WIDTH)), full((m, D_MODEL)), full((MIX_WIDTH, D_MODEL)), full((1, D_MODEL))],
        out_specs=full((m, D_MODEL)),
        out_shape=jax.ShapeDtypeStruct((m, D_MODEL), F32),
        compiler_params=pltpu.CompilerParams(dimension_semantics=("arbitrary",),
                                             vmem_limit_bytes=VMEM_LIMIT),
        name="sample_out",
    )(mix, x, w_out, g)


def _pad_heads(a):
    return jnp.pad(a.astype(F32), (0, DT_PAD - SSM_HEADS)).reshape(1, DT_PAD)


def kernel(x_prompt, x_sample, mem_prompt, state_ssm, state_conv, cache_mem_k, cache_mem_v, norm_g, w_in,
           gm_norm_g, gm_norm_b, gm_w_spatial, gm_b_spatial, conv_w, conv_b, dt_bias, a_log, d_skip,
           ssm_norm_g, mem_norm_g, w_mem_k, w_mem_v, w_out, final_norm_g):
    assert norm_g.shape[0] == 1, "single layer"
    bp, seq_p, _ = x_prompt.shape
    bs, seq_s, _ = x_sample.shape
    row = lambda a: a.reshape(1, -1).astype(F32)

    w_in_t = jnp.swapaxes(w_in[0], 0, 1).astype(BF16)
    w_out_b = w_out[0].astype(BF16)
    ng, gmg, gmb = row(norm_g[0]), row(gm_norm_g[0]), row(gm_norm_b[0])
    cw, cb = conv_w[0].astype(F32), row(conv_b[0])
    dtb, alog = _pad_heads(dt_bias[0]), _pad_heads(a_log[0])
    dsk = row(jnp.repeat(d_skip[0], SSM_HEAD_DIM))
    sng, fg = row(ssm_norm_g[0]), row(final_norm_g)
    w_sp = gm_w_spatial[0]
    tril_p = jnp.tril(jnp.ones((CHUNK, CHUNK), bool))
    ws_p = jnp.where(tril_p, w_sp, 0).astype(BF16)
    bsf_p = jnp.repeat(gm_b_spatial[0].T, GM_HEAD_DIM, axis=1).astype(F32)

    mk, mv, mkb, mvb = _memory_kv(mem_prompt, row(mem_norm_g[0]), w_mem_k[0].astype(BF16),
                                  w_mem_v[0].astype(BF16))
    y_p, ssm_p, conv_p = _prompt_layer(x_prompt, mkb, mvb, w_in_t, w_out_b, ng, gmg, gmb, ws_p, bsf_p,
                                       cw, cb, dtb, alog, dsk, sng, fg)

    rows = SAMPLE_BATCH_BLOCK * seq_s
    tpos = jnp.arange(rows) % seq_s
    gcoef = jnp.stack([
        jnp.where((tpos >= j)[:, None],
                  jnp.repeat(w_sp[:, tpos, jnp.maximum(tpos - j, 0)].T, GM_HEAD_DIM, axis=1), 0.0)
        for j in range(seq_s)]).astype(F32)
    gbias = jnp.repeat(gm_b_spatial[0][:, tpos].T, GM_HEAD_DIM, axis=1).astype(F32)
    expand = (jnp.arange(DT_PAD)[:, None] == (jnp.arange(SSM_WIDTH) // SSM_HEAD_DIM)[None, :]).astype(BF16)

    xs2 = x_sample.reshape(bs * seq_s, D_MODEL)
    proj_s = _sample_proj(xs2, ng, w_in_t)
    mix_s, ssm_s, conv_s, gv_s = _sample_mix(
        proj_s, state_conv[0], state_ssm[0].reshape(bs, SSM_WIDTH, SSM_STATE),
        cache_mem_k, cache_mem_v,
        gmg, gmb, gcoef, gbias, cw, cb, dtb, alog, dsk, sng, expand, seq_s)
    y_s = _sample_out(mix_s, xs2, w_out_b, fg)

    return (y_p,
            y_s.reshape(bs, seq_s, D_MODEL),
            ssm_p.reshape(1, bp, SSM_HEADS, SSM_HEAD_DIM, SSM_STATE),
            conv_p[None],
            mk,
            mv,
            ssm_s.reshape(1, bs, SSM_HEADS, SSM_HEAD_DIM, SSM_STATE),
            conv_s[None],
            gv_s.reshape(1, bs, seq_s, GM_WIDTH))
```

```python
import functools
import math

import jax
import jax.numpy as jnp
from jax import lax
from jax.experimental import pallas as pl
from jax.experimental.pallas import tpu as pltpu

F32 = jnp.float32
BF16 = jnp.bfloat16

D_MODEL = 1024
GM_WIDTH = 1024
GM_HEADS = 8
GM_HEAD_DIM = 128
CHUNK = 128
SSM_WIDTH = 1024
SSM_HEADS = 16
SSM_HEAD_DIM = 64
SSM_GROUPS = 2
SSM_STATE = 128
HEADS_PER_GROUP = SSM_HEADS // SSM_GROUPS
GROUP_WIDTH = SSM_WIDTH // SSM_GROUPS
CONV_WIDTH = 4
CONV_DIM = SSM_WIDTH + 2 * SSM_GROUPS * SSM_STATE
MEM_LEN = 256
MEM_HEADS = 4
MEM_HEAD_DIM = 256
MEM_WIDTH = 1024
MIX_WIDTH = GM_WIDTH + SSM_WIDTH + MEM_WIDTH
EPS = 1e-6

LANES = 128
SUBLANES = 8

DT_PAD = LANES
OFF_U = 0
OFF_V = OFF_U + GM_WIDTH
OFF_GATE = OFF_V + GM_WIDTH
OFF_Z = OFF_GATE + GM_WIDTH
OFF_XBC = OFF_Z + SSM_WIDTH
OFF_DT = OFF_XBC + CONV_DIM
OFF_Q = OFF_DT + SSM_HEADS
OFF_MGATE = OFF_Q + MEM_WIDTH
IN_WIDTH = OFF_MGATE + MEM_WIDTH
PROJ_BLOCK = 1024

PROMPT_TILE = 256
SAMPLE_BATCH_BLOCK = 4
VMEM_LIMIT = 56 * 1024 * 1024


def _rms(x, g):
    return x * lax.rsqrt(jnp.mean(x * x, axis=-1, keepdims=True) + EPS) * g


def _gelu(x):
    return 0.5 * x * (1.0 + lax.erf(x * math.sqrt(0.5)))


def _silu(x):
    return x * jax.nn.sigmoid(x)


def _softplus(x):
    return jnp.maximum(x, 0.0) + jnp.log1p(jnp.exp(-jnp.abs(x)))


def _layernorm(x, g, b):
    mu = jnp.mean(x, axis=-1, keepdims=True)
    xc = x - mu
    var = jnp.mean(xc * xc, axis=-1, keepdims=True)
    return xc * lax.rsqrt(var + EPS) * g + b


def _dot(a, b):
    return jnp.dot(a, b, preferred_element_type=F32)


def _dot_nt(a, b):
    return lax.dot_general(a, b, (((1,), (1,)), ((), ())), preferred_element_type=F32)


def _dot_tn(a, b):
    return lax.dot_general(a, b, (((0,), (0,)), ((), ())), preferred_element_type=F32)


def _group_rmsnorm(y, g):
    halves = []
    for i in range(SSM_GROUPS):
        yg = y[:, i * GROUP_WIDTH:(i + 1) * GROUP_WIDTH]
        halves.append(yg * lax.rsqrt(jnp.mean(yg * yg, axis=-1, keepdims=True) + EPS))
    return jnp.concatenate(halves, axis=-1) * g


def _memory_kv_kernel(mem_ref, g_ref, wk_ref, wv_ref, k_ref, v_ref, kb_ref, vb_ref):
    m = _rms(mem_ref[0], g_ref[...]).astype(BF16)
    k = _dot(m, wk_ref[...])
    v = _dot(m, wv_ref[...])
    for h in range(MEM_HEADS):
        k_ref[0, 0, :, h, :] = k[:, _head_slice(h)]
        v_ref[0, 0, :, h, :] = v[:, _head_slice(h)]
    kb_ref[0] = k.astype(BF16)
    vb_ref[0] = v.astype(BF16)


def _memory_kv(mem, g, wk, wv):
    b = mem.shape[0]
    blk = pl.BlockSpec((1, MEM_LEN, D_MODEL), lambda i: (i, 0, 0))
    blk5 = pl.BlockSpec((1, 1, MEM_LEN, MEM_HEADS, MEM_HEAD_DIM), lambda i: (0, i, 0, 0, 0))
    const = lambda shape: pl.BlockSpec(shape, lambda i: (0,) * len(shape))
    return pl.pallas_call(
        _memory_kv_kernel,
        grid=(b,),
        in_specs=[blk, const((1, D_MODEL)), const((D_MODEL, MEM_WIDTH)), const((D_MODEL, MEM_WIDTH))],
        out_specs=[blk5, blk5, blk, blk],
        out_shape=[jax.ShapeDtypeStruct((1, b, MEM_LEN, MEM_HEADS, MEM_HEAD_DIM), F32)] * 2
        + [jax.ShapeDtypeStruct((b, MEM_LEN, MEM_WIDTH), BF16)] * 2,
        compiler_params=pltpu.CompilerParams(dimension_semantics=("arbitrary",),
                                             vmem_limit_bytes=VMEM_LIMIT),
        name="memory_kv",
    )(mem, g, wk, wv)


def _head_slice(h):
    return slice(h * MEM_HEAD_DIM, (h + 1) * MEM_HEAD_DIM)


def _attention(q, k_head, v_head):
    qb = (q * (MEM_HEAD_DIM ** -0.5)).astype(BF16)
    outs = []
    for h in range(MEM_HEADS):
        s = _dot_nt(qb[:, _head_slice(h)], k_head(h))
        p = jnp.exp(s - jnp.max(s, axis=-1, keepdims=True))
        denom = jnp.sum(p, axis=-1, keepdims=True)
        outs.append(_dot(p.astype(BF16), v_head(h)) / denom)
    return jnp.concatenate(outs, axis=-1)


def _attention_interleaved(q, k2, v2):
    t = q.shape[0]
    qb = (q * (MEM_HEAD_DIM ** -0.5)).astype(BF16)
    q2 = jnp.concatenate([qb[:, _head_slice(h)] for h in range(MEM_HEADS)], axis=0)
    s = _dot_nt(q2, k2)
    row_head = lax.broadcasted_iota(jnp.int32, s.shape, 0) // t
    col_head = lax.broadcasted_iota(jnp.int32, s.shape, 1) % MEM_HEADS
    s = jnp.where(row_head == col_head, s, -jnp.inf)
    p = jnp.exp(s - jnp.max(s, axis=-1, keepdims=True))
    denom = jnp.sum(p, axis=-1, keepdims=True)
    return _dot(p.astype(BF16), v2) / denom


class _Handoff:
    N = 8

    def __init__(self, refs):
        (self.mix, self.ug, self.v, self.zs, self.xs, self.bm, self.cm, self.dt) = refs


def _prompt_step(xc_ref, xp_ref, kb_ref, vb_ref, win_ref, wout_ref, ng_ref, gmg_ref, gmb_ref, ws_ref, bsf_ref,
                 cw_ref, cb_ref, dtb_ref, alog_ref, dsk_ref, sng_ref, fg_ref, y_ref,
                 hn_sc, xp_sc, ht_sc, ssd_sc, out_sc, yd_sc, st_sc, ea_sc, new, old, tile):
    a_row = -jnp.exp(alog_ref[...])
    row = lax.broadcasted_iota(jnp.int32, (CHUNK, CHUNK), 0)
    col = lax.broadcasted_iota(jnp.int32, (CHUNK, CHUNK), 1)
    causal = row >= col
    tril_b = causal.astype(BF16)
    head_lane = col < SSM_HEADS
    first_half = col < SSM_HEAD_DIM
    def tables(c, g=0):
        base = c * 8
        return (ssd_sc.at[base], ssd_sc.at[base + 1], ssd_sc.at[base + 2], ssd_sc.at[base + 3],
                ssd_sc.at[base + 4 + 2 * g], ssd_sc.at[base + 5 + 2 * g])

    def proj(off, width):
        return _dot_nt(hn_sc[...], win_ref[off:off + width, :])

    def gmlp_mix(c):
        r = slice(c * CHUNK, (c + 1) * CHUNK)
        for h in range(GM_HEADS):
            hs = slice(h * GM_HEAD_DIM, (h + 1) * GM_HEAD_DIM)
            mixed = _dot(ws_ref[h], old.v[r, hs]) + bsf_ref[:, hs]
            old.mix[r, hs] = (old.ug[r, hs] * mixed).astype(BF16)

    def seq_chunk_start(c):
        r = slice(c * CHUNK, (c + 1) * CHUNK)
        acum_sc, acum_t_sc, dt_t_sc, to_end_t_sc, _, _ = tables(c)
        dt = old.dt[r, :]
        adt = jnp.where(head_lane, dt * a_row, 0.0)
        hi = adt.astype(BF16)
        rest = adt - hi.astype(F32)
        mid = rest.astype(BF16)
        lo = (rest - mid.astype(F32)).astype(BF16)
        acum = _dot(tril_b, hi) + _dot(tril_b, mid) + _dot(tril_b, lo)
        acum_t = acum.T
        dt_t = dt.T
        acum_sc[...] = acum
        acum_t_sc[...] = acum_t
        dt_t_sc[...] = dt_t
        to_end_t_sc[...] = jnp.exp(acum_t[:, CHUNK - 1:CHUNK] - acum_t) * dt_t
        for g in range(SSM_GROUPS):
            gs = slice(g * SSM_STATE, (g + 1) * SSM_STATE)
            _, _, _, _, cb_sc, bm_t_sc = tables(c, g)
            bm_g = old.bm[r, gs]
            cb_sc[...] = _dot_nt(old.cm[r, gs].astype(BF16), bm_g.astype(BF16))
            bm_t_sc[...] = bm_g.T

    def seq_group(c, g):
        r = slice(c * CHUNK, (c + 1) * CHUNK)
        acum_sc, acum_t_sc, dt_t_sc, to_end_t_sc, cb_sc, bm_t_sc = tables(c, g)
        for kk in range(0, HEADS_PER_GROUP, 2):
            k0 = g * HEADS_PER_GROUP + kk
            ps = slice(k0 * SSM_HEAD_DIM, (k0 + 2) * SSM_HEAD_DIM)
            m_parts, b_parts, ea = [], [], []
            for k in (k0, k0 + 1):
                a_col = jnp.broadcast_to(acum_sc[:, k:k + 1], (CHUNK, CHUNK))
                diff = a_col - acum_t_sc[k:k + 1, :]
                decay = jnp.exp(jnp.where(causal, diff, -jnp.inf))
                m_parts.append((cb_sc[...] * decay * dt_t_sc[k:k + 1, :]).astype(BF16))
                b_parts.append((bm_t_sc[...] * to_end_t_sc[k:k + 1, :]).astype(BF16))
                ea.append(jnp.exp(a_col))
            xs_pair = old.xs[r, ps]
            zero = jnp.zeros_like(xs_pair)
            xs_lo = jnp.where(first_half, xs_pair, zero).astype(BF16)
            xs_hi = jnp.where(first_half, zero, xs_pair).astype(BF16)
            lhs = jnp.concatenate([jnp.concatenate(m_parts, axis=1), jnp.concatenate(b_parts, axis=1)], axis=0)
            both = _dot(lhs, jnp.concatenate([xs_lo, xs_hi], axis=0))
            yd_sc[c, :, ps] = both[0:CHUNK, :]
            st_sc[c, :, ps] = both[CHUNK:2 * CHUNK, :]
            ea_sc[c, :, ps] = jnp.where(first_half, ea[0], ea[1])

    def seq_chunk_end(c):
        r = slice(c * CHUNK, (c + 1) * CHUNK)
        h_b = ht_sc[...].astype(BF16)
        y_off = jnp.concatenate(
            [_dot(old.cm[r, g * SSM_STATE:(g + 1) * SSM_STATE].astype(BF16), h_b[:, g * GROUP_WIDTH:(g + 1) * GROUP_WIDTH])
             for g in range(SSM_GROUPS)], axis=1)
        ea_c = ea_sc[c]
        y = yd_sc[c] + y_off * ea_c
        ht_sc[...] = ht_sc[...] * ea_c[CHUNK - 1:CHUNK, :] + st_sc[c]
        y = (y + dsk_ref[...] * old.xs[r, :]) * old.zs[r, :]
        old.mix[r, GM_WIDTH:GM_WIDTH + SSM_WIDTH] = _group_rmsnorm(y, sng_ref[...]).astype(BF16)

    def out_piece(j, n):
        cs = slice(j * D_MODEL // n, (j + 1) * D_MODEL // n)
        out_sc[:, cs] = xp_ref[0, :, cs] + _dot(old.mix[...], wout_ref[:, cs])

    assert tile // CHUNK == 2, "the emission order below is written for two chunks per tile"

    hn_sc[...] = _rms(xc_ref[0], ng_ref[...]).astype(BF16)

    new.zs[...] = _silu(proj(OFF_Z, SSM_WIDTH))
    xp_sc[SUBLANES:SUBLANES + tile, :] = proj(OFF_XBC, CONV_DIM)
    ext = xp_sc[...].reshape(tile // SUBLANES + 1, SUBLANES, CONV_DIM)
    sub = lax.broadcasted_iota(jnp.int32, (1, SUBLANES, CONV_DIM), 1)
    acc = cb_ref[...] + cw_ref[CONV_WIDTH - 1:CONV_WIDTH, :] * ext[1:]
    for shift in range(1, CONV_WIDTH):
        rot = pltpu.roll(ext, shift, 1)
        shifted = jnp.where(sub >= shift, rot[1:], rot[:-1])
        acc = acc + cw_ref[CONV_WIDTH - 1 - shift:CONV_WIDTH - shift, :] * shifted
    acc = acc.reshape(tile, CONV_DIM)
    xbc = _silu(acc)
    new.xs[...] = xbc[:, :SSM_WIDTH]
    new.bm[...] = xbc[:, SSM_WIDTH:SSM_WIDTH + SSM_GROUPS * SSM_STATE]
    new.cm[...] = xbc[:, SSM_WIDTH + SSM_GROUPS * SSM_STATE:]
    xp_sc[SUBLANES - (CONV_WIDTH - 1):SUBLANES, :] = xp_sc[tile + SUBLANES - (CONV_WIDTH - 1):tile + SUBLANES, :]
    new.dt[...] = _softplus(proj(OFF_DT, DT_PAD) + dtb_ref[...])
    u = _gelu(proj(OFF_U, GM_WIDTH))
    new.ug[...] = u * _silu(proj(OFF_GATE, GM_WIDTH))
    new.v[...] = _layernorm(_gelu(proj(OFF_V, GM_WIDTH)), gmg_ref[...], gmb_ref[...]).astype(BF16)
    seq_chunk_start(0)
    seq_chunk_start(1)
    gmlp_mix(0)
    gmlp_mix(1)
    seq_group(0, 0)
    seq_group(0, 1)
    seq_chunk_end(0)
    att = _attention(proj(OFF_Q, MEM_WIDTH), lambda h: kb_ref[0, :, _head_slice(h)], lambda h: vb_ref[0, :, _head_slice(h)])
    new.mix[:, GM_WIDTH + SSM_WIDTH:MIX_WIDTH] = (att * _silu(proj(OFF_MGATE, MEM_WIDTH))).astype(BF16)
    seq_group(1, 0)
    seq_group(1, 1)
    seq_chunk_end(1)

    n_out = 4
    for j in range(n_out):
        out_piece(j, n_out)
    y_ref[0] = _rms(out_sc[...], fg_ref[...])


def _prompt_kernel(xc_ref, kb_ref, vb_ref, win_ref, wout_ref, ng_ref, gmg_ref, gmb_ref,
                   ws_ref, bsf_ref, cw_ref, cb_ref, dtb_ref, alog_ref, dsk_ref, sng_ref, fg_ref,
                   y_ref, ssm_ref, conv_ref, hn_sc, xpad_sc, ht_sc, ssd_sc, out_sc, yd_sc, st_sc, ea_sc, *handoff, tile, nt, total):
    t = pl.program_id(0) % nt
    tile_set = _Handoff(handoff)

    @pl.when(t == 0)
    def _():
        xpad_sc[0:SUBLANES, :] = jnp.zeros((SUBLANES, CONV_DIM), F32)
        ht_sc[...] = jnp.zeros_like(ht_sc)

    _prompt_step(xc_ref, xc_ref, kb_ref, vb_ref, win_ref, wout_ref, ng_ref, gmg_ref, gmb_ref, ws_ref, bsf_ref,
                 cw_ref, cb_ref, dtb_ref, alog_ref, dsk_ref, sng_ref, fg_ref, y_ref,
                 hn_sc, xpad_sc, ht_sc, ssd_sc, out_sc, yd_sc, st_sc, ea_sc, tile_set, tile_set, tile)

    @pl.when(t == nt - 1)
    def _():
        conv_ref[0] = xpad_sc[SUBLANES - (CONV_WIDTH - 1):SUBLANES, :]
        ssm_ref[0] = ht_sc[...].T


def _prompt_layer(x, kb, vb, w_in_t, w_out, ng, gmg, gmb, ws, bsf, cw, cb, dtb, alog, dsk, sng, fg):
    b, seq, _ = x.shape
    tile = PROMPT_TILE
    nt = seq // tile
    total = b * nt
    assert seq % tile == 0 and tile % CHUNK == 0

    def const(shape):
        return pl.BlockSpec(shape, lambda s: (0,) * len(shape), pipeline_mode=pl.Buffered(1))

    cur = lambda s: s
    prev = lambda s: s
    in_specs = [
        pl.BlockSpec((1, tile, D_MODEL), lambda s: (cur(s) // nt, cur(s) % nt, 0)),
        pl.BlockSpec((1, MEM_LEN, MEM_WIDTH), lambda s: (cur(s) // nt, 0, 0)),
        pl.BlockSpec((1, MEM_LEN, MEM_WIDTH), lambda s: (cur(s) // nt, 0, 0)),
        const((IN_WIDTH, D_MODEL)),
        const((MIX_WIDTH, D_MODEL)),
        const((1, D_MODEL)), const((1, GM_WIDTH)), const((1, GM_WIDTH)),
        const((GM_HEADS, CHUNK, CHUNK)), const((CHUNK, GM_WIDTH)),
        const((CONV_WIDTH, CONV_DIM)), const((1, CONV_DIM)),
        const((1, DT_PAD)), const((1, DT_PAD)), const((1, SSM_WIDTH)), const((1, SSM_WIDTH)),
        const((1, D_MODEL)),
    ]
    out_specs = [
        pl.BlockSpec((1, tile, D_MODEL), lambda s: (prev(s) // nt, prev(s) % nt, 0)),
        pl.BlockSpec((1, SSM_WIDTH, SSM_STATE), lambda s: (prev(s) // nt, 0, 0)),
        pl.BlockSpec((1, CONV_WIDTH - 1, CONV_DIM), lambda s: (cur(s) // nt, 0, 0)),
    ]
    out_shape = [
        jax.ShapeDtypeStruct((b, seq, D_MODEL), F32),
        jax.ShapeDtypeStruct((b, SSM_WIDTH, SSM_STATE), F32),
        jax.ShapeDtypeStruct((b, CONV_WIDTH - 1, CONV_DIM), F32),
    ]
    handoff = [
        pltpu.VMEM((tile, MIX_WIDTH), BF16),
        pltpu.VMEM((tile, GM_WIDTH), F32),
        pltpu.VMEM((tile, GM_WIDTH), BF16),
        pltpu.VMEM((tile, SSM_WIDTH), F32),
        pltpu.VMEM((tile, SSM_WIDTH), F32),
        pltpu.VMEM((tile, SSM_GROUPS * SSM_STATE), F32),
        pltpu.VMEM((tile, SSM_GROUPS * SSM_STATE), F32),
        pltpu.VMEM((tile, DT_PAD), F32),
    ]
    assert len(handoff) == _Handoff.N
    scratch = [
        pltpu.VMEM((tile, D_MODEL), BF16),
        pltpu.VMEM((tile + SUBLANES, CONV_DIM), F32),
        pltpu.VMEM((SSM_STATE, SSM_WIDTH), F32),
        pltpu.VMEM((16, CHUNK, CHUNK), F32),
        pltpu.VMEM((tile, D_MODEL), F32),
        pltpu.VMEM((tile // CHUNK, CHUNK, SSM_WIDTH), F32),
        pltpu.VMEM((tile // CHUNK, SSM_STATE, SSM_WIDTH), F32),
        pltpu.VMEM((tile // CHUNK, CHUNK, SSM_WIDTH), F32),
    ] + handoff
    return pl.pallas_call(
        functools.partial(_prompt_kernel, tile=tile, nt=nt, total=total),
        grid=(total,),
        in_specs=in_specs,
        out_specs=out_specs,
        out_shape=out_shape,
        scratch_shapes=scratch,
        compiler_params=pltpu.CompilerParams(dimension_semantics=("arbitrary",),
                                             vmem_limit_bytes=VMEM_LIMIT),
        name="prompt_layer",
    )(x, kb, vb, w_in_t, w_out, ng, gmg, gmb, ws, bsf, cw, cb, dtb, alog, dsk, sng, fg)


def _sample_proj_kernel(x_ref, g_ref, w_ref, o_ref):
    hn = _rms(x_ref[...], g_ref[...]).astype(BF16)
    o_ref[...] = _dot_nt(hn, w_ref[...])


def _sample_proj(x, g, w_t):
    m, n = x.shape[0], w_t.shape[0]
    return pl.pallas_call(
        _sample_proj_kernel,
        grid=(pl.cdiv(n, PROJ_BLOCK),),
        in_specs=[pl.BlockSpec((m, D_MODEL), lambda j: (0, 0)),
                  pl.BlockSpec((1, D_MODEL), lambda j: (0, 0)),
                  pl.BlockSpec((PROJ_BLOCK, D_MODEL), lambda j: (j, 0))],
        out_specs=pl.BlockSpec((m, PROJ_BLOCK), lambda j: (0, j)),
        out_shape=jax.ShapeDtypeStruct((m, n), F32),
        compiler_params=pltpu.CompilerParams(dimension_semantics=("arbitrary",),
                                             vmem_limit_bytes=VMEM_LIMIT),
        name="sample_proj",
    )(x, g, w_t)


def _sample_mix_kernel(p_ref, cst_ref, ssm_ref, k_ref, v_ref, gmg_ref, gmb_ref, gcoef_ref, gbias_ref,
                       cw_ref, cb_ref, dtb_ref, alog_ref, dsk_ref, sng_ref, expand_ref,
                       mix_ref, ssm_out_ref, conv_out_ref, gv_ref,
                       xp_sc, xbc_sc, att_sc, yoff_sc, dec_sc, *, bblk, seq):
    rows = bblk * seq

    def col(off, width):
        return p_ref[:, off:off + width]

    tpos = lax.broadcasted_iota(jnp.int32, (rows, 1), 0) % seq

    def back(a, j):
        return a if j == 0 else pltpu.roll(a, j, 0)

    u = _gelu(col(OFF_U, GM_WIDTH))
    v = _layernorm(_gelu(col(OFF_V, GM_WIDTH)), gmg_ref[...], gmb_ref[...])
    gv_ref[...] = v
    mixed = gbias_ref[...]
    for j in range(seq):
        mixed = mixed + gcoef_ref[j] * back(v, j)
    mix_ref[:, 0:GM_WIDTH] = (u * mixed * _silu(col(OFF_GATE, GM_WIDTH))).astype(BF16)

    xbc_raw = col(OFF_XBC, CONV_DIM)
    for b in range(bblk):
        xp_sc[b, 0:CONV_WIDTH - 1, :] = cst_ref[b]
        xp_sc[b, CONV_WIDTH - 1:CONV_WIDTH - 1 + seq, :] = xbc_raw[b * seq:(b + 1) * seq, :]
    for b in range(bblk):
        acc = jnp.broadcast_to(cb_ref[...], (seq, CONV_DIM))
        for j in range(CONV_WIDTH):
            acc = acc + cw_ref[j:j + 1, :] * xp_sc[b, j:j + seq, :]
        xbc_sc[b * seq:(b + 1) * seq, :] = _silu(acc)
        conv_out_ref[b] = xp_sc[b, seq:seq + CONV_WIDTH - 1, :]
    xs = xbc_sc[:, 0:SSM_WIDTH]
    bm = xbc_sc[:, SSM_WIDTH:SSM_WIDTH + SSM_GROUPS * SSM_STATE]
    cm = xbc_sc[:, SSM_WIDTH + SSM_GROUPS * SSM_STATE:CONV_DIM]

    lane = lax.broadcasted_iota(jnp.int32, (rows, DT_PAD), 1)
    dt = _softplus(col(OFF_DT, DT_PAD) + dtb_ref[...])
    adt = jnp.where(lane < SSM_HEADS, dt * (-jnp.exp(alog_ref[...])), 0.0)
    acum = adt
    for j in range(1, seq):
        acum = acum + jnp.where(tpos >= j, back(adt, j), 0.0)
    a_last = jnp.zeros_like(acum)
    for j in range(seq):
        a_last = a_last + jnp.where(tpos == seq - 1 - j, acum if j == 0 else pltpu.roll(acum, rows - j, 0), 0.0)
    coefs = []
    for j in range(seq):
        cbj = []
        for g in range(SSM_GROUPS):
            gs = slice(g * SSM_STATE, (g + 1) * SSM_STATE)
            cbj.append(jnp.sum(cm[:, gs] * back(bm[:, gs], j), axis=-1, keepdims=True))
        cb_l = jnp.where(lane < HEADS_PER_GROUP, cbj[0], cbj[1])
        valid = tpos >= j
        decay = jnp.exp(jnp.where(valid, acum - back(acum, j), 0.0))
        coefs.append(jnp.where(valid, cb_l * decay * back(dt, j), 0.0))
    coefs.append(jnp.exp(acum))
    coefs.append(dt * jnp.exp(a_last - acum))
    stack = jnp.concatenate(coefs, axis=0)
    hi = stack.astype(BF16)
    lo = (stack - hi.astype(F32)).astype(BF16)
    wide = _dot(hi, expand_ref[...]) + _dot(lo, expand_ref[...])
    y = dsk_ref[...] * xs
    for j in range(seq):
        y = y + wide[j * rows:(j + 1) * rows, :] * back(xs, j)
    e_wide = wide[seq * rows:(seq + 1) * rows, :]
    wx = (xs * wide[(seq + 1) * rows:(seq + 2) * rows, :]).astype(BF16)
    dec_rows = jnp.exp(a_last)

    q = col(OFF_Q, MEM_WIDTH)
    cmb = cm.astype(BF16)
    bmb = bm.astype(BF16)
    for b in range(bblk):
        rs = slice(b * seq, (b + 1) * seq)
        att = _attention_interleaved(
            q[rs, :],
            k_ref[0, b].reshape(MEM_LEN * MEM_HEADS, MEM_HEAD_DIM).astype(BF16),
            v_ref[0, b].reshape(MEM_LEN * MEM_HEADS, MEM_HEAD_DIM).astype(BF16))
        for h in range(MEM_HEADS):
            att_sc[rs, _head_slice(h)] = att[h * seq:(h + 1) * seq, :]
        h0 = ssm_ref[b]
        h0b = h0.astype(BF16)
        dec_sc[...] = jnp.broadcast_to(dec_rows[b * seq + seq - 1:b * seq + seq, :], (LANES, DT_PAD)).T
        for g in range(SSM_GROUPS):
            gs = slice(g * SSM_STATE, (g + 1) * SSM_STATE)
            ws_ = slice(g * GROUP_WIDTH, (g + 1) * GROUP_WIDTH)
            yoff_sc[rs, ws_] = _dot_nt(cmb[rs, gs], h0b[ws_, :])
            upd = _dot_tn(wx[rs, ws_], bmb[rs, gs])
            for kk in range(HEADS_PER_GROUP):
                k = g * HEADS_PER_GROUP + kk
                hs = slice(k * SSM_HEAD_DIM, (k + 1) * SSM_HEAD_DIM)
                ssm_out_ref[b, hs, :] = (h0[hs, :] * dec_sc[k:k + 1, :]
                                         + upd[kk * SSM_HEAD_DIM:(kk + 1) * SSM_HEAD_DIM, :])

    y = (y + yoff_sc[...] * e_wide) * _silu(col(OFF_Z, SSM_WIDTH))
    mix_ref[:, GM_WIDTH:GM_WIDTH + SSM_WIDTH] = _group_rmsnorm(y, sng_ref[...]).astype(BF16)
    mix_ref[:, GM_WIDTH + SSM_WIDTH:MIX_WIDTH] = (att_sc[...] * _silu(col(OFF_MGATE, MEM_WIDTH))).astype(BF16)


def _sample_mix(proj, conv_state, ssm_state, mem_k, mem_v, gmg, gmb, gcoef, gbias, cw, cb, dtb, alog,
                dsk, sng, expand, seq):
    nb = conv_state.shape[0]
    bblk = SAMPLE_BATCH_BLOCK
    rows = bblk * seq
    assert nb % bblk == 0 and rows % (2 * SUBLANES) == 0

    def const(shape):
        return pl.BlockSpec(shape, lambda i: (0,) * len(shape))

    in_specs = [
        pl.BlockSpec((rows, IN_WIDTH), lambda i: (i, 0)),
        pl.BlockSpec((bblk, CONV_WIDTH - 1, CONV_DIM), lambda i: (i, 0, 0)),
        pl.BlockSpec((bblk, SSM_WIDTH, SSM_STATE), lambda i: (i, 0, 0)),
        pl.BlockSpec((1, bblk, MEM_LEN, MEM_HEADS, MEM_HEAD_DIM), lambda i: (0, i, 0, 0, 0)),
        pl.BlockSpec((1, bblk, MEM_LEN, MEM_HEADS, MEM_HEAD_DIM), lambda i: (0, i, 0, 0, 0)),
        const((1, GM_WIDTH)), const((1, GM_WIDTH)),
        const((seq, rows, GM_WIDTH)), const((rows, GM_WIDTH)),
        const((CONV_WIDTH, CONV_DIM)), const((1, CONV_DIM)),
        const((1, DT_PAD)), const((1, DT_PAD)), const((1, SSM_WIDTH)), const((1, SSM_WIDTH)),
        const((DT_PAD, SSM_WIDTH)),
    ]
    out_specs = [
        pl.BlockSpec((rows, MIX_WIDTH), lambda i: (i, 0)),
        pl.BlockSpec((bblk, SSM_WIDTH, SSM_STATE), lambda i: (i, 0, 0)),
        pl.BlockSpec((bblk, CONV_WIDTH - 1, CONV_DIM), lambda i: (i, 0, 0)),
        pl.BlockSpec((rows, GM_WIDTH), lambda i: (i, 0)),
    ]
    out_shape = [
        jax.ShapeDtypeStruct((nb * seq, MIX_WIDTH), BF16),
        jax.ShapeDtypeStruct((nb, SSM_WIDTH, SSM_STATE), F32),
        jax.ShapeDtypeStruct((nb, CONV_WIDTH - 1, CONV_DIM), F32),
        jax.ShapeDtypeStruct((nb * seq, GM_WIDTH), F32),
    ]
    scratch = [
        pltpu.VMEM((bblk, SUBLANES, CONV_DIM), F32),
        pltpu.VMEM((rows, CONV_DIM), F32),
        pltpu.VMEM((rows, MEM_WIDTH), F32),
        pltpu.VMEM((rows, SSM_WIDTH), F32),
        pltpu.VMEM((DT_PAD, LANES), F32),
    ]
    return pl.pallas_call(
        functools.partial(_sample_mix_kernel, bblk=bblk, seq=seq),
        grid=(nb // bblk,),
        in_specs=in_specs,
        out_specs=out_specs,
        out_shape=out_shape,
        scratch_shapes=scratch,
        compiler_params=pltpu.CompilerParams(dimension_semantics=("arbitrary",),
                                             vmem_limit_bytes=VMEM_LIMIT),
        name="sample_mix",
    )(proj, conv_state, ssm_state, mem_k, mem_v, gmg, gmb, gcoef, gbias, cw, cb, dtb, alog, dsk, sng, expand)


def _sample_out_kernel(mix_ref, x_ref, w_ref, g_ref, o_ref):
    out = x_ref[...] + _dot(mix_ref[...], w_ref[...])
    o_ref[...] = _rms(out, g_ref[...])


def _sample_out(mix, x, w_out, g):
    m = x.shape[0]
    full = lambda shape: pl.BlockSpec(shape, lambda i: (0,) * len(shape))
    return pl.pallas_call(
        _sample_out_kernel,
        grid=(1,),
        in_specs=[full((m, MIX_WIDTH)), full((m, D_MODEL)), full((MIX_WIDTH, D_MODEL)), full((1, D_MODEL))],
        out_specs=full((m, D_MODEL)),
        out_shape=jax.ShapeDtypeStruct((m, D_MODEL), F32),
        compiler_params=pltpu.CompilerParams(dimension_semantics=("arbitrary",),
                                             vmem_limit_bytes=VMEM_LIMIT),
        name="sample_out",
    )(mix, x, w_out, g)


def _pad_heads(a):
    return jnp.pad(a.astype(F32), (0, DT_PAD - SSM_HEADS)).reshape(1, DT_PAD)


def kernel(x_prompt, x_sample, mem_prompt, state_ssm, state_conv, cache_mem_k, cache_mem_v, norm_g, w_in,
           gm_norm_g, gm_norm_b, gm_w_spatial, gm_b_spatial, conv_w, conv_b, dt_bias, a_log, d_skip,
           ssm_norm_g, mem_norm_g, w_mem_k, w_mem_v, w_out, final_norm_g):
    assert norm_g.shape[0] == 1, "single layer"
    bp, seq_p, _ = x_prompt.shape
    bs, seq_s, _ = x_sample.shape
    row = lambda a: a.reshape(1, -1).astype(F32)

    w_in_t = jnp.swapaxes(w_in[0], 0, 1).astype(BF16)
    w_out_b = w_out[0].astype(BF16)
    ng, gmg, gmb = row(norm_g[0]), row(gm_norm_g[0]), row(gm_norm_b[0])
    cw, cb = conv_w[0].astype(F32), row(conv_b[0])
    dtb, alog = _pad_heads(dt_bias[0]), _pad_heads(a_log[0])
    dsk = row(jnp.repeat(d_skip[0], SSM_HEAD_DIM))
    sng, fg = row(ssm_norm_g[0]), row(final_norm_g)
    w_sp = gm_w_spatial[0]
    tril_p = jnp.tril(jnp.ones((CHUNK, CHUNK), bool))
    ws_p = jnp.where(tril_p, w_sp, 0).astype(BF16)
    bsf_p = jnp.repeat(gm_b_spatial[0].T, GM_HEAD_DIM, axis=1).astype(F32)

    mk, mv, mkb, mvb = _memory_kv(mem_prompt, row(mem_norm_g[0]), w_mem_k[0].astype(BF16),
                                  w_mem_v[0].astype(BF16))
    y_p, ssm_p, conv_p = _prompt_layer(x_prompt, mkb, mvb, w_in_t, w_out_b, ng, gmg, gmb, ws_p, bsf_p,
                                       cw, cb, dtb, alog, dsk, sng, fg)

    rows = SAMPLE_BATCH_BLOCK * seq_s
    tpos = jnp.arange(rows) % seq_s
    gcoef = jnp.stack([
        jnp.where((tpos >= j)[:, None],
                  jnp.repeat(w_sp[:, tpos, jnp.maximum(tpos - j, 0)].T, GM_HEAD_DIM, axis=1), 0.0)
        for j in range(seq_s)]).astype(F32)
    gbias = jnp.repeat(gm_b_spatial[0][:, tpos].T, GM_HEAD_DIM, axis=1).astype(F32)
    expand = (jnp.arange(DT_PAD)[:, None] == (jnp.arange(SSM_WIDTH) // SSM_HEAD_DIM)[None, :]).astype(BF16)

    xs2 = x_sample.reshape(bs * seq_s, D_MODEL)
    proj_s = _sample_proj(xs2, ng, w_in_t)
    mix_s, ssm_s, conv_s, gv_s = _sample_mix(
        proj_s, state_conv[0], state_ssm[0].reshape(bs, SSM_WIDTH, SSM_STATE),
        cache_mem_k, cache_mem_v,
        gmg, gmb, gcoef, gbias, cw, cb, dtb, alog, dsk, sng, expand, seq_s)
    y_s = _sample_out(mix_s, xs2, w_out_b, fg)

    return (y_p,
            y_s.reshape(bs, seq_s, D_MODEL),
            ssm_p.reshape(1, bp, SSM_HEADS, SSM_HEAD_DIM, SSM_STATE),
            conv_p[None],
            mk,
            mv,
            ssm_s.reshape(1, bs, SSM_HEADS, SSM_HEAD_DIM, SSM_STATE),
            conv_s[None],
            gv_s.reshape(1, bs, seq_s, GM_WIDTH))
```

```python
import functools
import math

import jax
import jax.numpy as jnp
from jax import lax
from jax.experimental import pallas as pl
from jax.experimental.pallas import tpu as pltpu

F32 = jnp.float32
BF16 = jnp.bfloat16

D_MODEL = 1024
GM_WIDTH = 1024
GM_HEADS = 8
GM_HEAD_DIM = 128
CHUNK = 128
SSM_WIDTH = 1024
SSM_HEADS = 16
SSM_HEAD_DIM = 64
SSM_GROUPS = 2
SSM_STATE = 128
HEADS_PER_GROUP = SSM_HEADS // SSM_GROUPS
GROUP_WIDTH = SSM_WIDTH // SSM_GROUPS
CONV_WIDTH = 4
CONV_DIM = SSM_WIDTH + 2 * SSM_GROUPS * SSM_STATE
MEM_LEN = 256
MEM_HEADS = 4
MEM_HEAD_DIM = 256
MEM_WIDTH = 1024
MIX_WIDTH = GM_WIDTH + SSM_WIDTH + MEM_WIDTH
EPS = 1e-6

LANES = 128
SUBLANES = 8

DT_PAD = LANES
OFF_U = 0
OFF_V = OFF_U + GM_WIDTH
OFF_GATE = OFF_V + GM_WIDTH
OFF_Z = OFF_GATE + GM_WIDTH
OFF_XBC = OFF_Z + SSM_WIDTH
OFF_DT = OFF_XBC + CONV_DIM
OFF_Q = OFF_DT + SSM_HEADS
OFF_MGATE = OFF_Q + MEM_WIDTH
IN_WIDTH = OFF_MGATE + MEM_WIDTH
PROJ_BLOCK = 1024

PROMPT_TILE = 256
TILES_PER_STEP = 2
SAMPLE_BATCH_BLOCK = 4
VMEM_LIMIT = 56 * 1024 * 1024


def _rms(x, g):
    return x * lax.rsqrt(jnp.mean(x * x, axis=-1, keepdims=True) + EPS) * g


def _gelu(x):
    return 0.5 * x * (1.0 + lax.erf(x * math.sqrt(0.5)))


def _silu(x):
    return x * jax.nn.sigmoid(x)


def _softplus(x):
    return jnp.maximum(x, 0.0) + jnp.log1p(jnp.exp(-jnp.abs(x)))


def _layernorm(x, g, b):
    mu = jnp.mean(x, axis=-1, keepdims=True)
    xc = x - mu
    var = jnp.mean(xc * xc, axis=-1, keepdims=True)
    return xc * lax.rsqrt(var + EPS) * g + b


def _dot(a, b):
    return jnp.dot(a, b, preferred_element_type=F32)


def _dot_nt(a, b):
    return lax.dot_general(a, b, (((1,), (1,)), ((), ())), preferred_element_type=F32)


def _dot_tn(a, b):
    return lax.dot_general(a, b, (((0,), (0,)), ((), ())), preferred_element_type=F32)


def _group_rmsnorm(y, g):
    halves = []
    for i in range(SSM_GROUPS):
        yg = y[:, i * GROUP_WIDTH:(i + 1) * GROUP_WIDTH]
        halves.append(yg * lax.rsqrt(jnp.mean(yg * yg, axis=-1, keepdims=True) + EPS))
    return jnp.concatenate(halves, axis=-1) * g


def _memory_kv_kernel(mem_ref, g_ref, wk_ref, wv_ref, k_ref, v_ref, kb_ref, vb_ref):
    m = _rms(mem_ref[0], g_ref[...]).astype(BF16)
    k = _dot(m, wk_ref[...])
    v = _dot(m, wv_ref[...])
    for h in range(MEM_HEADS):
        k_ref[0, 0, :, h, :] = k[:, _head_slice(h)]
        v_ref[0, 0, :, h, :] = v[:, _head_slice(h)]
    kb_ref[0] = k.astype(BF16)
    vb_ref[0] = v.astype(BF16)


def _memory_kv(mem, g, wk, wv):
    b = mem.shape[0]
    blk = pl.BlockSpec((1, MEM_LEN, D_MODEL), lambda i: (i, 0, 0))
    blk5 = pl.BlockSpec((1, 1, MEM_LEN, MEM_HEADS, MEM_HEAD_DIM), lambda i: (0, i, 0, 0, 0))
    const = lambda shape: pl.BlockSpec(shape, lambda i: (0,) * len(shape))
    return pl.pallas_call(
        _memory_kv_kernel,
        grid=(b,),
        in_specs=[blk, const((1, D_MODEL)), const((D_MODEL, MEM_WIDTH)), const((D_MODEL, MEM_WIDTH))],
        out_specs=[blk5, blk5, blk, blk],
        out_shape=[jax.ShapeDtypeStruct((1, b, MEM_LEN, MEM_HEADS, MEM_HEAD_DIM), F32)] * 2
        + [jax.ShapeDtypeStruct((b, MEM_LEN, MEM_WIDTH), BF16)] * 2,
        compiler_params=pltpu.CompilerParams(dimension_semantics=("arbitrary",),
                                             vmem_limit_bytes=VMEM_LIMIT),
        name="memory_kv",
    )(mem, g, wk, wv)


def _head_slice(h):
    return slice(h * MEM_HEAD_DIM, (h + 1) * MEM_HEAD_DIM)


def _attention(q, k_head, v_head):
    qb = (q * (MEM_HEAD_DIM ** -0.5)).astype(BF16)
    outs = []
    for h in range(MEM_HEADS):
        s = _dot_nt(qb[:, _head_slice(h)], k_head(h))
        p = jnp.exp(s - jnp.max(s, axis=-1, keepdims=True))
        denom = jnp.sum(p, axis=-1, keepdims=True)
        outs.append(_dot(p.astype(BF16), v_head(h)) / denom)
    return jnp.concatenate(outs, axis=-1)


def _attention_interleaved(q, k2, v2):
    t = q.shape[0]
    qb = (q * (MEM_HEAD_DIM ** -0.5)).astype(BF16)
    q2 = jnp.concatenate([qb[:, _head_slice(h)] for h in range(MEM_HEADS)], axis=0)
    s = _dot_nt(q2, k2)
    row_head = lax.broadcasted_iota(jnp.int32, s.shape, 0) // t
    col_head = lax.broadcasted_iota(jnp.int32, s.shape, 1) % MEM_HEADS
    s = jnp.where(row_head == col_head, s, -jnp.inf)
    p = jnp.exp(s - jnp.max(s, axis=-1, keepdims=True))
    denom = jnp.sum(p, axis=-1, keepdims=True)
    return _dot(p.astype(BF16), v2) / denom


class _Handoff:
    N = 8

    def __init__(self, refs):
        (self.mix, self.ug, self.v, self.zs, self.xs, self.bm, self.cm, self.dt) = refs


def _prompt_step(x_ref, kb_ref, vb_ref, win_ref, wout_ref, ng_ref, gmg_ref, gmb_ref, ws_ref, bsf_ref,
                 cw_ref, cb_ref, dtb_ref, alog_ref, dsk_ref, sng_ref, fg_ref, y_ref,
                 hn_sc, xp_sc, ht_sc, ssd_sc, out_sc, yd_sc, st_sc, ea_sc, new, old, tile, rows):
    a_row = -jnp.exp(alog_ref[...])
    row = lax.broadcasted_iota(jnp.int32, (CHUNK, CHUNK), 0)
    col = lax.broadcasted_iota(jnp.int32, (CHUNK, CHUNK), 1)
    causal = row >= col
    tril_b = causal.astype(BF16)
    head_lane = col < SSM_HEADS
    first_half = col < SSM_HEAD_DIM
    def tables(c, g=0):
        base = c * 8
        return (ssd_sc.at[base], ssd_sc.at[base + 1], ssd_sc.at[base + 2], ssd_sc.at[base + 3],
                ssd_sc.at[base + 4 + 2 * g], ssd_sc.at[base + 5 + 2 * g])

    def proj(off, width):
        return _dot_nt(hn_sc[...], win_ref[off:off + width, :])

    def gmlp_mix(c):
        r = slice(c * CHUNK, (c + 1) * CHUNK)
        for h in range(GM_HEADS):
            hs = slice(h * GM_HEAD_DIM, (h + 1) * GM_HEAD_DIM)
            mixed = _dot(ws_ref[h], old.v[r, hs]) + bsf_ref[:, hs]
            old.mix[r, hs] = (old.ug[r, hs] * mixed).astype(BF16)

    def seq_chunk_start(c):
        r = slice(c * CHUNK, (c + 1) * CHUNK)
        acum_sc, acum_t_sc, dt_t_sc, to_end_t_sc, _, _ = tables(c)
        dt = old.dt[r, :]
        adt = jnp.where(head_lane, dt * a_row, 0.0)
        hi = adt.astype(BF16)
        rest = adt - hi.astype(F32)
        mid = rest.astype(BF16)
        lo = (rest - mid.astype(F32)).astype(BF16)
        acum = _dot(tril_b, hi) + _dot(tril_b, mid) + _dot(tril_b, lo)
        acum_t = acum.T
        dt_t = dt.T
        acum_sc[...] = acum
        acum_t_sc[...] = acum_t
        dt_t_sc[...] = dt_t
        to_end_t_sc[...] = jnp.exp(acum_t[:, CHUNK - 1:CHUNK] - acum_t) * dt_t
        for g in range(SSM_GROUPS):
            gs = slice(g * SSM_STATE, (g + 1) * SSM_STATE)
            _, _, _, _, cb_sc, bm_t_sc = tables(c, g)
            bm_g = old.bm[r, gs]
            cb_sc[...] = _dot_nt(old.cm[r, gs].astype(BF16), bm_g.astype(BF16))
            bm_t_sc[...] = bm_g.T

    def seq_group(c, g):
        r = slice(c * CHUNK, (c + 1) * CHUNK)
        acum_sc, acum_t_sc, dt_t_sc, to_end_t_sc, cb_sc, bm_t_sc = tables(c, g)
        for kk in range(0, HEADS_PER_GROUP, 2):
            k0 = g * HEADS_PER_GROUP + kk
            ps = slice(k0 * SSM_HEAD_DIM, (k0 + 2) * SSM_HEAD_DIM)
            m_parts, b_parts, ea = [], [], []
            for k in (k0, k0 + 1):
                a_col = jnp.broadcast_to(acum_sc[:, k:k + 1], (CHUNK, CHUNK))
                diff = a_col - acum_t_sc[k:k + 1, :]
                decay = jnp.exp(jnp.where(causal, diff, -jnp.inf))
                m_parts.append((cb_sc[...] * decay * dt_t_sc[k:k + 1, :]).astype(BF16))
                b_parts.append((bm_t_sc[...] * to_end_t_sc[k:k + 1, :]).astype(BF16))
                ea.append(jnp.exp(a_col))
            xs_pair = old.xs[r, ps]
            zero = jnp.zeros_like(xs_pair)
            xs_lo = jnp.where(first_half, xs_pair, zero).astype(BF16)
            xs_hi = jnp.where(first_half, zero, xs_pair).astype(BF16)
            lhs = jnp.concatenate([jnp.concatenate(m_parts, axis=1), jnp.concatenate(b_parts, axis=1)], axis=0)
            both = _dot(lhs, jnp.concatenate([xs_lo, xs_hi], axis=0))
            yd_sc[c, :, ps] = both[0:CHUNK, :]
            st_sc[c, :, ps] = both[CHUNK:2 * CHUNK, :]
            ea_sc[c, :, ps] = jnp.where(first_half, ea[0], ea[1])

    def seq_chunk_end(c):
        r = slice(c * CHUNK, (c + 1) * CHUNK)
        h_b = ht_sc[...].astype(BF16)
        y_off = jnp.concatenate(
            [_dot(old.cm[r, g * SSM_STATE:(g + 1) * SSM_STATE].astype(BF16), h_b[:, g * GROUP_WIDTH:(g + 1) * GROUP_WIDTH])
             for g in range(SSM_GROUPS)], axis=1)
        ea_c = ea_sc[c]
        y = yd_sc[c] + y_off * ea_c
        ht_sc[...] = ht_sc[...] * ea_c[CHUNK - 1:CHUNK, :] + st_sc[c]
        y = (y + dsk_ref[...] * old.xs[r, :]) * old.zs[r, :]
        old.mix[r, GM_WIDTH:GM_WIDTH + SSM_WIDTH] = _group_rmsnorm(y, sng_ref[...]).astype(BF16)

    def out_piece(j, n):
        cs = slice(j * D_MODEL // n, (j + 1) * D_MODEL // n)
        out_sc[:, cs] = x_ref[0, rows, cs] + _dot(old.mix[...], wout_ref[:, cs])

    assert tile // CHUNK == 2, "the emission order below is written for two chunks per tile"

    hn_sc[...] = _rms(x_ref[0, rows, :], ng_ref[...]).astype(BF16)

    new.zs[...] = _silu(proj(OFF_Z, SSM_WIDTH))
    xp_sc[SUBLANES:SUBLANES + tile, :] = proj(OFF_XBC, CONV_DIM)
    ext = xp_sc[...].reshape(tile // SUBLANES + 1, SUBLANES, CONV_DIM)
    sub = lax.broadcasted_iota(jnp.int32, (1, SUBLANES, CONV_DIM), 1)
    acc = cb_ref[...] + cw_ref[CONV_WIDTH - 1:CONV_WIDTH, :] * ext[1:]
    for shift in range(1, CONV_WIDTH):
        rot = pltpu.roll(ext, shift, 1)
        shifted = jnp.where(sub >= shift, rot[1:], rot[:-1])
        acc = acc + cw_ref[CONV_WIDTH - 1 - shift:CONV_WIDTH - shift, :] * shifted
    acc = acc.reshape(tile, CONV_DIM)
    xbc = _silu(acc)
    new.xs[...] = xbc[:, :SSM_WIDTH]
    new.bm[...] = xbc[:, SSM_WIDTH:SSM_WIDTH + SSM_GROUPS * SSM_STATE]
    new.cm[...] = xbc[:, SSM_WIDTH + SSM_GROUPS * SSM_STATE:]
    xp_sc[SUBLANES - (CONV_WIDTH - 1):SUBLANES, :] = xp_sc[tile + SUBLANES - (CONV_WIDTH - 1):tile + SUBLANES, :]
    new.dt[...] = _softplus(proj(OFF_DT, DT_PAD) + dtb_ref[...])
    u = _gelu(proj(OFF_U, GM_WIDTH))
    new.ug[...] = u * _silu(proj(OFF_GATE, GM_WIDTH))
    new.v[...] = _layernorm(_gelu(proj(OFF_V, GM_WIDTH)), gmg_ref[...], gmb_ref[...]).astype(BF16)
    seq_chunk_start(0)
    seq_chunk_start(1)
    gmlp_mix(0)
    gmlp_mix(1)
    seq_group(0, 0)
    seq_group(0, 1)
    seq_chunk_end(0)
    att = _attention(proj(OFF_Q, MEM_WIDTH), lambda h: kb_ref[0, :, _head_slice(h)], lambda h: vb_ref[0, :, _head_slice(h)])
    new.mix[:, GM_WIDTH + SSM_WIDTH:MIX_WIDTH] = (att * _silu(proj(OFF_MGATE, MEM_WIDTH))).astype(BF16)
    seq_group(1, 0)
    seq_group(1, 1)
    seq_chunk_end(1)

    n_out = 4
    for j in range(n_out):
        out_piece(j, n_out)
    y_ref[0, rows, :] = _rms(out_sc[...], fg_ref[...])


def _prompt_kernel(x_ref, kb_ref, vb_ref, win_ref, wout_ref, ng_ref, gmg_ref, gmb_ref,
                   ws_ref, bsf_ref, cw_ref, cb_ref, dtb_ref, alog_ref, dsk_ref, sng_ref, fg_ref,
                   y_ref, ssm_ref, conv_ref, hn_sc, xpad_sc, ht_sc, ssd_sc, out_sc, yd_sc, st_sc, ea_sc, *handoff, tile, nt, total):
    t = pl.program_id(0) % nt
    tile_set = _Handoff(handoff)

    @pl.when(t == 0)
    def _():
        xpad_sc[0:SUBLANES, :] = jnp.zeros((SUBLANES, CONV_DIM), F32)
        ht_sc[...] = jnp.zeros_like(ht_sc)

    for i in range(TILES_PER_STEP):
        _prompt_step(x_ref, kb_ref, vb_ref, win_ref, wout_ref, ng_ref, gmg_ref, gmb_ref, ws_ref, bsf_ref,
                     cw_ref, cb_ref, dtb_ref, alog_ref, dsk_ref, sng_ref, fg_ref, y_ref,
                     hn_sc, xpad_sc, ht_sc, ssd_sc, out_sc, yd_sc, st_sc, ea_sc, tile_set, tile_set, tile,
                     slice(i * tile, (i + 1) * tile))

    @pl.when(t == nt - 1)
    def _():
        conv_ref[0] = xpad_sc[SUBLANES - (CONV_WIDTH - 1):SUBLANES, :]
        ssm_ref[0] = ht_sc[...].T


def _prompt_layer(x, kb, vb, w_in_t, w_out, ng, gmg, gmb, ws, bsf, cw, cb, dtb, alog, dsk, sng, fg):
    b, seq, _ = x.shape
    tile = PROMPT_TILE
    step_rows = tile * TILES_PER_STEP
    nt = seq // step_rows
    total = b * nt
    assert seq % step_rows == 0 and tile % CHUNK == 0

    def const(shape):
        return pl.BlockSpec(shape, lambda s: (0,) * len(shape), pipeline_mode=pl.Buffered(1))

    in_specs = [
        pl.BlockSpec((1, step_rows, D_MODEL), lambda s: (s // nt, s % nt, 0)),
        pl.BlockSpec((1, MEM_LEN, MEM_WIDTH), lambda s: (s // nt, 0, 0)),
        pl.BlockSpec((1, MEM_LEN, MEM_WIDTH), lambda s: (s // nt, 0, 0)),
        const((IN_WIDTH, D_MODEL)),
        const((MIX_WIDTH, D_MODEL)),
        const((1, D_MODEL)), const((1, GM_WIDTH)), const((1, GM_WIDTH)),
        const((GM_HEADS, CHUNK, CHUNK)), const((CHUNK, GM_WIDTH)),
        const((CONV_WIDTH, CONV_DIM)), const((1, CONV_DIM)),
        const((1, DT_PAD)), const((1, DT_PAD)), const((1, SSM_WIDTH)), const((1, SSM_WIDTH)),
        const((1, D_MODEL)),
    ]
    out_specs = [
        pl.BlockSpec((1, step_rows, D_MODEL), lambda s: (s // nt, s % nt, 0)),
        pl.BlockSpec((1, SSM_WIDTH, SSM_STATE), lambda s: (s // nt, 0, 0)),
        pl.BlockSpec((1, CONV_WIDTH - 1, CONV_DIM), lambda s: (s // nt, 0, 0)),
    ]
    out_shape = [
        jax.ShapeDtypeStruct((b, seq, D_MODEL), F32),
        jax.ShapeDtypeStruct((b, SSM_WIDTH, SSM_STATE), F32),
        jax.ShapeDtypeStruct((b, CONV_WIDTH - 1, CONV_DIM), F32),
    ]
    handoff = [
        pltpu.VMEM((tile, MIX_WIDTH), BF16),
        pltpu.VMEM((tile, GM_WIDTH), F32),
        pltpu.VMEM((tile, GM_WIDTH), BF16),
        pltpu.VMEM((tile, SSM_WIDTH), F32),
        pltpu.VMEM((tile, SSM_WIDTH), F32),
        pltpu.VMEM((tile, SSM_GROUPS * SSM_STATE), F32),
        pltpu.VMEM((tile, SSM_GROUPS * SSM_STATE), F32),
        pltpu.VMEM((tile, DT_PAD), F32),
    ]
    assert len(handoff) == _Handoff.N
    scratch = [
        pltpu.VMEM((tile, D_MODEL), BF16),
        pltpu.VMEM((tile + SUBLANES, CONV_DIM), F32),
        pltpu.VMEM((SSM_STATE, SSM_WIDTH), F32),
        pltpu.VMEM((16, CHUNK, CHUNK), F32),
        pltpu.VMEM((tile, D_MODEL), F32),
        pltpu.VMEM((tile // CHUNK, CHUNK, SSM_WIDTH), F32),
        pltpu.VMEM((tile // CHUNK, SSM_STATE, SSM_WIDTH), F32),
        pltpu.VMEM((tile // CHUNK, CHUNK, SSM_WIDTH), F32),
    ] + handoff
    return pl.pallas_call(
        functools.partial(_prompt_kernel, tile=tile, nt=nt, total=total),
        grid=(total,),
        in_specs=in_specs,
        out_specs=out_specs,
        out_shape=out_shape,
        scratch_shapes=scratch,
        compiler_params=pltpu.CompilerParams(dimension_semantics=("arbitrary",),
                                             vmem_limit_bytes=VMEM_LIMIT),
        name="prompt_layer",
    )(x, kb, vb, w_in_t, w_out, ng, gmg, gmb, ws, bsf, cw, cb, dtb, alog, dsk, sng, fg)


def _sample_proj_kernel(x_ref, g_ref, w_ref, o_ref):
    hn = _rms(x_ref[...], g_ref[...]).astype(BF16)
    o_ref[...] = _dot_nt(hn, w_ref[...])


def _sample_proj(x, g, w_t):
    m, n = x.shape[0], w_t.shape[0]
    return pl.pallas_call(
        _sample_proj_kernel,
        grid=(pl.cdiv(n, PROJ_BLOCK),),
        in_specs=[pl.BlockSpec((m, D_MODEL), lambda j: (0, 0)),
                  pl.BlockSpec((1, D_MODEL), lambda j: (0, 0)),
                  pl.BlockSpec((PROJ_BLOCK, D_MODEL), lambda j: (j, 0))],
        out_specs=pl.BlockSpec((m, PROJ_BLOCK), lambda j: (0, j)),
        out_shape=jax.ShapeDtypeStruct((m, n), F32),
        compiler_params=pltpu.CompilerParams(dimension_semantics=("arbitrary",),
                                             vmem_limit_bytes=VMEM_LIMIT),
        name="sample_proj",
    )(x, g, w_t)


def _sample_mix_kernel(p_ref, cst_ref, ssm_ref, k_ref, v_ref, gmg_ref, gmb_ref, gcoef_ref, gbias_ref,
                       cw_ref, cb_ref, dtb_ref, alog_ref, dsk_ref, sng_ref, expand_ref,
                       mix_ref, ssm_out_ref, conv_out_ref, gv_ref,
                       xp_sc, xbc_sc, att_sc, yoff_sc, dec_sc, *, bblk, seq):
    rows = bblk * seq

    def col(off, width):
        return p_ref[:, off:off + width]

    tpos = lax.broadcasted_iota(jnp.int32, (rows, 1), 0) % seq

    def back(a, j):
        return a if j == 0 else pltpu.roll(a, j, 0)

    u = _gelu(col(OFF_U, GM_WIDTH))
    v = _layernorm(_gelu(col(OFF_V, GM_WIDTH)), gmg_ref[...], gmb_ref[...])
    gv_ref[...] = v
    mixed = gbias_ref[...]
    for j in range(seq):
        mixed = mixed + gcoef_ref[j] * back(v, j)
    mix_ref[:, 0:GM_WIDTH] = (u * mixed * _silu(col(OFF_GATE, GM_WIDTH))).astype(BF16)

    xbc_raw = col(OFF_XBC, CONV_DIM)
    for b in range(bblk):
        xp_sc[b, 0:CONV_WIDTH - 1, :] = cst_ref[b]
        xp_sc[b, CONV_WIDTH - 1:CONV_WIDTH - 1 + seq, :] = xbc_raw[b * seq:(b + 1) * seq, :]
    for b in range(bblk):
        acc = jnp.broadcast_to(cb_ref[...], (seq, CONV_DIM))
        for j in range(CONV_WIDTH):
            acc = acc + cw_ref[j:j + 1, :] * xp_sc[b, j:j + seq, :]
        xbc_sc[b * seq:(b + 1) * seq, :] = _silu(acc)
        conv_out_ref[b] = xp_sc[b, seq:seq + CONV_WIDTH - 1, :]
    xs = xbc_sc[:, 0:SSM_WIDTH]
    bm = xbc_sc[:, SSM_WIDTH:SSM_WIDTH + SSM_GROUPS * SSM_STATE]
    cm = xbc_sc[:, SSM_WIDTH + SSM_GROUPS * SSM_STATE:CONV_DIM]

    lane = lax.broadcasted_iota(jnp.int32, (rows, DT_PAD), 1)
    dt = _softplus(col(OFF_DT, DT_PAD) + dtb_ref[...])
    adt = jnp.where(lane < SSM_HEADS, dt * (-jnp.exp(alog_ref[...])), 0.0)
    acum = adt
    for j in range(1, seq):
        acum = acum + jnp.where(tpos >= j, back(adt, j), 0.0)
    a_last = jnp.zeros_like(acum)
    for j in range(seq):
        a_last = a_last + jnp.where(tpos == seq - 1 - j, acum if j == 0 else pltpu.roll(acum, rows - j, 0), 0.0)
    coefs = []
    for j in range(seq):
        cbj = []
        for g in range(SSM_GROUPS):
            gs = slice(g * SSM_STATE, (g + 1) * SSM_STATE)
            cbj.append(jnp.sum(cm[:, gs] * back(bm[:, gs], j), axis=-1, keepdims=True))
        cb_l = jnp.where(lane < HEADS_PER_GROUP, cbj[0], cbj[1])
        valid = tpos >= j
        decay = jnp.exp(jnp.where(valid, acum - back(acum, j), 0.0))
        coefs.append(jnp.where(valid, cb_l * decay * back(dt, j), 0.0))
    coefs.append(jnp.exp(acum))
    coefs.append(dt * jnp.exp(a_last - acum))
    stack = jnp.concatenate(coefs, axis=0)
    hi = stack.astype(BF16)
    lo = (stack - hi.astype(F32)).astype(BF16)
    wide = _dot(hi, expand_ref[...]) + _dot(lo, expand_ref[...])
    y = dsk_ref[...] * xs
    for j in range(seq):
        y = y + wide[j * rows:(j + 1) * rows, :] * back(xs, j)
    e_wide = wide[seq * rows:(seq + 1) * rows, :]
    wx = (xs * wide[(seq + 1) * rows:(seq + 2) * rows, :]).astype(BF16)
    dec_rows = jnp.exp(a_last)

    q = col(OFF_Q, MEM_WIDTH)
    cmb = cm.astype(BF16)
    bmb = bm.astype(BF16)
    for b in range(bblk):
        rs = slice(b * seq, (b + 1) * seq)
        att = _attention_interleaved(
            q[rs, :],
            k_ref[0, b].reshape(MEM_LEN * MEM_HEADS, MEM_HEAD_DIM).astype(BF16),
            v_ref[0, b].reshape(MEM_LEN * MEM_HEADS, MEM_HEAD_DIM).astype(BF16))
        for h in range(MEM_HEADS):
            att_sc[rs, _head_slice(h)] = att[h * seq:(h + 1) * seq, :]
        h0 = ssm_ref[b]
        h0b = h0.astype(BF16)
        dec_sc[...] = jnp.broadcast_to(dec_rows[b * seq + seq - 1:b * seq + seq, :], (LANES, DT_PAD)).T
        for g in range(SSM_GROUPS):
            gs = slice(g * SSM_STATE, (g + 1) * SSM_STATE)
            ws_ = slice(g * GROUP_WIDTH, (g + 1) * GROUP_WIDTH)
            yoff_sc[rs, ws_] = _dot_nt(cmb[rs, gs], h0b[ws_, :])
            upd = _dot_tn(wx[rs, ws_], bmb[rs, gs])
            for kk in range(HEADS_PER_GROUP):
                k = g * HEADS_PER_GROUP + kk
                hs = slice(k * SSM_HEAD_DIM, (k + 1) * SSM_HEAD_DIM)
                ssm_out_ref[b, hs, :] = (h0[hs, :] * dec_sc[k:k + 1, :]
                                         + upd[kk * SSM_HEAD_DIM:(kk + 1) * SSM_HEAD_DIM, :])

    y = (y + yoff_sc[...] * e_wide) * _silu(col(OFF_Z, SSM_WIDTH))
    mix_ref[:, GM_WIDTH:GM_WIDTH + SSM_WIDTH] = _group_rmsnorm(y, sng_ref[...]).astype(BF16)
    mix_ref[:, GM_WIDTH + SSM_WIDTH:MIX_WIDTH] = (att_sc[...] * _silu(col(OFF_MGATE, MEM_WIDTH))).astype(BF16)


def _sample_mix(proj, conv_state, ssm_state, mem_k, mem_v, gmg, gmb, gcoef, gbias, cw, cb, dtb, alog,
                dsk, sng, expand, seq):
    nb = conv_state.shape[0]
    bblk = SAMPLE_BATCH_BLOCK
    rows = bblk * seq
    assert nb % bblk == 0 and rows % (2 * SUBLANES) == 0

    def const(shape):
        return pl.BlockSpec(shape, lambda i: (0,) * len(shape))

    in_specs = [
        pl.BlockSpec((rows, IN_WIDTH), lambda i: (i, 0)),
        pl.BlockSpec((bblk, CONV_WIDTH - 1, CONV_DIM), lambda i: (i, 0, 0)),
        pl.BlockSpec((bblk, SSM_WIDTH, SSM_STATE), lambda i: (i, 0, 0)),
        pl.BlockSpec((1, bblk, MEM_LEN, MEM_HEADS, MEM_HEAD_DIM), lambda i: (0, i, 0, 0, 0)),
        pl.BlockSpec((1, bblk, MEM_LEN, MEM_HEADS, MEM_HEAD_DIM), lambda i: (0, i, 0, 0, 0)),
        const((1, GM_WIDTH)), const((1, GM_WIDTH)),
        const((seq, rows, GM_WIDTH)), const((rows, GM_WIDTH)),
        const((CONV_WIDTH, CONV_DIM)), const((1, CONV_DIM)),
        const((1, DT_PAD)), const((1, DT_PAD)), const((1, SSM_WIDTH)), const((1, SSM_WIDTH)),
        const((DT_PAD, SSM_WIDTH)),
    ]
    out_specs = [
        pl.BlockSpec((rows, MIX_WIDTH), lambda i: (i, 0)),
        pl.BlockSpec((bblk, SSM_WIDTH, SSM_STATE), lambda i: (i, 0, 0)),
        pl.BlockSpec((bblk, CONV_WIDTH - 1, CONV_DIM), lambda i: (i, 0, 0)),
        pl.BlockSpec((rows, GM_WIDTH), lambda i: (i, 0)),
    ]
    out_shape = [
        jax.ShapeDtypeStruct((nb * seq, MIX_WIDTH), BF16),
        jax.ShapeDtypeStruct((nb, SSM_WIDTH, SSM_STATE), F32),
        jax.ShapeDtypeStruct((nb, CONV_WIDTH - 1, CONV_DIM), F32),
        jax.ShapeDtypeStruct((nb * seq, GM_WIDTH), F32),
    ]
    scratch = [
        pltpu.VMEM((bblk, SUBLANES, CONV_DIM), F32),
        pltpu.VMEM((rows, CONV_DIM), F32),
        pltpu.VMEM((rows, MEM_WIDTH), F32),
        pltpu.VMEM((rows, SSM_WIDTH), F32),
        pltpu.VMEM((DT_PAD, LANES), F32),
    ]
    return pl.pallas_call(
        functools.partial(_sample_mix_kernel, bblk=bblk, seq=seq),
        grid=(nb // bblk,),
        in_specs=in_specs,
        out_specs=out_specs,
        out_shape=out_shape,
        scratch_shapes=scratch,
        compiler_params=pltpu.CompilerParams(dimension_semantics=("arbitrary",),
                                             vmem_limit_bytes=VMEM_LIMIT),
        name="sample_mix",
    )(proj, conv_state, ssm_state, mem_k, mem_v, gmg, gmb, gcoef, gbias, cw, cb, dtb, alog, dsk, sng, expand)


def _sample_out_kernel(mix_ref, x_ref, w_ref, g_ref, o_ref):
    out = x_ref[...] + _dot(mix_ref[...], w_ref[...])
    o_ref[...] = _rms(out, g_ref[...])


def _sample_out(mix, x, w_out, g):
    m = x.shape[0]
    full = lambda shape: pl.BlockSpec(shape, lambda i: (0,) * len(shape))
    return pl.pallas_call(
        _sample_out_kernel,
        grid=(1,),
        in_specs=[full((m, MIX_WIDTH)), full((m, D_MODEL)), full((MIX_WIDTH, D_MODEL)), full((1, D_MODEL))],
        out_specs=full((m, D_MODEL)),
        out_shape=jax.ShapeDtypeStruct((m, D_MODEL), F32),
        compiler_params=pltpu.CompilerParams(dimension_semantics=("arbitrary",),
                                             vmem_limit_bytes=VMEM_LIMIT),
        name="sample_out",
    )(mix, x, w_out, g)


def _pad_heads(a):
    return jnp.pad(a.astype(F32), (0, DT_PAD - SSM_HEADS)).reshape(1, DT_PAD)


def kernel(x_prompt, x_sample, mem_prompt, state_ssm, state_conv, cache_mem_k, cache_mem_v, norm_g, w_in,
           gm_norm_g, gm_norm_b, gm_w_spatial, gm_b_spatial, conv_w, conv_b, dt_bias, a_log, d_skip,
           ssm_norm_g, mem_norm_g, w_mem_k, w_mem_v, w_out, final_norm_g):
    assert norm_g.shape[0] == 1, "single layer"
    bp, seq_p, _ = x_prompt.shape
    bs, seq_s, _ = x_sample.shape
    row = lambda a: a.reshape(1, -1).astype(F32)

    w_in_t = jnp.swapaxes(w_in[0], 0, 1).astype(BF16)
    w_out_b = w_out[0].astype(BF16)
    ng, gmg, gmb = row(norm_g[0]), row(gm_norm_g[0]), row(gm_norm_b[0])
    cw, cb = conv_w[0].astype(F32), row(conv_b[0])
    dtb, alog = _pad_heads(dt_bias[0]), _pad_heads(a_log[0])
    dsk = row(jnp.repeat(d_skip[0], SSM_HEAD_DIM))
    sng, fg = row(ssm_norm_g[0]), row(final_norm_g)
    w_sp = gm_w_spatial[0]
    tril_p = jnp.tril(jnp.ones((CHUNK, CHUNK), bool))
    ws_p = jnp.where(tril_p, w_sp, 0).astype(BF16)
    bsf_p = jnp.repeat(gm_b_spatial[0].T, GM_HEAD_DIM, axis=1).astype(F32)

    mk, mv, mkb, mvb = _memory_kv(mem_prompt, row(mem_norm_g[0]), w_mem_k[0].astype(BF16),
                                  w_mem_v[0].astype(BF16))
    y_p, ssm_p, conv_p = _prompt_layer(x_prompt, mkb, mvb, w_in_t, w_out_b, ng, gmg, gmb, ws_p, bsf_p,
                                       cw, cb, dtb, alog, dsk, sng, fg)

    rows = SAMPLE_BATCH_BLOCK * seq_s
    tpos = jnp.arange(rows) % seq_s
    gcoef = jnp.stack([
        jnp.where((tpos >= j)[:, None],
                  jnp.repeat(w_sp[:, tpos, jnp.maximum(tpos - j, 0)].T, GM_HEAD_DIM, axis=1), 0.0)
        for j in range(seq_s)]).astype(F32)
    gbias = jnp.repeat(gm_b_spatial[0][:, tpos].T, GM_HEAD_DIM, axis=1).astype(F32)
    expand = (jnp.arange(DT_PAD)[:, None] == (jnp.arange(SSM_WIDTH) // SSM_HEAD_DIM)[None, :]).astype(BF16)

    xs2 = x_sample.reshape(bs * seq_s, D_MODEL)
    proj_s = _sample_proj(xs2, ng, w_in_t)
    mix_s, ssm_s, conv_s, gv_s = _sample_mix(
        proj_s, state_conv[0], state_ssm[0].reshape(bs, SSM_WIDTH, SSM_STATE),
        cache_mem_k, cache_mem_v,
        gmg, gmb, gcoef, gbias, cw, cb, dtb, alog, dsk, sng, expand, seq_s)
    y_s = _sample_out(mix_s, xs2, w_out_b, fg)

    return (y_p,
            y_s.reshape(bs, seq_s, D_MODEL),
            ssm_p.reshape(1, bp, SSM_HEADS, SSM_HEAD_DIM, SSM_STATE),
            conv_p[None],
            mk,
            mv,
            ssm_s.reshape(1, bs, SSM_HEADS, SSM_HEAD_DIM, SSM_STATE),
            conv_s[None],
            gv_s.reshape(1, bs, seq_s, GM_WIDTH))
```

```python
import functools
import math

import jax
import jax.numpy as jnp
from jax import lax
from jax.experimental import pallas as pl
from jax.experimental.pallas import tpu as pltpu

F32 = jnp.float32
BF16 = jnp.bfloat16

D_MODEL = 1024
GM_WIDTH = 1024
GM_HEADS = 8
GM_HEAD_DIM = 128
CHUNK = 128
SSM_WIDTH = 1024
SSM_HEADS = 16
SSM_HEAD_DIM = 64
SSM_GROUPS = 2
SSM_STATE = 128
HEADS_PER_GROUP = SSM_HEADS // SSM_GROUPS
GROUP_WIDTH = SSM_WIDTH // SSM_GROUPS
CONV_WIDTH = 4
CONV_DIM = SSM_WIDTH + 2 * SSM_GROUPS * SSM_STATE
MEM_LEN = 256
MEM_HEADS = 4
MEM_HEAD_DIM = 256
MEM_WIDTH = 1024
MIX_WIDTH = GM_WIDTH + SSM_WIDTH + MEM_WIDTH
EPS = 1e-6

LANES = 128
SUBLANES = 8

DT_PAD = LANES
OFF_U = 0
OFF_V = OFF_U + GM_WIDTH
OFF_GATE = OFF_V + GM_WIDTH
OFF_Z = OFF_GATE + GM_WIDTH
OFF_XBC = OFF_Z + SSM_WIDTH
OFF_DT = OFF_XBC + CONV_DIM
OFF_Q = OFF_DT + SSM_HEADS
OFF_MGATE = OFF_Q + MEM_WIDTH
IN_WIDTH = OFF_MGATE + MEM_WIDTH
PROJ_BLOCK = 1024

PROMPT_TILE = 256
TILES_PER_STEP = 2
SAMPLE_BATCH_BLOCK = 4
VMEM_LIMIT = 56 * 1024 * 1024


def _rms(x, g):
    return x * lax.rsqrt(jnp.mean(x * x, axis=-1, keepdims=True) + EPS) * g


def _gelu(x):
    return 0.5 * x * (1.0 + lax.erf(x * math.sqrt(0.5)))


def _silu(x):
    return x * jax.nn.sigmoid(x)


def _softplus(x):
    return jnp.maximum(x, 0.0) + jnp.log1p(jnp.exp(-jnp.abs(x)))


def _layernorm(x, g, b):
    mu = jnp.mean(x, axis=-1, keepdims=True)
    xc = x - mu
    var = jnp.mean(xc * xc, axis=-1, keepdims=True)
    return xc * lax.rsqrt(var + EPS) * g + b


def _dot(a, b):
    return jnp.dot(a, b, preferred_element_type=F32)


def _dot_nt(a, b):
    return lax.dot_general(a, b, (((1,), (1,)), ((), ())), preferred_element_type=F32)


def _dot_tn(a, b):
    return lax.dot_general(a, b, (((0,), (0,)), ((), ())), preferred_element_type=F32)


def _group_rmsnorm(y, g):
    halves = []
    for i in range(SSM_GROUPS):
        yg = y[:, i * GROUP_WIDTH:(i + 1) * GROUP_WIDTH]
        halves.append(yg * lax.rsqrt(jnp.mean(yg * yg, axis=-1, keepdims=True) + EPS))
    return jnp.concatenate(halves, axis=-1) * g


def _memory_kv_kernel(mem_ref, g_ref, wk_ref, wv_ref, k_ref, v_ref, kb_ref, vb_ref):
    m = _rms(mem_ref[0], g_ref[...]).astype(BF16)
    k = _dot(m, wk_ref[...])
    v = _dot(m, wv_ref[...])
    for h in range(MEM_HEADS):
        k_ref[0, 0, :, h, :] = k[:, _head_slice(h)]
        v_ref[0, 0, :, h, :] = v[:, _head_slice(h)]
    kb_ref[0] = k.astype(BF16)
    vb_ref[0] = v.astype(BF16)


def _memory_kv(mem, g, wk, wv):
    b = mem.shape[0]
    blk = pl.BlockSpec((1, MEM_LEN, D_MODEL), lambda i: (i, 0, 0))
    blk5 = pl.BlockSpec((1, 1, MEM_LEN, MEM_HEADS, MEM_HEAD_DIM), lambda i: (0, i, 0, 0, 0))
    const = lambda shape: pl.BlockSpec(shape, lambda i: (0,) * len(shape))
    return pl.pallas_call(
        _memory_kv_kernel,
        grid=(b,),
        in_specs=[blk, const((1, D_MODEL)), const((D_MODEL, MEM_WIDTH)), const((D_MODEL, MEM_WIDTH))],
        out_specs=[blk5, blk5, blk, blk],
        out_shape=[jax.ShapeDtypeStruct((1, b, MEM_LEN, MEM_HEADS, MEM_HEAD_DIM), F32)] * 2
        + [jax.ShapeDtypeStruct((b, MEM_LEN, MEM_WIDTH), BF16)] * 2,
        compiler_params=pltpu.CompilerParams(dimension_semantics=("arbitrary",),
                                             vmem_limit_bytes=VMEM_LIMIT),
        name="memory_kv",
    )(mem, g, wk, wv)


def _head_slice(h):
    return slice(h * MEM_HEAD_DIM, (h + 1) * MEM_HEAD_DIM)


def _attention(q, k_head, v_head):
    qb = (q * (MEM_HEAD_DIM ** -0.5)).astype(BF16)
    outs = []
    for h in range(MEM_HEADS):
        s = _dot_nt(qb[:, _head_slice(h)], k_head(h))
        p = jnp.exp(s - jnp.max(s, axis=-1, keepdims=True))
        denom = jnp.sum(p, axis=-1, keepdims=True)
        outs.append(_dot(p.astype(BF16), v_head(h)) / denom)
    return jnp.concatenate(outs, axis=-1)


def _attention_interleaved(q, k2, v2):
    t = q.shape[0]
    qb = (q * (MEM_HEAD_DIM ** -0.5)).astype(BF16)
    q2 = jnp.concatenate([qb[:, _head_slice(h)] for h in range(MEM_HEADS)], axis=0)
    s = _dot_nt(q2, k2)
    row_head = lax.broadcasted_iota(jnp.int32, s.shape, 0) // t
    col_head = lax.broadcasted_iota(jnp.int32, s.shape, 1) % MEM_HEADS
    s = jnp.where(row_head == col_head, s, -jnp.inf)
    p = jnp.exp(s - jnp.max(s, axis=-1, keepdims=True))
    denom = jnp.sum(p, axis=-1, keepdims=True)
    return _dot(p.astype(BF16), v2) / denom


class _Handoff:
    N = 8

    def __init__(self, refs):
        (self.mix, self.ug, self.v, self.zs, self.xs, self.bm, self.cm, self.dt) = refs


def _prompt_step(x_ref, kb_ref, vb_ref, win_ref, wout_ref, ng_ref, gmg_ref, gmb_ref, ws_ref, bsf_ref,
                 cw_ref, cb_ref, dtb_ref, alog_ref, dsk_ref, sng_ref, fg_ref, y_ref,
                 hn_sc, xp_sc, ht_sc, ssd_sc, out_sc, yd_sc, st_sc, ea_sc, new, old, tile, rows):
    a_row = -jnp.exp(alog_ref[...])
    row = lax.broadcasted_iota(jnp.int32, (CHUNK, CHUNK), 0)
    col = lax.broadcasted_iota(jnp.int32, (CHUNK, CHUNK), 1)
    causal = row >= col
    tril_b = causal.astype(BF16)
    head_lane = col < SSM_HEADS
    first_half = col < SSM_HEAD_DIM
    def tables(c, g=0):
        base = c * 8
        return (ssd_sc.at[base], ssd_sc.at[base + 1], ssd_sc.at[base + 2], ssd_sc.at[base + 3],
                ssd_sc.at[base + 4 + 2 * g], ssd_sc.at[base + 5 + 2 * g])

    def proj(off, width):
        return _dot_nt(hn_sc[...], win_ref[off:off + width, :])

    def gmlp_mix(c):
        r = slice(c * CHUNK, (c + 1) * CHUNK)
        for h in range(GM_HEADS):
            hs = slice(h * GM_HEAD_DIM, (h + 1) * GM_HEAD_DIM)
            mixed = _dot(ws_ref[h], old.v[r, hs]) + bsf_ref[:, hs]
            old.mix[r, hs] = (old.ug[r, hs] * mixed).astype(BF16)

    def seq_chunk_start(c):
        r = slice(c * CHUNK, (c + 1) * CHUNK)
        acum_sc, acum_t_sc, dt_t_sc, to_end_t_sc, _, _ = tables(c)
        dt = old.dt[r, :]
        adt = jnp.where(head_lane, dt * a_row, 0.0)
        hi = adt.astype(BF16)
        rest = adt - hi.astype(F32)
        mid = rest.astype(BF16)
        lo = (rest - mid.astype(F32)).astype(BF16)
        acum = _dot(tril_b, hi) + _dot(tril_b, mid) + _dot(tril_b, lo)
        acum_t = acum.T
        dt_t = dt.T
        acum_sc[...] = acum
        acum_t_sc[...] = acum_t
        dt_t_sc[...] = dt_t
        to_end_t_sc[...] = jnp.exp(acum_t[:, CHUNK - 1:CHUNK] - acum_t) * dt_t
        for g in range(SSM_GROUPS):
            gs = slice(g * SSM_STATE, (g + 1) * SSM_STATE)
            _, _, _, _, cb_sc, bm_t_sc = tables(c, g)
            bm_g = old.bm[r, gs]
            cb_sc[...] = _dot_nt(old.cm[r, gs].astype(BF16), bm_g.astype(BF16))
            bm_t_sc[...] = bm_g.T

    def seq_group(c, g):
        r = slice(c * CHUNK, (c + 1) * CHUNK)
        acum_sc, acum_t_sc, dt_t_sc, to_end_t_sc, cb_sc, bm_t_sc = tables(c, g)
        for kk in range(0, HEADS_PER_GROUP, 2):
            k0 = g * HEADS_PER_GROUP + kk
            ps = slice(k0 * SSM_HEAD_DIM, (k0 + 2) * SSM_HEAD_DIM)
            m_parts, b_parts, ea = [], [], []
            for k in (k0, k0 + 1):
                a_col = jnp.broadcast_to(acum_sc[:, k:k + 1], (CHUNK, CHUNK))
                diff = a_col - acum_t_sc[k:k + 1, :]
                decay = jnp.exp(jnp.where(causal, diff, -jnp.inf))
                m_parts.append((cb_sc[...] * decay * dt_t_sc[k:k + 1, :]).astype(BF16))
                b_parts.append((bm_t_sc[...] * to_end_t_sc[k:k + 1, :]).astype(BF16))
                ea.append(jnp.exp(a_col))
            xs_pair = old.xs[r, ps]
            zero = jnp.zeros_like(xs_pair)
            xs_lo = jnp.where(first_half, xs_pair, zero).astype(BF16)
            xs_hi = jnp.where(first_half, zero, xs_pair).astype(BF16)
            lhs = jnp.concatenate([jnp.concatenate(m_parts, axis=1), jnp.concatenate(b_parts, axis=1)], axis=0)
            both = _dot(lhs, jnp.concatenate([xs_lo, xs_hi], axis=0))
            yd_sc[c, :, ps] = both[0:CHUNK, :]
            st_sc[c, :, ps] = both[CHUNK:2 * CHUNK, :]
            ea_sc[c, :, ps] = jnp.where(first_half, ea[0], ea[1])

    def seq_chunk_end(c):
        r = slice(c * CHUNK, (c + 1) * CHUNK)
        h_b = ht_sc[...].astype(BF16)
        y_off = jnp.concatenate(
            [_dot(old.cm[r, g * SSM_STATE:(g + 1) * SSM_STATE].astype(BF16), h_b[:, g * GROUP_WIDTH:(g + 1) * GROUP_WIDTH])
             for g in range(SSM_GROUPS)], axis=1)
        ea_c = ea_sc[c]
        y = yd_sc[c] + y_off * ea_c
        ht_sc[...] = ht_sc[...] * ea_c[CHUNK - 1:CHUNK, :] + st_sc[c]
        y = (y + dsk_ref[...] * old.xs[r, :]) * old.zs[r, :]
        old.mix[r, GM_WIDTH:GM_WIDTH + SSM_WIDTH] = _group_rmsnorm(y, sng_ref[...]).astype(BF16)

    def out_piece(j, n):
        cs = slice(j * D_MODEL // n, (j + 1) * D_MODEL // n)
        out_sc[:, cs] = x_ref[0, rows, cs] + _dot(old.mix[...], wout_ref[:, cs])

    assert tile // CHUNK == 2, "the emission order below is written for two chunks per tile"

    hn_sc[...] = _rms(x_ref[0, rows, :], ng_ref[...]).astype(BF16)

    new.zs[...] = _silu(proj(OFF_Z, SSM_WIDTH))
    xp_sc[SUBLANES:SUBLANES + tile, :] = proj(OFF_XBC, CONV_DIM)
    ext = xp_sc[...].reshape(tile // SUBLANES + 1, SUBLANES, CONV_DIM)
    sub = lax.broadcasted_iota(jnp.int32, (1, SUBLANES, CONV_DIM), 1)
    acc = cb_ref[...] + cw_ref[CONV_WIDTH - 1:CONV_WIDTH, :] * ext[1:]
    for shift in range(1, CONV_WIDTH):
        rot = pltpu.roll(ext, shift, 1)
        shifted = jnp.where(sub >= shift, rot[1:], rot[:-1])
        acc = acc + cw_ref[CONV_WIDTH - 1 - shift:CONV_WIDTH - shift, :] * shifted
    acc = acc.reshape(tile, CONV_DIM)
    xbc = _silu(acc)
    new.xs[...] = xbc[:, :SSM_WIDTH]
    new.bm[...] = xbc[:, SSM_WIDTH:SSM_WIDTH + SSM_GROUPS * SSM_STATE]
    new.cm[...] = xbc[:, SSM_WIDTH + SSM_GROUPS * SSM_STATE:]
    xp_sc[SUBLANES - (CONV_WIDTH - 1):SUBLANES, :] = xp_sc[tile + SUBLANES - (CONV_WIDTH - 1):tile + SUBLANES, :]
    new.dt[...] = _softplus(proj(OFF_DT, DT_PAD) + dtb_ref[...])
    u = _gelu(proj(OFF_U, GM_WIDTH))
    new.ug[...] = u * _silu(proj(OFF_GATE, GM_WIDTH))
    new.v[...] = _layernorm(_gelu(proj(OFF_V, GM_WIDTH)), gmg_ref[...], gmb_ref[...]).astype(BF16)
    seq_chunk_start(0)
    seq_chunk_start(1)
    gmlp_mix(0)
    gmlp_mix(1)
    seq_group(0, 0)
    seq_group(0, 1)
    seq_chunk_end(0)
    att = _attention(proj(OFF_Q, MEM_WIDTH), lambda h: kb_ref[0, :, _head_slice(h)], lambda h: vb_ref[0, :, _head_slice(h)])
    new.mix[:, GM_WIDTH + SSM_WIDTH:MIX_WIDTH] = (att * _silu(proj(OFF_MGATE, MEM_WIDTH))).astype(BF16)
    seq_group(1, 0)
    seq_group(1, 1)
    seq_chunk_end(1)

    n_out = 4
    for j in range(n_out):
        out_piece(j, n_out)
    y_ref[0, rows, :] = _rms(out_sc[...], fg_ref[...])


def _prompt_kernel(x_ref, kb_ref, vb_ref, win_ref, wout_ref, ng_ref, gmg_ref, gmb_ref,
                   ws_ref, bsf_ref, cw_ref, cb_ref, dtb_ref, alog_ref, dsk_ref, sng_ref, fg_ref,
                   y_ref, ssm_ref, conv_ref, hn_sc, xpad_sc, ht_sc, ssd_sc, out_sc, yd_sc, st_sc, ea_sc, *handoff, tile, nt, total):
    t = pl.program_id(0) % nt
    tile_set = _Handoff(handoff)

    @pl.when(t == 0)
    def _():
        xpad_sc[0:SUBLANES, :] = jnp.zeros((SUBLANES, CONV_DIM), F32)
        ht_sc[...] = jnp.zeros_like(ht_sc)

    for i in range(TILES_PER_STEP):
        _prompt_step(x_ref, kb_ref, vb_ref, win_ref, wout_ref, ng_ref, gmg_ref, gmb_ref, ws_ref, bsf_ref,
                     cw_ref, cb_ref, dtb_ref, alog_ref, dsk_ref, sng_ref, fg_ref, y_ref,
                     hn_sc, xpad_sc, ht_sc, ssd_sc, out_sc, yd_sc, st_sc, ea_sc, tile_set, tile_set, tile,
                     slice(i * tile, (i + 1) * tile))

    @pl.when(t == nt - 1)
    def _():
        conv_ref[0] = xpad_sc[SUBLANES - (CONV_WIDTH - 1):SUBLANES, :]
        ssm_ref[0] = ht_sc[...].T


def _prompt_layer(x, kb, vb, w_in_t, w_out, ng, gmg, gmb, ws, bsf, cw, cb, dtb, alog, dsk, sng, fg):
    b, seq, _ = x.shape
    tile = PROMPT_TILE
    step_rows = tile * TILES_PER_STEP
    nt = seq // step_rows
    total = b * nt
    assert seq % step_rows == 0 and tile % CHUNK == 0

    def const(shape):
        return pl.BlockSpec(shape, lambda s: (0,) * len(shape), pipeline_mode=pl.Buffered(1))

    in_specs = [
        pl.BlockSpec((1, step_rows, D_MODEL), lambda s: (s // nt, s % nt, 0)),
        pl.BlockSpec((1, MEM_LEN, MEM_WIDTH), lambda s: (s // nt, 0, 0)),
        pl.BlockSpec((1, MEM_LEN, MEM_WIDTH), lambda s: (s // nt, 0, 0)),
        const((IN_WIDTH, D_MODEL)),
        const((MIX_WIDTH, D_MODEL)),
        const((1, D_MODEL)), const((1, GM_WIDTH)), const((1, GM_WIDTH)),
        const((GM_HEADS, CHUNK, CHUNK)), const((CHUNK, GM_WIDTH)),
        const((CONV_WIDTH, CONV_DIM)), const((1, CONV_DIM)),
        const((1, DT_PAD)), const((1, DT_PAD)), const((1, SSM_WIDTH)), const((1, SSM_WIDTH)),
        const((1, D_MODEL)),
    ]
    out_specs = [
        pl.BlockSpec((1, step_rows, D_MODEL), lambda s: (s // nt, s % nt, 0)),
        pl.BlockSpec((1, SSM_WIDTH, SSM_STATE), lambda s: (s // nt, 0, 0)),
        pl.BlockSpec((1, CONV_WIDTH - 1, CONV_DIM), lambda s: (s // nt, 0, 0)),
    ]
    out_shape = [
        jax.ShapeDtypeStruct((b, seq, D_MODEL), F32),
        jax.ShapeDtypeStruct((b, SSM_WIDTH, SSM_STATE), F32),
        jax.ShapeDtypeStruct((b, CONV_WIDTH - 1, CONV_DIM), F32),
    ]
    handoff = [
        pltpu.VMEM((tile, MIX_WIDTH), BF16),
        pltpu.VMEM((tile, GM_WIDTH), F32),
        pltpu.VMEM((tile, GM_WIDTH), BF16),
        pltpu.VMEM((tile, SSM_WIDTH), F32),
        pltpu.VMEM((tile, SSM_WIDTH), F32),
        pltpu.VMEM((tile, SSM_GROUPS * SSM_STATE), F32),
        pltpu.VMEM((tile, SSM_GROUPS * SSM_STATE), F32),
        pltpu.VMEM((tile, DT_PAD), F32),
    ]
    assert len(handoff) == _Handoff.N
    scratch = [
        pltpu.VMEM((tile, D_MODEL), BF16),
        pltpu.VMEM((tile + SUBLANES, CONV_DIM), F32),
        pltpu.VMEM((SSM_STATE, SSM_WIDTH), F32),
        pltpu.VMEM((16, CHUNK, CHUNK), F32),
        pltpu.VMEM((tile, D_MODEL), F32),
        pltpu.VMEM((tile // CHUNK, CHUNK, SSM_WIDTH), F32),
        pltpu.VMEM((tile // CHUNK, SSM_STATE, SSM_WIDTH), F32),
        pltpu.VMEM((tile // CHUNK, CHUNK, SSM_WIDTH), F32),
    ] + handoff
    return pl.pallas_call(
        functools.partial(_prompt_kernel, tile=tile, nt=nt, total=total),
        grid=(total,),
        in_specs=in_specs,
        out_specs=out_specs,
        out_shape=out_shape,
        scratch_shapes=scratch,
        compiler_params=pltpu.CompilerParams(dimension_semantics=("arbitrary",),
                                             vmem_limit_bytes=VMEM_LIMIT),
        name="prompt_layer",
    )(x, kb, vb, w_in_t, w_out, ng, gmg, gmb, ws, bsf, cw, cb, dtb, alog, dsk, sng, fg)


def _sample_proj_kernel(x_ref, g_ref, w_ref, o_ref, wb_ref):
    hn = _rms(x_ref[...], g_ref[...]).astype(BF16)
    wb = w_ref[...].astype(BF16)
    wb_ref[...] = wb
    o_ref[...] = _dot_nt(hn, wb)


def _sample_proj(x, g, w_t):
    m, n = x.shape[0], w_t.shape[0]
    return pl.pallas_call(
        _sample_proj_kernel,
        grid=(pl.cdiv(n, PROJ_BLOCK),),
        in_specs=[pl.BlockSpec((m, D_MODEL), lambda j: (0, 0)),
                  pl.BlockSpec((1, D_MODEL), lambda j: (0, 0)),
                  pl.BlockSpec((PROJ_BLOCK, D_MODEL), lambda j: (j, 0))],
        out_specs=[pl.BlockSpec((m, PROJ_BLOCK), lambda j: (0, j)),
                   pl.BlockSpec((PROJ_BLOCK, D_MODEL), lambda j: (j, 0))],
        out_shape=[jax.ShapeDtypeStruct((m, n), F32), jax.ShapeDtypeStruct((n, D_MODEL), BF16)],
        compiler_params=pltpu.CompilerParams(dimension_semantics=("arbitrary",),
                                             vmem_limit_bytes=VMEM_LIMIT),
        name="sample_proj",
    )(x, g, w_t)


def _sample_mix_kernel(p_ref, cst_ref, ssm_ref, k_ref, v_ref, gmg_ref, gmb_ref, gcoef_ref, gbias_ref,
                       cw_ref, cb_ref, dtb_ref, alog_ref, dsk_ref, sng_ref, expand_ref,
                       mix_ref, ssm_out_ref, conv_out_ref, gv_ref,
                       xp_sc, xbc_sc, att_sc, yoff_sc, dec_sc, *, bblk, seq):
    rows = bblk * seq

    def col(off, width):
        return p_ref[:, off:off + width]

    tpos = lax.broadcasted_iota(jnp.int32, (rows, 1), 0) % seq

    def back(a, j):
        return a if j == 0 else pltpu.roll(a, j, 0)

    u = _gelu(col(OFF_U, GM_WIDTH))
    v = _layernorm(_gelu(col(OFF_V, GM_WIDTH)), gmg_ref[...], gmb_ref[...])
    gv_ref[...] = v
    mixed = gbias_ref[...]
    for j in range(seq):
        mixed = mixed + gcoef_ref[j] * back(v, j)
    mix_ref[:, 0:GM_WIDTH] = (u * mixed * _silu(col(OFF_GATE, GM_WIDTH))).astype(BF16)

    xbc_raw = col(OFF_XBC, CONV_DIM)
    for b in range(bblk):
        xp_sc[b, 0:CONV_WIDTH - 1, :] = cst_ref[b]
        xp_sc[b, CONV_WIDTH - 1:CONV_WIDTH - 1 + seq, :] = xbc_raw[b * seq:(b + 1) * seq, :]
    for b in range(bblk):
        acc = jnp.broadcast_to(cb_ref[...], (seq, CONV_DIM))
        for j in range(CONV_WIDTH):
            acc = acc + cw_ref[j:j + 1, :] * xp_sc[b, j:j + seq, :]
        xbc_sc[b * seq:(b + 1) * seq, :] = _silu(acc)
        conv_out_ref[b] = xp_sc[b, seq:seq + CONV_WIDTH - 1, :]
    xs = xbc_sc[:, 0:SSM_WIDTH]
    bm = xbc_sc[:, SSM_WIDTH:SSM_WIDTH + SSM_GROUPS * SSM_STATE]
    cm = xbc_sc[:, SSM_WIDTH + SSM_GROUPS * SSM_STATE:CONV_DIM]

    lane = lax.broadcasted_iota(jnp.int32, (rows, DT_PAD), 1)
    dt = _softplus(col(OFF_DT, DT_PAD) + dtb_ref[...])
    adt = jnp.where(lane < SSM_HEADS, dt * (-jnp.exp(alog_ref[...])), 0.0)
    acum = adt
    for j in range(1, seq):
        acum = acum + jnp.where(tpos >= j, back(adt, j), 0.0)
    a_last = jnp.zeros_like(acum)
    for j in range(seq):
        a_last = a_last + jnp.where(tpos == seq - 1 - j, acum if j == 0 else pltpu.roll(acum, rows - j, 0), 0.0)
    coefs = []
    for j in range(seq):
        cbj = []
        for g in range(SSM_GROUPS):
            gs = slice(g * SSM_STATE, (g + 1) * SSM_STATE)
            cbj.append(jnp.sum(cm[:, gs] * back(bm[:, gs], j), axis=-1, keepdims=True))
        cb_l = jnp.where(lane < HEADS_PER_GROUP, cbj[0], cbj[1])
        valid = tpos >= j
        decay = jnp.exp(jnp.where(valid, acum - back(acum, j), 0.0))
        coefs.append(jnp.where(valid, cb_l * decay * back(dt, j), 0.0))
    coefs.append(jnp.exp(acum))
    coefs.append(dt * jnp.exp(a_last - acum))
    stack = jnp.concatenate(coefs, axis=0)
    hi = stack.astype(BF16)
    lo = (stack - hi.astype(F32)).astype(BF16)
    wide = _dot(hi, expand_ref[...]) + _dot(lo, expand_ref[...])
    y = dsk_ref[...] * xs
    for j in range(seq):
        y = y + wide[j * rows:(j + 1) * rows, :] * back(xs, j)
    e_wide = wide[seq * rows:(seq + 1) * rows, :]
    wx = (xs * wide[(seq + 1) * rows:(seq + 2) * rows, :]).astype(BF16)
    dec_rows = jnp.exp(a_last)

    q = col(OFF_Q, MEM_WIDTH)
    cmb = cm.astype(BF16)
    bmb = bm.astype(BF16)
    for b in range(bblk):
        rs = slice(b * seq, (b + 1) * seq)
        att = _attention_interleaved(
            q[rs, :],
            k_ref[0, b].reshape(MEM_LEN * MEM_HEADS, MEM_HEAD_DIM).astype(BF16),
            v_ref[0, b].reshape(MEM_LEN * MEM_HEADS, MEM_HEAD_DIM).astype(BF16))
        for h in range(MEM_HEADS):
            att_sc[rs, _head_slice(h)] = att[h * seq:(h + 1) * seq, :]
        h0 = ssm_ref[b]
        h0b = h0.astype(BF16)
        dec_sc[...] = jnp.broadcast_to(dec_rows[b * seq + seq - 1:b * seq + seq, :], (LANES, DT_PAD)).T
        for g in range(SSM_GROUPS):
            gs = slice(g * SSM_STATE, (g + 1) * SSM_STATE)
            ws_ = slice(g * GROUP_WIDTH, (g + 1) * GROUP_WIDTH)
            yoff_sc[rs, ws_] = _dot_nt(cmb[rs, gs], h0b[ws_, :])
            upd = _dot_tn(wx[rs, ws_], bmb[rs, gs])
            for kk in range(HEADS_PER_GROUP):
                k = g * HEADS_PER_GROUP + kk
                hs = slice(k * SSM_HEAD_DIM, (k + 1) * SSM_HEAD_DIM)
                ssm_out_ref[b, hs, :] = (h0[hs, :] * dec_sc[k:k + 1, :]
                                         + upd[kk * SSM_HEAD_DIM:(kk + 1) * SSM_HEAD_DIM, :])

    y = (y + yoff_sc[...] * e_wide) * _silu(col(OFF_Z, SSM_WIDTH))
    mix_ref[:, GM_WIDTH:GM_WIDTH + SSM_WIDTH] = _group_rmsnorm(y, sng_ref[...]).astype(BF16)
    mix_ref[:, GM_WIDTH + SSM_WIDTH:MIX_WIDTH] = (att_sc[...] * _silu(col(OFF_MGATE, MEM_WIDTH))).astype(BF16)


def _sample_mix(proj, conv_state, ssm_state, mem_k, mem_v, gmg, gmb, gcoef, gbias, cw, cb, dtb, alog,
                dsk, sng, expand, seq):
    nb = conv_state.shape[0]
    bblk = SAMPLE_BATCH_BLOCK
    rows = bblk * seq
    assert nb % bblk == 0 and rows % (2 * SUBLANES) == 0

    def const(shape):
        return pl.BlockSpec(shape, lambda i: (0,) * len(shape))

    in_specs = [
        pl.BlockSpec((rows, IN_WIDTH), lambda i: (i, 0)),
        pl.BlockSpec((bblk, CONV_WIDTH - 1, CONV_DIM), lambda i: (i, 0, 0)),
        pl.BlockSpec((bblk, SSM_WIDTH, SSM_STATE), lambda i: (i, 0, 0)),
        pl.BlockSpec((1, bblk, MEM_LEN, MEM_HEADS, MEM_HEAD_DIM), lambda i: (0, i, 0, 0, 0)),
        pl.BlockSpec((1, bblk, MEM_LEN, MEM_HEADS, MEM_HEAD_DIM), lambda i: (0, i, 0, 0, 0)),
        const((1, GM_WIDTH)), const((1, GM_WIDTH)),
        const((seq, rows, GM_WIDTH)), const((rows, GM_WIDTH)),
        const((CONV_WIDTH, CONV_DIM)), const((1, CONV_DIM)),
        const((1, DT_PAD)), const((1, DT_PAD)), const((1, SSM_WIDTH)), const((1, SSM_WIDTH)),
        const((DT_PAD, SSM_WIDTH)),
    ]
    out_specs = [
        pl.BlockSpec((rows, MIX_WIDTH), lambda i: (i, 0)),
        pl.BlockSpec((bblk, SSM_WIDTH, SSM_STATE), lambda i: (i, 0, 0)),
        pl.BlockSpec((bblk, CONV_WIDTH - 1, CONV_DIM), lambda i: (i, 0, 0)),
        pl.BlockSpec((rows, GM_WIDTH), lambda i: (i, 0)),
    ]
    out_shape = [
        jax.ShapeDtypeStruct((nb * seq, MIX_WIDTH), BF16),
        jax.ShapeDtypeStruct((nb, SSM_WIDTH, SSM_STATE), F32),
        jax.ShapeDtypeStruct((nb, CONV_WIDTH - 1, CONV_DIM), F32),
        jax.ShapeDtypeStruct((nb * seq, GM_WIDTH), F32),
    ]
    scratch = [
        pltpu.VMEM((bblk, SUBLANES, CONV_DIM), F32),
        pltpu.VMEM((rows, CONV_DIM), F32),
        pltpu.VMEM((rows, MEM_WIDTH), F32),
        pltpu.VMEM((rows, SSM_WIDTH), F32),
        pltpu.VMEM((DT_PAD, LANES), F32),
    ]
    return pl.pallas_call(
        functools.partial(_sample_mix_kernel, bblk=bblk, seq=seq),
        grid=(nb // bblk,),
        in_specs=in_specs,
        out_specs=out_specs,
        out_shape=out_shape,
        scratch_shapes=scratch,
        compiler_params=pltpu.CompilerParams(dimension_semantics=("arbitrary",),
                                             vmem_limit_bytes=VMEM_LIMIT),
        name="sample_mix",
    )(proj, conv_state, ssm_state, mem_k, mem_v, gmg, gmb, gcoef, gbias, cw, cb, dtb, alog, dsk, sng, expand)


def _sample_out_kernel(mix_ref, x_ref, w_ref, g_ref, o_ref):
    out = x_ref[...] + _dot(mix_ref[...], w_ref[...])
    o_ref[...] = _rms(out, g_ref[...])


def _sample_out(mix, x, w_out, g):
    m = x.shape[0]
    full = lambda shape: pl.BlockSpec(shape, lambda i: (0,) * len(shape))
    return pl.pallas_call(
        _sample_out_kernel,
        grid=(1,),
        in_specs=[full((m, MIX_WIDTH)), full((m, D_MODEL)), full((MIX_WIDTH, D_MODEL)), full((1, D_MODEL))],
        out_specs=full((m, D_MODEL)),
        out_shape=jax.ShapeDtypeStruct((m, D_MODEL), F32),
        compiler_params=pltpu.CompilerParams(dimension_semantics=("arbitrary",),
                                             vmem_limit_bytes=VMEM_LIMIT),
        name="sample_out",
    )(mix, x, w_out, g)


def _pad_heads(a):
    return jnp.pad(a.astype(F32), (0, DT_PAD - SSM_HEADS)).reshape(1, DT_PAD)


def kernel(x_prompt, x_sample, mem_prompt, state_ssm, state_conv, cache_mem_k, cache_mem_v, norm_g, w_in,
           gm_norm_g, gm_norm_b, gm_w_spatial, gm_b_spatial, conv_w, conv_b, dt_bias, a_log, d_skip,
           ssm_norm_g, mem_norm_g, w_mem_k, w_mem_v, w_out, final_norm_g):
    assert norm_g.shape[0] == 1, "single layer"
    bp, seq_p, _ = x_prompt.shape
    bs, seq_s, _ = x_sample.shape
    row = lambda a: a.reshape(1, -1).astype(F32)

    w_out_b = w_out[0].astype(BF16)
    ng, gmg, gmb = row(norm_g[0]), row(gm_norm_g[0]), row(gm_norm_b[0])
    cw, cb = conv_w[0].astype(F32), row(conv_b[0])
    dtb, alog = _pad_heads(dt_bias[0]), _pad_heads(a_log[0])
    dsk = row(jnp.repeat(d_skip[0], SSM_HEAD_DIM))
    sng, fg = row(ssm_norm_g[0]), row(final_norm_g)
    w_sp = gm_w_spatial[0]
    tril_p = jnp.tril(jnp.ones((CHUNK, CHUNK), bool))
    ws_p = jnp.where(tril_p, w_sp, 0).astype(BF16)
    bsf_p = jnp.repeat(gm_b_spatial[0].T, GM_HEAD_DIM, axis=1).astype(F32)

    xs2 = x_sample.reshape(bs * seq_s, D_MODEL)
    proj_s, w_in_t = _sample_proj(xs2, ng, jnp.swapaxes(w_in[0], 0, 1))

    mk, mv, mkb, mvb = _memory_kv(mem_prompt, row(mem_norm_g[0]), w_mem_k[0].astype(BF16),
                                  w_mem_v[0].astype(BF16))
    y_p, ssm_p, conv_p = _prompt_layer(x_prompt, mkb, mvb, w_in_t, w_out_b, ng, gmg, gmb, ws_p, bsf_p,
                                       cw, cb, dtb, alog, dsk, sng, fg)

    rows = SAMPLE_BATCH_BLOCK * seq_s
    tpos = jnp.arange(rows) % seq_s
    gcoef = jnp.stack([
        jnp.where((tpos >= j)[:, None],
                  jnp.repeat(w_sp[:, tpos, jnp.maximum(tpos - j, 0)].T, GM_HEAD_DIM, axis=1), 0.0)
        for j in range(seq_s)]).astype(F32)
    gbias = jnp.repeat(gm_b_spatial[0][:, tpos].T, GM_HEAD_DIM, axis=1).astype(F32)
    expand = (jnp.arange(DT_PAD)[:, None] == (jnp.arange(SSM_WIDTH) // SSM_HEAD_DIM)[None, :]).astype(BF16)

    mix_s, ssm_s, conv_s, gv_s = _sample_mix(
        proj_s, state_conv[0], state_ssm[0].reshape(bs, SSM_WIDTH, SSM_STATE),
        cache_mem_k, cache_mem_v,
        gmg, gmb, gcoef, gbias, cw, cb, dtb, alog, dsk, sng, expand, seq_s)
    y_s = _sample_out(mix_s, xs2, w_out_b, fg)

    return (y_p,
            y_s.reshape(bs, seq_s, D_MODEL),
            ssm_p.reshape(1, bp, SSM_HEADS, SSM_HEAD_DIM, SSM_STATE),
            conv_p[None],
            mk,
            mv,
            ssm_s.reshape(1, bs, SSM_HEADS, SSM_HEAD_DIM, SSM_STATE),
            conv_s[None],
            gv_s.reshape(1, bs, seq_s, GM_WIDTH))
```

```python
import functools
import math

import jax
import jax.numpy as jnp
from jax import lax
from jax.experimental import pallas as pl
from jax.experimental.pallas import tpu as pltpu

F32 = jnp.float32
BF16 = jnp.bfloat16

D_MODEL = 1024
GM_WIDTH = 1024
GM_HEADS = 8
GM_HEAD_DIM = 128
CHUNK = 128
SSM_WIDTH = 1024
SSM_HEADS = 16
SSM_HEAD_DIM = 64
SSM_GROUPS = 2
SSM_STATE = 128
HEADS_PER_GROUP = SSM_HEADS // SSM_GROUPS
GROUP_WIDTH = SSM_WIDTH // SSM_GROUPS
CONV_WIDTH = 4
CONV_DIM = SSM_WIDTH + 2 * SSM_GROUPS * SSM_STATE
MEM_LEN = 256
MEM_HEADS = 4
MEM_HEAD_DIM = 256
MEM_WIDTH = 1024
MIX_WIDTH = GM_WIDTH + SSM_WIDTH + MEM_WIDTH
EPS = 1e-6

LANES = 128
SUBLANES = 8

DT_PAD = LANES
OFF_U = 0
OFF_V = OFF_U + GM_WIDTH
OFF_GATE = OFF_V + GM_WIDTH
OFF_Z = OFF_GATE + GM_WIDTH
OFF_XBC = OFF_Z + SSM_WIDTH
OFF_DT = OFF_XBC + CONV_DIM
OFF_Q = OFF_DT + SSM_HEADS
OFF_MGATE = OFF_Q + MEM_WIDTH
IN_WIDTH = OFF_MGATE + MEM_WIDTH
PROJ_BLOCK = 1024

PROMPT_TILE = 256
TILES_PER_STEP = 2
SAMPLE_BATCH_BLOCK = 4
VMEM_LIMIT = 56 * 1024 * 1024


def _rms(x, g):
    return x * lax.rsqrt(jnp.mean(x * x, axis=-1, keepdims=True) + EPS) * g


def _gelu(x):
    return 0.5 * x * (1.0 + lax.erf(x * math.sqrt(0.5)))


def _silu(x):
    h = 0.5 * x
    return h + h * jnp.tanh(h)


def _softplus(x):
    return jnp.maximum(x, 0.0) + jnp.log1p(jnp.exp(-jnp.abs(x)))


def _layernorm(x, g, b):
    mu = jnp.mean(x, axis=-1, keepdims=True)
    xc = x - mu
    var = jnp.mean(xc * xc, axis=-1, keepdims=True)
    return xc * lax.rsqrt(var + EPS) * g + b


def _dot(a, b):
    return jnp.dot(a, b, preferred_element_type=F32)


def _dot_nt(a, b):
    return lax.dot_general(a, b, (((1,), (1,)), ((), ())), preferred_element_type=F32)


def _dot_tn(a, b):
    return lax.dot_general(a, b, (((0,), (0,)), ((), ())), preferred_element_type=F32)


def _group_rmsnorm(y, g):
    halves = []
    for i in range(SSM_GROUPS):
        yg = y[:, i * GROUP_WIDTH:(i + 1) * GROUP_WIDTH]
        halves.append(yg * lax.rsqrt(jnp.mean(yg * yg, axis=-1, keepdims=True) + EPS))
    return jnp.concatenate(halves, axis=-1) * g


def _memory_kv_kernel(mem_ref, g_ref, wk_ref, wv_ref, k_ref, v_ref, kb_ref, vb_ref):
    m = _rms(mem_ref[0], g_ref[...]).astype(BF16)
    k = _dot(m, wk_ref[...])
    v = _dot(m, wv_ref[...])
    for h in range(MEM_HEADS):
        k_ref[0, 0, :, h, :] = k[:, _head_slice(h)]
        v_ref[0, 0, :, h, :] = v[:, _head_slice(h)]
    kb_ref[0] = k.astype(BF16)
    vb_ref[0] = v.astype(BF16)


def _memory_kv(mem, g, wk, wv):
    b = mem.shape[0]
    blk = pl.BlockSpec((1, MEM_LEN, D_MODEL), lambda i: (i, 0, 0))
    blk5 = pl.BlockSpec((1, 1, MEM_LEN, MEM_HEADS, MEM_HEAD_DIM), lambda i: (0, i, 0, 0, 0))
    const = lambda shape: pl.BlockSpec(shape, lambda i: (0,) * len(shape))
    return pl.pallas_call(
        _memory_kv_kernel,
        grid=(b,),
        in_specs=[blk, const((1, D_MODEL)), const((D_MODEL, MEM_WIDTH)), const((D_MODEL, MEM_WIDTH))],
        out_specs=[blk5, blk5, blk, blk],
        out_shape=[jax.ShapeDtypeStruct((1, b, MEM_LEN, MEM_HEADS, MEM_HEAD_DIM), F32)] * 2
        + [jax.ShapeDtypeStruct((b, MEM_LEN, MEM_WIDTH), BF16)] * 2,
        compiler_params=pltpu.CompilerParams(dimension_semantics=("arbitrary",),
                                             vmem_limit_bytes=VMEM_LIMIT),
        name="memory_kv",
    )(mem, g, wk, wv)


def _head_slice(h):
    return slice(h * MEM_HEAD_DIM, (h + 1) * MEM_HEAD_DIM)


def _attention(q, k_head, v_head):
    qb = (q * (MEM_HEAD_DIM ** -0.5)).astype(BF16)
    outs = []
    for h in range(MEM_HEADS):
        s = _dot_nt(qb[:, _head_slice(h)], k_head(h))
        p = jnp.exp(s - jnp.max(s, axis=-1, keepdims=True))
        denom = jnp.sum(p, axis=-1, keepdims=True)
        outs.append(_dot(p.astype(BF16), v_head(h)) / denom)
    return jnp.concatenate(outs, axis=-1)


def _attention_interleaved(q, k2, v2):
    t = q.shape[0]
    qb = (q * (MEM_HEAD_DIM ** -0.5)).astype(BF16)
    q2 = jnp.concatenate([qb[:, _head_slice(h)] for h in range(MEM_HEADS)], axis=0)
    s = _dot_nt(q2, k2)
    row_head = lax.broadcasted_iota(jnp.int32, s.shape, 0) // t
    col_head = lax.broadcasted_iota(jnp.int32, s.shape, 1) % MEM_HEADS
    s = jnp.where(row_head == col_head, s, -jnp.inf)
    p = jnp.exp(s - jnp.max(s, axis=-1, keepdims=True))
    denom = jnp.sum(p, axis=-1, keepdims=True)
    return _dot(p.astype(BF16), v2) / denom


class _Handoff:
    N = 8

    def __init__(self, refs):
        (self.mix, self.ug, self.v, self.zs, self.xs, self.bm, self.cm, self.dt) = refs


def _prompt_step(x_ref, kb_ref, vb_ref, win_ref, wout_ref, ng_ref, gmg_ref, gmb_ref, ws_ref, bsf_ref,
                 cw_ref, cb_ref, dtb_ref, alog_ref, dsk_ref, sng_ref, fg_ref, y_ref,
                 hn_sc, xp_sc, ht_sc, ssd_sc, out_sc, yd_sc, st_sc, ea_sc, new, old, tile, rows):
    a_row = -jnp.exp(alog_ref[...])
    row = lax.broadcasted_iota(jnp.int32, (CHUNK, CHUNK), 0)
    col = lax.broadcasted_iota(jnp.int32, (CHUNK, CHUNK), 1)
    causal = row >= col
    tril_b = causal.astype(BF16)
    head_lane = col < SSM_HEADS
    first_half = col < SSM_HEAD_DIM
    def tables(c, g=0):
        base = c * 8
        return (ssd_sc.at[base], ssd_sc.at[base + 1], ssd_sc.at[base + 2], ssd_sc.at[base + 3],
                ssd_sc.at[base + 4 + 2 * g], ssd_sc.at[base + 5 + 2 * g])

    def proj(off, width):
        return _dot_nt(hn_sc[...], win_ref[off:off + width, :])

    def gmlp_mix(c):
        r = slice(c * CHUNK, (c + 1) * CHUNK)
        for h in range(GM_HEADS):
            hs = slice(h * GM_HEAD_DIM, (h + 1) * GM_HEAD_DIM)
            mixed = _dot(ws_ref[h], old.v[r, hs]) + bsf_ref[:, hs]
            old.mix[r, hs] = (old.ug[r, hs] * mixed).astype(BF16)

    def seq_chunk_start(c):
        r = slice(c * CHUNK, (c + 1) * CHUNK)
        acum_sc, acum_t_sc, dt_t_sc, to_end_t_sc, _, _ = tables(c)
        dt = old.dt[r, :]
        adt = jnp.where(head_lane, dt * a_row, 0.0)
        hi = adt.astype(BF16)
        rest = adt - hi.astype(F32)
        mid = rest.astype(BF16)
        lo = (rest - mid.astype(F32)).astype(BF16)
        acum = _dot(tril_b, hi) + _dot(tril_b, mid) + _dot(tril_b, lo)
        acum_t = acum.T
        dt_t = dt.T
        acum_sc[...] = acum
        acum_t_sc[...] = acum_t
        dt_t_sc[...] = dt_t
        to_end_t_sc[...] = jnp.exp(acum_t[:, CHUNK - 1:CHUNK] - acum_t) * dt_t
        for g in range(SSM_GROUPS):
            gs = slice(g * SSM_STATE, (g + 1) * SSM_STATE)
            _, _, _, _, cb_sc, bm_t_sc = tables(c, g)
            bm_g = old.bm[r, gs]
            cb_sc[...] = _dot_nt(old.cm[r, gs].astype(BF16), bm_g.astype(BF16))
            bm_t_sc[...] = bm_g.T

    def seq_group(c, g):
        r = slice(c * CHUNK, (c + 1) * CHUNK)
        acum_sc, acum_t_sc, dt_t_sc, to_end_t_sc, cb_sc, bm_t_sc = tables(c, g)
        for kk in range(0, HEADS_PER_GROUP, 2):
            k0 = g * HEADS_PER_GROUP + kk
            ps = slice(k0 * SSM_HEAD_DIM, (k0 + 2) * SSM_HEAD_DIM)
            m_parts, b_parts, ea = [], [], []
            for k in (k0, k0 + 1):
                a_col = jnp.broadcast_to(acum_sc[:, k:k + 1], (CHUNK, CHUNK))
                diff = a_col - acum_t_sc[k:k + 1, :]
                decay = jnp.exp(jnp.where(causal, diff, -jnp.inf))
                m_parts.append((cb_sc[...] * decay * dt_t_sc[k:k + 1, :]).astype(BF16))
                b_parts.append((bm_t_sc[...] * to_end_t_sc[k:k + 1, :]).astype(BF16))
                ea.append(jnp.exp(a_col))
            xs_pair = old.xs[r, ps]
            zero = jnp.zeros_like(xs_pair)
            xs_lo = jnp.where(first_half, xs_pair, zero).astype(BF16)
            xs_hi = jnp.where(first_half, zero, xs_pair).astype(BF16)
            lhs = jnp.concatenate([jnp.concatenate(m_parts, axis=1), jnp.concatenate(b_parts, axis=1)], axis=0)
            both = _dot(lhs, jnp.concatenate([xs_lo, xs_hi], axis=0))
            yd_sc[c, :, ps] = both[0:CHUNK, :]
            st_sc[c, :, ps] = both[CHUNK:2 * CHUNK, :]
            ea_sc[c, :, ps] = jnp.where(first_half, ea[0], ea[1])

    def seq_chunk_end(c):
        r = slice(c * CHUNK, (c + 1) * CHUNK)
        h_b = ht_sc[...].astype(BF16)
        y_off = jnp.concatenate(
            [_dot(old.cm[r, g * SSM_STATE:(g + 1) * SSM_STATE].astype(BF16), h_b[:, g * GROUP_WIDTH:(g + 1) * GROUP_WIDTH])
             for g in range(SSM_GROUPS)], axis=1)
        ea_c = ea_sc[c]
        y = yd_sc[c] + y_off * ea_c
        ht_sc[...] = ht_sc[...] * ea_c[CHUNK - 1:CHUNK, :] + st_sc[c]
        y = (y + dsk_ref[...] * old.xs[r, :]) * old.zs[r, :]
        old.mix[r, GM_WIDTH:GM_WIDTH + SSM_WIDTH] = _group_rmsnorm(y, sng_ref[...]).astype(BF16)

    def out_piece(j, n):
        cs = slice(j * D_MODEL // n, (j + 1) * D_MODEL // n)
        out_sc[:, cs] = x_ref[0, rows, cs] + _dot(old.mix[...], wout_ref[:, cs])

    assert tile // CHUNK == 2, "the emission order below is written for two chunks per tile"

    hn_sc[...] = _rms(x_ref[0, rows, :], ng_ref[...]).astype(BF16)

    new.zs[...] = _silu(proj(OFF_Z, SSM_WIDTH))
    xp_sc[SUBLANES:SUBLANES + tile, :] = proj(OFF_XBC, CONV_DIM)
    ext = xp_sc[...].reshape(tile // SUBLANES + 1, SUBLANES, CONV_DIM)
    sub = lax.broadcasted_iota(jnp.int32, (1, SUBLANES, CONV_DIM), 1)
    acc = cb_ref[...] + cw_ref[CONV_WIDTH - 1:CONV_WIDTH, :] * ext[1:]
    for shift in range(1, CONV_WIDTH):
        rot = pltpu.roll(ext, shift, 1)
        shifted = jnp.where(sub >= shift, rot[1:], rot[:-1])
        acc = acc + cw_ref[CONV_WIDTH - 1 - shift:CONV_WIDTH - shift, :] * shifted
    acc = acc.reshape(tile, CONV_DIM)
    xbc = _silu(acc)
    new.xs[...] = xbc[:, :SSM_WIDTH]
    new.bm[...] = xbc[:, SSM_WIDTH:SSM_WIDTH + SSM_GROUPS * SSM_STATE]
    new.cm[...] = xbc[:, SSM_WIDTH + SSM_GROUPS * SSM_STATE:]
    xp_sc[SUBLANES - (CONV_WIDTH - 1):SUBLANES, :] = xp_sc[tile + SUBLANES - (CONV_WIDTH - 1):tile + SUBLANES, :]
    new.dt[...] = _softplus(proj(OFF_DT, DT_PAD) + dtb_ref[...])
    u = _gelu(proj(OFF_U, GM_WIDTH))
    new.ug[...] = u * _silu(proj(OFF_GATE, GM_WIDTH))
    new.v[...] = _layernorm(_gelu(proj(OFF_V, GM_WIDTH)), gmg_ref[...], gmb_ref[...]).astype(BF16)
    seq_chunk_start(0)
    seq_chunk_start(1)
    gmlp_mix(0)
    gmlp_mix(1)
    seq_group(0, 0)
    seq_group(0, 1)
    seq_chunk_end(0)
    att = _attention(proj(OFF_Q, MEM_WIDTH), lambda h: kb_ref[0, :, _head_slice(h)], lambda h: vb_ref[0, :, _head_slice(h)])
    new.mix[:, GM_WIDTH + SSM_WIDTH:MIX_WIDTH] = (att * _silu(proj(OFF_MGATE, MEM_WIDTH))).astype(BF16)
    seq_group(1, 0)
    seq_group(1, 1)
    seq_chunk_end(1)

    n_out = 4
    for j in range(n_out):
        out_piece(j, n_out)
    y_ref[0, rows, :] = _rms(out_sc[...], fg_ref[...])


def _prompt_kernel(x_ref, kb_ref, vb_ref, win_ref, wout_ref, ng_ref, gmg_ref, gmb_ref,
                   ws_ref, bsf_ref, cw_ref, cb_ref, dtb_ref, alog_ref, dsk_ref, sng_ref, fg_ref,
                   y_ref, ssm_ref, conv_ref, hn_sc, xpad_sc, ht_sc, ssd_sc, out_sc, yd_sc, st_sc, ea_sc, *handoff, tile, nt, total):
    t = pl.program_id(0) % nt
    tile_set = _Handoff(handoff)

    @pl.when(t == 0)
    def _():
        xpad_sc[0:SUBLANES, :] = jnp.zeros((SUBLANES, CONV_DIM), F32)
        ht_sc[...] = jnp.zeros_like(ht_sc)

    for i in range(TILES_PER_STEP):
        _prompt_step(x_ref, kb_ref, vb_ref, win_ref, wout_ref, ng_ref, gmg_ref, gmb_ref, ws_ref, bsf_ref,
                     cw_ref, cb_ref, dtb_ref, alog_ref, dsk_ref, sng_ref, fg_ref, y_ref,
                     hn_sc, xpad_sc, ht_sc, ssd_sc, out_sc, yd_sc, st_sc, ea_sc, tile_set, tile_set, tile,
                     slice(i * tile, (i + 1) * tile))

    @pl.when(t == nt - 1)
    def _():
        conv_ref[0] = xpad_sc[SUBLANES - (CONV_WIDTH - 1):SUBLANES, :]
        ssm_ref[0] = ht_sc[...].T


def _prompt_layer(x, kb, vb, w_in_t, w_out, ng, gmg, gmb, ws, bsf, cw, cb, dtb, alog, dsk, sng, fg):
    b, seq, _ = x.shape
    tile = PROMPT_TILE
    step_rows = tile * TILES_PER_STEP
    nt = seq // step_rows
    total = b * nt
    assert seq % step_rows == 0 and tile % CHUNK == 0

    def const(shape):
        return pl.BlockSpec(shape, lambda s: (0,) * len(shape), pipeline_mode=pl.Buffered(1))

    in_specs = [
        pl.BlockSpec((1, step_rows, D_MODEL), lambda s: (s // nt, s % nt, 0)),
        pl.BlockSpec((1, MEM_LEN, MEM_WIDTH), lambda s: (s // nt, 0, 0)),
        pl.BlockSpec((1, MEM_LEN, MEM_WIDTH), lambda s: (s // nt, 0, 0)),
        const((IN_WIDTH, D_MODEL)),
        const((MIX_WIDTH, D_MODEL)),
        const((1, D_MODEL)), const((1, GM_WIDTH)), const((1, GM_WIDTH)),
        const((GM_HEADS, CHUNK, CHUNK)), const((CHUNK, GM_WIDTH)),
        const((CONV_WIDTH, CONV_DIM)), const((1, CONV_DIM)),
        const((1, DT_PAD)), const((1, DT_PAD)), const((1, SSM_WIDTH)), const((1, SSM_WIDTH)),
        const((1, D_MODEL)),
    ]
    out_specs = [
        pl.BlockSpec((1, step_rows, D_MODEL), lambda s: (s // nt, s % nt, 0)),
        pl.BlockSpec((1, SSM_WIDTH, SSM_STATE), lambda s: (s // nt, 0, 0)),
        pl.BlockSpec((1, CONV_WIDTH - 1, CONV_DIM), lambda s: (s // nt, 0, 0)),
    ]
    out_shape = [
        jax.ShapeDtypeStruct((b, seq, D_MODEL), F32),
        jax.ShapeDtypeStruct((b, SSM_WIDTH, SSM_STATE), F32),
        jax.ShapeDtypeStruct((b, CONV_WIDTH - 1, CONV_DIM), F32),
    ]
    handoff = [
        pltpu.VMEM((tile, MIX_WIDTH), BF16),
        pltpu.VMEM((tile, GM_WIDTH), F32),
        pltpu.VMEM((tile, GM_WIDTH), BF16),
        pltpu.VMEM((tile, SSM_WIDTH), F32),
        pltpu.VMEM((tile, SSM_WIDTH), F32),
        pltpu.VMEM((tile, SSM_GROUPS * SSM_STATE), F32),
        pltpu.VMEM((tile, SSM_GROUPS * SSM_STATE), F32),
        pltpu.VMEM((tile, DT_PAD), F32),
    ]
    assert len(handoff) == _Handoff.N
    scratch = [
        pltpu.VMEM((tile, D_MODEL), BF16),
        pltpu.VMEM((tile + SUBLANES, CONV_DIM), F32),
        pltpu.VMEM((SSM_STATE, SSM_WIDTH), F32),
        pltpu.VMEM((16, CHUNK, CHUNK), F32),
        pltpu.VMEM((tile, D_MODEL), F32),
        pltpu.VMEM((tile // CHUNK, CHUNK, SSM_WIDTH), F32),
        pltpu.VMEM((tile // CHUNK, SSM_STATE, SSM_WIDTH), F32),
        pltpu.VMEM((tile // CHUNK, CHUNK, SSM_WIDTH), F32),
    ] + handoff
    return pl.pallas_call(
        functools.partial(_prompt_kernel, tile=tile, nt=nt, total=total),
        grid=(total,),
        in_specs=in_specs,
        out_specs=out_specs,
        out_shape=out_shape,
        scratch_shapes=scratch,
        compiler_params=pltpu.CompilerParams(dimension_semantics=("arbitrary",),
                                             vmem_limit_bytes=VMEM_LIMIT),
        name="prompt_layer",
    )(x, kb, vb, w_in_t, w_out, ng, gmg, gmb, ws, bsf, cw, cb, dtb, alog, dsk, sng, fg)


def _sample_proj_kernel(x_ref, g_ref, w_ref, o_ref, wb_ref):
    nb, seq, d = x_ref.shape
    hn = _rms(x_ref[...].reshape(nb * seq, d), g_ref[...]).astype(BF16)
    wb = w_ref[...].astype(BF16)
    wb_ref[...] = wb
    o_ref[...] = _dot_nt(hn, wb)


def _sample_proj(x, g, w_t):
    m, n = x.shape[0] * x.shape[1], w_t.shape[0]
    return pl.pallas_call(
        _sample_proj_kernel,
        grid=(pl.cdiv(n, PROJ_BLOCK),),
        in_specs=[pl.BlockSpec(x.shape, lambda j: (0, 0, 0)),
                  pl.BlockSpec((1, D_MODEL), lambda j: (0, 0)),
                  pl.BlockSpec((PROJ_BLOCK, D_MODEL), lambda j: (j, 0))],
        out_specs=[pl.BlockSpec((m, PROJ_BLOCK), lambda j: (0, j)),
                   pl.BlockSpec((PROJ_BLOCK, D_MODEL), lambda j: (j, 0))],
        out_shape=[jax.ShapeDtypeStruct((m, n), F32), jax.ShapeDtypeStruct((n, D_MODEL), BF16)],
        compiler_params=pltpu.CompilerParams(dimension_semantics=("arbitrary",),
                                             vmem_limit_bytes=VMEM_LIMIT),
        name="sample_proj",
    )(x, g, w_t)


def _sample_mix_kernel(p_ref, cst_ref, ssm_ref, k_ref, v_ref, gmg_ref, gmb_ref, gcoef_ref, gbias_ref,
                       cw_ref, cb_ref, dtb_ref, alog_ref, dsk_ref, sng_ref, expand_ref,
                       mix_ref, ssm_out_ref, conv_out_ref, gv_ref,
                       xp_sc, xbc_sc, att_sc, yoff_sc, dec_sc, *, bblk, seq):
    rows = bblk * seq

    def col(off, width):
        return p_ref[:, off:off + width]

    tpos = lax.broadcasted_iota(jnp.int32, (rows, 1), 0) % seq

    def back(a, j):
        return a if j == 0 else pltpu.roll(a, j, 0)

    u = _gelu(col(OFF_U, GM_WIDTH))
    v = _layernorm(_gelu(col(OFF_V, GM_WIDTH)), gmg_ref[...], gmb_ref[...])
    gv_ref[...] = v
    mixed = gbias_ref[...]
    for j in range(seq):
        mixed = mixed + gcoef_ref[j] * back(v, j)
    mix_ref[:, 0:GM_WIDTH] = (u * mixed * _silu(col(OFF_GATE, GM_WIDTH))).astype(BF16)

    xbc_raw = col(OFF_XBC, CONV_DIM)
    for b in range(bblk):
        xp_sc[b, 0:CONV_WIDTH - 1, :] = cst_ref[b]
        xp_sc[b, CONV_WIDTH - 1:CONV_WIDTH - 1 + seq, :] = xbc_raw[b * seq:(b + 1) * seq, :]
    for b in range(bblk):
        acc = jnp.broadcast_to(cb_ref[...], (seq, CONV_DIM))
        for j in range(CONV_WIDTH):
            acc = acc + cw_ref[j:j + 1, :] * xp_sc[b, j:j + seq, :]
        xbc_sc[b * seq:(b + 1) * seq, :] = _silu(acc)
        conv_out_ref[b] = xp_sc[b, seq:seq + CONV_WIDTH - 1, :]
    xs = xbc_sc[:, 0:SSM_WIDTH]
    bm = xbc_sc[:, SSM_WIDTH:SSM_WIDTH + SSM_GROUPS * SSM_STATE]
    cm = xbc_sc[:, SSM_WIDTH + SSM_GROUPS * SSM_STATE:CONV_DIM]

    lane = lax.broadcasted_iota(jnp.int32, (rows, DT_PAD), 1)
    dt = _softplus(col(OFF_DT, DT_PAD) + dtb_ref[...])
    adt = jnp.where(lane < SSM_HEADS, dt * (-jnp.exp(alog_ref[...])), 0.0)
    acum = adt
    for j in range(1, seq):
        acum = acum + jnp.where(tpos >= j, back(adt, j), 0.0)
    a_last = jnp.zeros_like(acum)
    for j in range(seq):
        a_last = a_last + jnp.where(tpos == seq - 1 - j, acum if j == 0 else pltpu.roll(acum, rows - j, 0), 0.0)
    coefs = []
    for j in range(seq):
        cbj = []
        for g in range(SSM_GROUPS):
            gs = slice(g * SSM_STATE, (g + 1) * SSM_STATE)
            cbj.append(jnp.sum(cm[:, gs] * back(bm[:, gs], j), axis=-1, keepdims=True))
        cb_l = jnp.where(lane < HEADS_PER_GROUP, cbj[0], cbj[1])
        valid = tpos >= j
        decay = jnp.exp(jnp.where(valid, acum - back(acum, j), 0.0))
        coefs.append(jnp.where(valid, cb_l * decay * back(dt, j), 0.0))
    coefs.append(jnp.exp(acum))
    coefs.append(dt * jnp.exp(a_last - acum))
    stack = jnp.concatenate(coefs, axis=0)
    hi = stack.astype(BF16)
    lo = (stack - hi.astype(F32)).astype(BF16)
    wide = _dot(hi, expand_ref[...]) + _dot(lo, expand_ref[...])
    y = dsk_ref[...] * xs
    for j in range(seq):
        y = y + wide[j * rows:(j + 1) * rows, :] * back(xs, j)
    e_wide = wide[seq * rows:(seq + 1) * rows, :]
    wx = (xs * wide[(seq + 1) * rows:(seq + 2) * rows, :]).astype(BF16)
    dec_rows = jnp.exp(a_last)

    q = col(OFF_Q, MEM_WIDTH)
    cmb = cm.astype(BF16)
    bmb = bm.astype(BF16)
    for b in range(bblk):
        rs = slice(b * seq, (b + 1) * seq)
        att = _attention_interleaved(
            q[rs, :],
            k_ref[0, b].reshape(MEM_LEN * MEM_HEADS, MEM_HEAD_DIM).astype(BF16),
            v_ref[0, b].reshape(MEM_LEN * MEM_HEADS, MEM_HEAD_DIM).astype(BF16))
        for h in range(MEM_HEADS):
            att_sc[rs, _head_slice(h)] = att[h * seq:(h + 1) * seq, :]
        h0 = ssm_ref[b]
        h0b = h0.astype(BF16)
        dec_sc[...] = jnp.broadcast_to(dec_rows[b * seq + seq - 1:b * seq + seq, :], (LANES, DT_PAD)).T
        for g in range(SSM_GROUPS):
            gs = slice(g * SSM_STATE, (g + 1) * SSM_STATE)
            ws_ = slice(g * GROUP_WIDTH, (g + 1) * GROUP_WIDTH)
            yoff_sc[rs, ws_] = _dot_nt(cmb[rs, gs], h0b[ws_, :])
            upd = _dot_tn(wx[rs, ws_], bmb[rs, gs])
            for kk in range(HEADS_PER_GROUP):
                k = g * HEADS_PER_GROUP + kk
                hs = slice(k * SSM_HEAD_DIM, (k + 1) * SSM_HEAD_DIM)
                ssm_out_ref[b, hs, :] = (h0[hs, :] * dec_sc[k:k + 1, :]
                                         + upd[kk * SSM_HEAD_DIM:(kk + 1) * SSM_HEAD_DIM, :])

    y = (y + yoff_sc[...] * e_wide) * _silu(col(OFF_Z, SSM_WIDTH))
    mix_ref[:, GM_WIDTH:GM_WIDTH + SSM_WIDTH] = _group_rmsnorm(y, sng_ref[...]).astype(BF16)
    mix_ref[:, GM_WIDTH + SSM_WIDTH:MIX_WIDTH] = (att_sc[...] * _silu(col(OFF_MGATE, MEM_WIDTH))).astype(BF16)


def _sample_mix(proj, conv_state, ssm_state, mem_k, mem_v, gmg, gmb, gcoef, gbias, cw, cb, dtb, alog,
                dsk, sng, expand, seq):
    nb = conv_state.shape[0]
    bblk = SAMPLE_BATCH_BLOCK
    rows = bblk * seq
    assert nb % bblk == 0 and rows % (2 * SUBLANES) == 0

    def const(shape):
        return pl.BlockSpec(shape, lambda i: (0,) * len(shape))

    in_specs = [
        pl.BlockSpec((rows, IN_WIDTH), lambda i: (i, 0)),
        pl.BlockSpec((bblk, CONV_WIDTH - 1, CONV_DIM), lambda i: (i, 0, 0)),
        pl.BlockSpec((bblk, SSM_WIDTH, SSM_STATE), lambda i: (i, 0, 0)),
        pl.BlockSpec((1, bblk, MEM_LEN, MEM_HEADS, MEM_HEAD_DIM), lambda i: (0, i, 0, 0, 0)),
        pl.BlockSpec((1, bblk, MEM_LEN, MEM_HEADS, MEM_HEAD_DIM), lambda i: (0, i, 0, 0, 0)),
        const((1, GM_WIDTH)), const((1, GM_WIDTH)),
        const((seq, rows, GM_WIDTH)), const((rows, GM_WIDTH)),
        const((CONV_WIDTH, CONV_DIM)), const((1, CONV_DIM)),
        const((1, DT_PAD)), const((1, DT_PAD)), const((1, SSM_WIDTH)), const((1, SSM_WIDTH)),
        const((DT_PAD, SSM_WIDTH)),
    ]
    out_specs = [
        pl.BlockSpec((rows, MIX_WIDTH), lambda i: (i, 0)),
        pl.BlockSpec((bblk, SSM_WIDTH, SSM_STATE), lambda i: (i, 0, 0)),
        pl.BlockSpec((bblk, CONV_WIDTH - 1, CONV_DIM), lambda i: (i, 0, 0)),
        pl.BlockSpec((rows, GM_WIDTH), lambda i: (i, 0)),
    ]
    out_shape = [
        jax.ShapeDtypeStruct((nb * seq, MIX_WIDTH), BF16),
        jax.ShapeDtypeStruct((nb, SSM_WIDTH, SSM_STATE), F32),
        jax.ShapeDtypeStruct((nb, CONV_WIDTH - 1, CONV_DIM), F32),
        jax.ShapeDtypeStruct((nb * seq, GM_WIDTH), F32),
    ]
    scratch = [
        pltpu.VMEM((bblk, SUBLANES, CONV_DIM), F32),
        pltpu.VMEM((rows, CONV_DIM), F32),
        pltpu.VMEM((rows, MEM_WIDTH), F32),
        pltpu.VMEM((rows, SSM_WIDTH), F32),
        pltpu.VMEM((DT_PAD, LANES), F32),
    ]
    return pl.pallas_call(
        functools.partial(_sample_mix_kernel, bblk=bblk, seq=seq),
        grid=(nb // bblk,),
        in_specs=in_specs,
        out_specs=out_specs,
        out_shape=out_shape,
        scratch_shapes=scratch,
        compiler_params=pltpu.CompilerParams(dimension_semantics=("arbitrary",),
                                             vmem_limit_bytes=VMEM_LIMIT),
        name="sample_mix",
    )(proj, conv_state, ssm_state, mem_k, mem_v, gmg, gmb, gcoef, gbias, cw, cb, dtb, alog, dsk, sng, expand)


def _sample_out_kernel(mix_ref, x_ref, w_ref, g_ref, o_ref):
    nb, seq, d = x_ref.shape
    out = x_ref[...].reshape(nb * seq, d) + _dot(mix_ref[...], w_ref[...])
    o_ref[...] = _rms(out, g_ref[...]).reshape(nb, seq, d)


def _sample_out(mix, x, w_out, g):
    m = x.shape[0] * x.shape[1]
    full = lambda shape: pl.BlockSpec(shape, lambda i: (0,) * len(shape))
    return pl.pallas_call(
        _sample_out_kernel,
        grid=(1,),
        in_specs=[full((m, MIX_WIDTH)), full(x.shape), full((MIX_WIDTH, D_MODEL)), full((1, D_MODEL))],
        out_specs=full(x.shape),
        out_shape=jax.ShapeDtypeStruct(x.shape, F32),
        compiler_params=pltpu.CompilerParams(dimension_semantics=("arbitrary",),
                                             vmem_limit_bytes=VMEM_LIMIT),
        name="sample_out",
    )(mix, x, w_out, g)


def _pad_heads(a):
    return jnp.pad(a.astype(F32), (0, DT_PAD - SSM_HEADS)).reshape(1, DT_PAD)


def kernel(x_prompt, x_sample, mem_prompt, state_ssm, state_conv, cache_mem_k, cache_mem_v, norm_g, w_in,
           gm_norm_g, gm_norm_b, gm_w_spatial, gm_b_spatial, conv_w, conv_b, dt_bias, a_log, d_skip,
           ssm_norm_g, mem_norm_g, w_mem_k, w_mem_v, w_out, final_norm_g):
    assert norm_g.shape[0] == 1, "single layer"
    bp, seq_p, _ = x_prompt.shape
    bs, seq_s, _ = x_sample.shape
    row = lambda a: a.reshape(1, -1).astype(F32)

    w_out_b = w_out[0].astype(BF16)
    ng, gmg, gmb = row(norm_g[0]), row(gm_norm_g[0]), row(gm_norm_b[0])
    cw, cb = conv_w[0].astype(F32), row(conv_b[0])
    dtb, alog = _pad_heads(dt_bias[0]), _pad_heads(a_log[0])
    dsk = row(jnp.repeat(d_skip[0], SSM_HEAD_DIM))
    sng, fg = row(ssm_norm_g[0]), row(final_norm_g)
    w_sp = gm_w_spatial[0]
    tril_p = jnp.tril(jnp.ones((CHUNK, CHUNK), bool))
    ws_p = jnp.where(tril_p, w_sp, 0).astype(BF16)
    bsf_p = jnp.repeat(gm_b_spatial[0].T, GM_HEAD_DIM, axis=1).astype(F32)

    proj_s, w_in_t = _sample_proj(x_sample, ng, jnp.swapaxes(w_in[0], 0, 1))

    mk, mv, mkb, mvb = _memory_kv(mem_prompt, row(mem_norm_g[0]), w_mem_k[0].astype(BF16),
                                  w_mem_v[0].astype(BF16))
    y_p, ssm_p, conv_p = _prompt_layer(x_prompt, mkb, mvb, w_in_t, w_out_b, ng, gmg, gmb, ws_p, bsf_p,
                                       cw, cb, dtb, alog, dsk, sng, fg)

    rows = SAMPLE_BATCH_BLOCK * seq_s
    tpos = jnp.arange(rows) % seq_s
    gcoef = jnp.stack([
        jnp.where((tpos >= j)[:, None],
                  jnp.repeat(w_sp[:, tpos, jnp.maximum(tpos - j, 0)].T, GM_HEAD_DIM, axis=1), 0.0)
        for j in range(seq_s)]).astype(F32)
    gbias = jnp.repeat(gm_b_spatial[0][:, tpos].T, GM_HEAD_DIM, axis=1).astype(F32)
    expand = (jnp.arange(DT_PAD)[:, None] == (jnp.arange(SSM_WIDTH) // SSM_HEAD_DIM)[None, :]).astype(BF16)

    mix_s, ssm_s, conv_s, gv_s = _sample_mix(
        proj_s, state_conv[0], state_ssm[0].reshape(bs, SSM_WIDTH, SSM_STATE),
        cache_mem_k, cache_mem_v,
        gmg, gmb, gcoef, gbias, cw, cb, dtb, alog, dsk, sng, expand, seq_s)
    y_s = _sample_out(mix_s, x_sample, w_out_b, fg)

    return (y_p,
            y_s,
            ssm_p.reshape(1, bp, SSM_HEADS, SSM_HEAD_DIM, SSM_STATE),
            conv_p[None],
            mk,
            mv,
            ssm_s.reshape(1, bs, SSM_HEADS, SSM_HEAD_DIM, SSM_STATE),
            conv_s[None],
            gv_s.reshape(1, bs, seq_s, GM_WIDTH))
```

```python
import functools
import math

import jax
import jax.numpy as jnp
from jax import lax
from jax.experimental import pallas as pl
from jax.experimental.pallas import tpu as pltpu

F32 = jnp.float32
BF16 = jnp.bfloat16

D_MODEL = 1024
GM_WIDTH = 1024
GM_HEADS = 8
GM_HEAD_DIM = 128
CHUNK = 128
SSM_WIDTH = 1024
SSM_HEADS = 16
SSM_HEAD_DIM = 64
SSM_GROUPS = 2
SSM_STATE = 128
HEADS_PER_GROUP = SSM_HEADS // SSM_GROUPS
SSD_TABLES_PER_CHUNK = 4 + 2 * SSM_GROUPS
GROUP_WIDTH = SSM_WIDTH // SSM_GROUPS
CONV_WIDTH = 4
CONV_DIM = SSM_WIDTH + 2 * SSM_GROUPS * SSM_STATE
MEM_LEN = 256
MEM_HEADS = 4
MEM_HEAD_DIM = 256
MEM_WIDTH = 1024
MIX_WIDTH = GM_WIDTH + SSM_WIDTH + MEM_WIDTH
EPS = 1e-6

LANES = 128
SUBLANES = 8

DT_PAD = LANES
OFF_U = 0
OFF_V = OFF_U + GM_WIDTH
OFF_GATE = OFF_V + GM_WIDTH
OFF_Z = OFF_GATE + GM_WIDTH
OFF_XBC = OFF_Z + SSM_WIDTH
OFF_DT = OFF_XBC + CONV_DIM
OFF_Q = OFF_DT + SSM_HEADS
OFF_MGATE = OFF_Q + MEM_WIDTH
IN_WIDTH = OFF_MGATE + MEM_WIDTH
PROJ_BLOCK = 1024

PROMPT_TILE = 256
TILES_PER_STEP = 2
SAMPLE_BATCH_BLOCK = 4
VMEM_LIMIT = 56 * 1024 * 1024


def _rms(x, g):
    return x * lax.rsqrt(jnp.mean(x * x, axis=-1, keepdims=True) + EPS) * g


def _gelu(x):
    return 0.5 * x * (1.0 + lax.erf(x * math.sqrt(0.5)))


def _silu(x):
    h = 0.5 * x
    return h + h * jnp.tanh(h)


def _softplus(x):
    return jnp.maximum(x, 0.0) + jnp.log1p(jnp.exp(-jnp.abs(x)))


def _layernorm(x, g, b):
    mu = jnp.mean(x, axis=-1, keepdims=True)
    xc = x - mu
    var = jnp.mean(xc * xc, axis=-1, keepdims=True)
    return xc * lax.rsqrt(var + EPS) * g + b


def _dot(a, b):
    return jnp.dot(a, b, preferred_element_type=F32)


def _dot_nt(a, b):
    return lax.dot_general(a, b, (((1,), (1,)), ((), ())), preferred_element_type=F32)


def _dot_tn(a, b):
    return lax.dot_general(a, b, (((0,), (0,)), ((), ())), preferred_element_type=F32)


def _group_rmsnorm(y, g):
    halves = []
    for i in range(SSM_GROUPS):
        yg = y[:, i * GROUP_WIDTH:(i + 1) * GROUP_WIDTH]
        halves.append(yg * lax.rsqrt(jnp.mean(yg * yg, axis=-1, keepdims=True) + EPS))
    return jnp.concatenate(halves, axis=-1) * g


def _memory_kv_kernel(mem_ref, g_ref, wk_ref, wv_ref, k_ref, v_ref, kb_ref, vb_ref):
    m = _rms(mem_ref[0], g_ref[...]).astype(BF16)
    k = _dot(m, wk_ref[...])
    v = _dot(m, wv_ref[...])
    for h in range(MEM_HEADS):
        k_ref[0, 0, :, h, :] = k[:, _head_slice(h)]
        v_ref[0, 0, :, h, :] = v[:, _head_slice(h)]
    kb_ref[0] = k.astype(BF16)
    vb_ref[0] = v.astype(BF16)


def _memory_kv(mem, g, wk, wv):
    b = mem.shape[0]
    blk = pl.BlockSpec((1, MEM_LEN, D_MODEL), lambda i: (i, 0, 0))
    blk5 = pl.BlockSpec((1, 1, MEM_LEN, MEM_HEADS, MEM_HEAD_DIM), lambda i: (0, i, 0, 0, 0))
    const = lambda shape: pl.BlockSpec(shape, lambda i: (0,) * len(shape))
    return pl.pallas_call(
        _memory_kv_kernel,
        grid=(b,),
        in_specs=[blk, const((1, D_MODEL)), const((D_MODEL, MEM_WIDTH)), const((D_MODEL, MEM_WIDTH))],
        out_specs=[blk5, blk5, blk, blk],
        out_shape=[jax.ShapeDtypeStruct((1, b, MEM_LEN, MEM_HEADS, MEM_HEAD_DIM), F32)] * 2
        + [jax.ShapeDtypeStruct((b, MEM_LEN, MEM_WIDTH), BF16)] * 2,
        compiler_params=pltpu.CompilerParams(dimension_semantics=("arbitrary",),
                                             vmem_limit_bytes=VMEM_LIMIT),
        name="memory_kv",
    )(mem, g, wk, wv)


def _head_slice(h):
    return slice(h * MEM_HEAD_DIM, (h + 1) * MEM_HEAD_DIM)


def _attention(q, k_head, v_head):
    qb = (q * (MEM_HEAD_DIM ** -0.5)).astype(BF16)
    outs = []
    for h in range(MEM_HEADS):
        s = _dot_nt(qb[:, _head_slice(h)], k_head(h))
        p = jnp.exp(s - jnp.max(s, axis=-1, keepdims=True))
        denom = jnp.sum(p, axis=-1, keepdims=True)
        outs.append(_dot(p.astype(BF16), v_head(h)) / denom)
    return jnp.concatenate(outs, axis=-1)


def _attention_interleaved(q, k2, v2):
    t = q.shape[0]
    qb = (q * (MEM_HEAD_DIM ** -0.5)).astype(BF16)
    q2 = jnp.concatenate([qb[:, _head_slice(h)] for h in range(MEM_HEADS)], axis=0)
    s = _dot_nt(q2, k2)
    row_head = lax.broadcasted_iota(jnp.int32, s.shape, 0) // t
    col_head = lax.broadcasted_iota(jnp.int32, s.shape, 1) % MEM_HEADS
    s = jnp.where(row_head == col_head, s, -jnp.inf)
    p = jnp.exp(s - jnp.max(s, axis=-1, keepdims=True))
    denom = jnp.sum(p, axis=-1, keepdims=True)
    return _dot(p.astype(BF16), v2) / denom


class _Handoff:
    N = 8

    def __init__(self, refs):
        (self.mix, self.ug, self.v, self.zs, self.xs, self.bm, self.cm, self.dt) = refs


def _prompt_step(x_ref, kb_ref, vb_ref, win_ref, wout_ref, ng_ref, gmg_ref, gmb_ref, ws_ref, bsf_ref,
                 cw_ref, cb_ref, dtb_ref, alog_ref, dsk_ref, sng_ref, fg_ref, y_ref,
                 hn_sc, xp_sc, ht_sc, ssd_sc, out_sc, yd_sc, st_sc, ea_sc, buf, tile, rows):
    a_row = -jnp.exp(alog_ref[...])
    row = lax.broadcasted_iota(jnp.int32, (CHUNK, CHUNK), 0)
    col = lax.broadcasted_iota(jnp.int32, (CHUNK, CHUNK), 1)
    causal = row >= col
    tril_b = causal.astype(BF16)
    head_lane = col < SSM_HEADS
    first_half = col < SSM_HEAD_DIM
    def tables(c, g=0):
        base = c * SSD_TABLES_PER_CHUNK
        return (ssd_sc.at[base], ssd_sc.at[base + 1], ssd_sc.at[base + 2], ssd_sc.at[base + 3],
                ssd_sc.at[base + 4 + 2 * g], ssd_sc.at[base + 5 + 2 * g])

    def proj(off, width):
        return _dot_nt(hn_sc[...], win_ref[off:off + width, :])

    def gmlp_mix(c):
        r = slice(c * CHUNK, (c + 1) * CHUNK)
        for h in range(GM_HEADS):
            hs = slice(h * GM_HEAD_DIM, (h + 1) * GM_HEAD_DIM)
            mixed = _dot(ws_ref[h], buf.v[r, hs]) + bsf_ref[:, hs]
            buf.mix[r, hs] = (buf.ug[r, hs] * mixed).astype(BF16)

    def seq_chunk_start(c):
        r = slice(c * CHUNK, (c + 1) * CHUNK)
        acum_sc, acum_t_sc, dt_t_sc, to_end_t_sc, _, _ = tables(c)
        dt = buf.dt[r, :]
        adt = jnp.where(head_lane, dt * a_row, 0.0)
        hi = adt.astype(BF16)
        rest = adt - hi.astype(F32)
        mid = rest.astype(BF16)
        lo = (rest - mid.astype(F32)).astype(BF16)
        acum = _dot(tril_b, hi) + _dot(tril_b, mid) + _dot(tril_b, lo)
        acum_t = acum.T
        dt_t = dt.T
        acum_sc[...] = acum
        acum_t_sc[...] = acum_t
        dt_t_sc[...] = dt_t
        to_end_t_sc[...] = jnp.exp(acum_t[:, CHUNK - 1:CHUNK] - acum_t) * dt_t
        for g in range(SSM_GROUPS):
            gs = slice(g * SSM_STATE, (g + 1) * SSM_STATE)
            _, _, _, _, cb_sc, bm_t_sc = tables(c, g)
            bm_g = buf.bm[r, gs]
            cb_sc[...] = _dot_nt(buf.cm[r, gs].astype(BF16), bm_g.astype(BF16))
            bm_t_sc[...] = bm_g.T

    def seq_group(c, g):
        r = slice(c * CHUNK, (c + 1) * CHUNK)
        acum_sc, acum_t_sc, dt_t_sc, to_end_t_sc, cb_sc, bm_t_sc = tables(c, g)
        for kk in range(0, HEADS_PER_GROUP, 2):
            k0 = g * HEADS_PER_GROUP + kk
            ps = slice(k0 * SSM_HEAD_DIM, (k0 + 2) * SSM_HEAD_DIM)
            m_parts, b_parts, ea = [], [], []
            for k in (k0, k0 + 1):
                a_col = jnp.broadcast_to(acum_sc[:, k:k + 1], (CHUNK, CHUNK))
                diff = a_col - acum_t_sc[k:k + 1, :]
                decay = jnp.exp(jnp.where(causal, diff, -jnp.inf))
                m_parts.append((cb_sc[...] * decay * dt_t_sc[k:k + 1, :]).astype(BF16))
                b_parts.append((bm_t_sc[...] * to_end_t_sc[k:k + 1, :]).astype(BF16))
                ea.append(jnp.exp(a_col))
            xs_pair = buf.xs[r, ps]
            zero = jnp.zeros_like(xs_pair)
            xs_lo = jnp.where(first_half, xs_pair, zero).astype(BF16)
            xs_hi = jnp.where(first_half, zero, xs_pair).astype(BF16)
            lhs = jnp.concatenate([jnp.concatenate(m_parts, axis=1), jnp.concatenate(b_parts, axis=1)], axis=0)
            both = _dot(lhs, jnp.concatenate([xs_lo, xs_hi], axis=0))
            yd_sc[c, :, ps] = both[0:CHUNK, :]
            st_sc[c, :, ps] = both[CHUNK:2 * CHUNK, :]
            ea_sc[c, :, ps] = jnp.where(first_half, ea[0], ea[1])

    def seq_chunk_end(c):
        r = slice(c * CHUNK, (c + 1) * CHUNK)
        h_b = ht_sc[...].astype(BF16)
        y_off = jnp.concatenate(
            [_dot(buf.cm[r, g * SSM_STATE:(g + 1) * SSM_STATE].astype(BF16), h_b[:, g * GROUP_WIDTH:(g + 1) * GROUP_WIDTH])
             for g in range(SSM_GROUPS)], axis=1)
        ea_c = ea_sc[c]
        y = yd_sc[c] + y_off * ea_c
        ht_sc[...] = ht_sc[...] * ea_c[CHUNK - 1:CHUNK, :] + st_sc[c]
        y = (y + dsk_ref[...] * buf.xs[r, :]) * buf.zs[r, :]
        buf.mix[r, GM_WIDTH:GM_WIDTH + SSM_WIDTH] = _group_rmsnorm(y, sng_ref[...]).astype(BF16)

    def out_piece(j, n):
        cs = slice(j * D_MODEL // n, (j + 1) * D_MODEL // n)
        out_sc[:, cs] = x_ref[0, rows, cs] + _dot(buf.mix[...], wout_ref[:, cs])

    assert tile // CHUNK == 2, "the emission order below is written for two chunks per tile"

    hn_sc[...] = _rms(x_ref[0, rows, :], ng_ref[...]).astype(BF16)

    buf.zs[...] = _silu(proj(OFF_Z, SSM_WIDTH))
    xp_sc[SUBLANES:SUBLANES + tile, :] = proj(OFF_XBC, CONV_DIM)
    ext = xp_sc[...].reshape(tile // SUBLANES + 1, SUBLANES, CONV_DIM)
    sub = lax.broadcasted_iota(jnp.int32, (1, SUBLANES, CONV_DIM), 1)
    acc = cb_ref[...] + cw_ref[CONV_WIDTH - 1:CONV_WIDTH, :] * ext[1:]
    for shift in range(1, CONV_WIDTH):
        rot = pltpu.roll(ext, shift, 1)
        shifted = jnp.where(sub >= shift, rot[1:], rot[:-1])
        acc = acc + cw_ref[CONV_WIDTH - 1 - shift:CONV_WIDTH - shift, :] * shifted
    acc = acc.reshape(tile, CONV_DIM)
    xbc = _silu(acc)
    buf.xs[...] = xbc[:, :SSM_WIDTH]
    buf.bm[...] = xbc[:, SSM_WIDTH:SSM_WIDTH + SSM_GROUPS * SSM_STATE]
    buf.cm[...] = xbc[:, SSM_WIDTH + SSM_GROUPS * SSM_STATE:]
    xp_sc[SUBLANES - (CONV_WIDTH - 1):SUBLANES, :] = xp_sc[tile + SUBLANES - (CONV_WIDTH - 1):tile + SUBLANES, :]
    buf.dt[...] = _softplus(proj(OFF_DT, DT_PAD) + dtb_ref[...])
    u = _gelu(proj(OFF_U, GM_WIDTH))
    buf.ug[...] = u * _silu(proj(OFF_GATE, GM_WIDTH))
    buf.v[...] = _layernorm(_gelu(proj(OFF_V, GM_WIDTH)), gmg_ref[...], gmb_ref[...]).astype(BF16)
    seq_chunk_start(0)
    seq_chunk_start(1)
    gmlp_mix(0)
    gmlp_mix(1)
    seq_group(0, 0)
    seq_group(0, 1)
    seq_chunk_end(0)
    att = _attention(proj(OFF_Q, MEM_WIDTH), lambda h: kb_ref[0, :, _head_slice(h)], lambda h: vb_ref[0, :, _head_slice(h)])
    buf.mix[:, GM_WIDTH + SSM_WIDTH:MIX_WIDTH] = (att * _silu(proj(OFF_MGATE, MEM_WIDTH))).astype(BF16)
    seq_group(1, 0)
    seq_group(1, 1)
    seq_chunk_end(1)

    n_out = 4
    for j in range(n_out):
        out_piece(j, n_out)
    y_ref[0, rows, :] = _rms(out_sc[...], fg_ref[...])


def _prompt_kernel(x_ref, kb_ref, vb_ref, win_ref, wout_ref, ng_ref, gmg_ref, gmb_ref,
                   ws_ref, bsf_ref, cw_ref, cb_ref, dtb_ref, alog_ref, dsk_ref, sng_ref, fg_ref,
                   y_ref, ssm_ref, conv_ref, hn_sc, xpad_sc, ht_sc, ssd_sc, out_sc, yd_sc, st_sc, ea_sc, *handoff, tile, nt, total):
    t = pl.program_id(0) % nt
    tile_set = _Handoff(handoff)

    @pl.when(t == 0)
    def _():
        xpad_sc[0:SUBLANES, :] = jnp.zeros((SUBLANES, CONV_DIM), F32)
        ht_sc[...] = jnp.zeros_like(ht_sc)

    for i in range(TILES_PER_STEP):
        _prompt_step(x_ref, kb_ref, vb_ref, win_ref, wout_ref, ng_ref, gmg_ref, gmb_ref, ws_ref, bsf_ref,
                     cw_ref, cb_ref, dtb_ref, alog_ref, dsk_ref, sng_ref, fg_ref, y_ref,
                     hn_sc, xpad_sc, ht_sc, ssd_sc, out_sc, yd_sc, st_sc, ea_sc, tile_set, tile,
                     slice(i * tile, (i + 1) * tile))

    @pl.when(t == nt - 1)
    def _():
        conv_ref[0] = xpad_sc[SUBLANES - (CONV_WIDTH - 1):SUBLANES, :]
        ssm_ref[0] = ht_sc[...].T


def _prompt_layer(x, kb, vb, w_in_t, w_out, ng, gmg, gmb, ws, bsf, cw, cb, dtb, alog, dsk, sng, fg):
    b, seq, _ = x.shape
    tile = PROMPT_TILE
    step_rows = tile * TILES_PER_STEP
    nt = seq // step_rows
    total = b * nt
    assert seq % step_rows == 0 and tile % CHUNK == 0

    def const(shape):
        return pl.BlockSpec(shape, lambda s: (0,) * len(shape), pipeline_mode=pl.Buffered(1))

    in_specs = [
        pl.BlockSpec((1, step_rows, D_MODEL), lambda s: (s // nt, s % nt, 0)),
        pl.BlockSpec((1, MEM_LEN, MEM_WIDTH), lambda s: (s // nt, 0, 0)),
        pl.BlockSpec((1, MEM_LEN, MEM_WIDTH), lambda s: (s // nt, 0, 0)),
        const((IN_WIDTH, D_MODEL)),
        const((MIX_WIDTH, D_MODEL)),
        const((1, D_MODEL)), const((1, GM_WIDTH)), const((1, GM_WIDTH)),
        const((GM_HEADS, CHUNK, CHUNK)), const((CHUNK, GM_WIDTH)),
        const((CONV_WIDTH, CONV_DIM)), const((1, CONV_DIM)),
        const((1, DT_PAD)), const((1, DT_PAD)), const((1, SSM_WIDTH)), const((1, SSM_WIDTH)),
        const((1, D_MODEL)),
    ]
    out_specs = [
        pl.BlockSpec((1, step_rows, D_MODEL), lambda s: (s // nt, s % nt, 0)),
        pl.BlockSpec((1, SSM_WIDTH, SSM_STATE), lambda s: (s // nt, 0, 0)),
        pl.BlockSpec((1, CONV_WIDTH - 1, CONV_DIM), lambda s: (s // nt, 0, 0)),
    ]
    out_shape = [
        jax.ShapeDtypeStruct((b, seq, D_MODEL), F32),
        jax.ShapeDtypeStruct((b, SSM_WIDTH, SSM_STATE), F32),
        jax.ShapeDtypeStruct((b, CONV_WIDTH - 1, CONV_DIM), F32),
    ]
    handoff = [
        pltpu.VMEM((tile, MIX_WIDTH), BF16),
        pltpu.VMEM((tile, GM_WIDTH), F32),
        pltpu.VMEM((tile, GM_WIDTH), BF16),
        pltpu.VMEM((tile, SSM_WIDTH), F32),
        pltpu.VMEM((tile, SSM_WIDTH), F32),
        pltpu.VMEM((tile, SSM_GROUPS * SSM_STATE), F32),
        pltpu.VMEM((tile, SSM_GROUPS * SSM_STATE), F32),
        pltpu.VMEM((tile, DT_PAD), F32),
    ]
    assert len(handoff) == _Handoff.N
    scratch = [
        pltpu.VMEM((tile, D_MODEL), BF16),
        pltpu.VMEM((tile + SUBLANES, CONV_DIM), F32),
        pltpu.VMEM((SSM_STATE, SSM_WIDTH), F32),
        pltpu.VMEM((tile // CHUNK * SSD_TABLES_PER_CHUNK, CHUNK, CHUNK), F32),
        pltpu.VMEM((tile, D_MODEL), F32),
        pltpu.VMEM((tile // CHUNK, CHUNK, SSM_WIDTH), F32),
        pltpu.VMEM((tile // CHUNK, SSM_STATE, SSM_WIDTH), F32),
        pltpu.VMEM((tile // CHUNK, CHUNK, SSM_WIDTH), F32),
    ] + handoff
    return pl.pallas_call(
        functools.partial(_prompt_kernel, tile=tile, nt=nt, total=total),
        grid=(total,),
        in_specs=in_specs,
        out_specs=out_specs,
        out_shape=out_shape,
        scratch_shapes=scratch,
        compiler_params=pltpu.CompilerParams(dimension_semantics=("arbitrary",),
                                             vmem_limit_bytes=VMEM_LIMIT),
        name="prompt_layer",
    )(x, kb, vb, w_in_t, w_out, ng, gmg, gmb, ws, bsf, cw, cb, dtb, alog, dsk, sng, fg)


def _sample_proj_kernel(x_ref, g_ref, w_ref, o_ref, wb_ref):
    nb, seq, d = x_ref.shape
    hn = _rms(x_ref[...].reshape(nb * seq, d), g_ref[...]).astype(BF16)
    wb = w_ref[...].astype(BF16)
    wb_ref[...] = wb
    o_ref[...] = _dot_nt(hn, wb)


def _sample_proj(x, g, w_t):
    m, n = x.shape[0] * x.shape[1], w_t.shape[0]
    return pl.pallas_call(
        _sample_proj_kernel,
        grid=(pl.cdiv(n, PROJ_BLOCK),),
        in_specs=[pl.BlockSpec(x.shape, lambda j: (0, 0, 0)),
                  pl.BlockSpec((1, D_MODEL), lambda j: (0, 0)),
                  pl.BlockSpec((PROJ_BLOCK, D_MODEL), lambda j: (j, 0))],
        out_specs=[pl.BlockSpec((m, PROJ_BLOCK), lambda j: (0, j)),
                   pl.BlockSpec((PROJ_BLOCK, D_MODEL), lambda j: (j, 0))],
        out_shape=[jax.ShapeDtypeStruct((m, n), F32), jax.ShapeDtypeStruct((n, D_MODEL), BF16)],
        compiler_params=pltpu.CompilerParams(dimension_semantics=("arbitrary",),
                                             vmem_limit_bytes=VMEM_LIMIT),
        name="sample_proj",
    )(x, g, w_t)


def _sample_mix_kernel(p_ref, cst_ref, ssm_ref, k_ref, v_ref, gmg_ref, gmb_ref, gcoef_ref, gbias_ref,
                       cw_ref, cb_ref, dtb_ref, alog_ref, dsk_ref, sng_ref, expand_ref,
                       mix_ref, ssm_out_ref, conv_out_ref, gv_ref,
                       xp_sc, xbc_sc, att_sc, yoff_sc, dec_sc, *, bblk, seq):
    rows = bblk * seq

    def col(off, width):
        return p_ref[:, off:off + width]

    tpos = lax.broadcasted_iota(jnp.int32, (rows, 1), 0) % seq

    def back(a, j):
        return a if j == 0 else pltpu.roll(a, j, 0)

    u = _gelu(col(OFF_U, GM_WIDTH))
    v = _layernorm(_gelu(col(OFF_V, GM_WIDTH)), gmg_ref[...], gmb_ref[...])
    gv_ref[...] = v
    mixed = gbias_ref[...]
    for j in range(seq):
        mixed = mixed + gcoef_ref[j] * back(v, j)
    mix_ref[:, 0:GM_WIDTH] = (u * mixed * _silu(col(OFF_GATE, GM_WIDTH))).astype(BF16)

    xbc_raw = col(OFF_XBC, CONV_DIM)
    for b in range(bblk):
        xp_sc[b, 0:CONV_WIDTH - 1, :] = cst_ref[b]
        xp_sc[b, CONV_WIDTH - 1:CONV_WIDTH - 1 + seq, :] = xbc_raw[b * seq:(b + 1) * seq, :]
    for b in range(bblk):
        acc = jnp.broadcast_to(cb_ref[...], (seq, CONV_DIM))
        for j in range(CONV_WIDTH):
            acc = acc + cw_ref[j:j + 1, :] * xp_sc[b, j:j + seq, :]
        xbc_sc[b * seq:(b + 1) * seq, :] = _silu(acc)
        conv_out_ref[b] = xp_sc[b, seq:seq + CONV_WIDTH - 1, :]
    xs = xbc_sc[:, 0:SSM_WIDTH]
    bm = xbc_sc[:, SSM_WIDTH:SSM_WIDTH + SSM_GROUPS * SSM_STATE]
    cm = xbc_sc[:, SSM_WIDTH + SSM_GROUPS * SSM_STATE:CONV_DIM]

    lane = lax.broadcasted_iota(jnp.int32, (rows, DT_PAD), 1)
    dt = _softplus(col(OFF_DT, DT_PAD) + dtb_ref[...])
    adt = jnp.where(lane < SSM_HEADS, dt * (-jnp.exp(alog_ref[...])), 0.0)
    acum = adt
    for j in range(1, seq):
        acum = acum + jnp.where(tpos >= j, back(adt, j), 0.0)
    a_last = jnp.zeros_like(acum)
    for j in range(seq):
        a_last = a_last + jnp.where(tpos == seq - 1 - j, acum if j == 0 else pltpu.roll(acum, rows - j, 0), 0.0)
    coefs = []
    for j in range(seq):
        cbj = []
        for g in range(SSM_GROUPS):
            gs = slice(g * SSM_STATE, (g + 1) * SSM_STATE)
            cbj.append(jnp.sum(cm[:, gs] * back(bm[:, gs], j), axis=-1, keepdims=True))
        cb_l = jnp.where(lane < HEADS_PER_GROUP, cbj[0], cbj[1])
        valid = tpos >= j
        decay = jnp.exp(jnp.where(valid, acum - back(acum, j), 0.0))
        coefs.append(jnp.where(valid, cb_l * decay * back(dt, j), 0.0))
    coefs.append(jnp.exp(acum))
    coefs.append(dt * jnp.exp(a_last - acum))
    stack = jnp.concatenate(coefs, axis=0)
    hi = stack.astype(BF16)
    lo = (stack - hi.astype(F32)).astype(BF16)
    wide = _dot(hi, expand_ref[...]) + _dot(lo, expand_ref[...])
    y = dsk_ref[...] * xs
    for j in range(seq):
        y = y + wide[j * rows:(j + 1) * rows, :] * back(xs, j)
    e_wide = wide[seq * rows:(seq + 1) * rows, :]
    wx = (xs * wide[(seq + 1) * rows:(seq + 2) * rows, :]).astype(BF16)
    dec_rows = jnp.exp(a_last)

    q = col(OFF_Q, MEM_WIDTH)
    cmb = cm.astype(BF16)
    bmb = bm.astype(BF16)
    for b in range(bblk):
        rs = slice(b * seq, (b + 1) * seq)
        att = _attention_interleaved(
            q[rs, :],
            k_ref[0, b].reshape(MEM_LEN * MEM_HEADS, MEM_HEAD_DIM).astype(BF16),
            v_ref[0, b].reshape(MEM_LEN * MEM_HEADS, MEM_HEAD_DIM).astype(BF16))
        for h in range(MEM_HEADS):
            att_sc[rs, _head_slice(h)] = att[h * seq:(h + 1) * seq, :]
        h0 = ssm_ref[b]
        h0b = h0.astype(BF16)
        dec_sc[...] = jnp.broadcast_to(dec_rows[b * seq + seq - 1:b * seq + seq, :], (LANES, DT_PAD)).T
        for g in range(SSM_GROUPS):
            gs = slice(g * SSM_STATE, (g + 1) * SSM_STATE)
            ws_ = slice(g * GROUP_WIDTH, (g + 1) * GROUP_WIDTH)
            yoff_sc[rs, ws_] = _dot_nt(cmb[rs, gs], h0b[ws_, :])
            upd = _dot_tn(wx[rs, ws_], bmb[rs, gs])
            for kk in range(HEADS_PER_GROUP):
                k = g * HEADS_PER_GROUP + kk
                hs = slice(k * SSM_HEAD_DIM, (k + 1) * SSM_HEAD_DIM)
                ssm_out_ref[b, hs, :] = (h0[hs, :] * dec_sc[k:k + 1, :]
                                         + upd[kk * SSM_HEAD_DIM:(kk + 1) * SSM_HEAD_DIM, :])

    y = (y + yoff_sc[...] * e_wide) * _silu(col(OFF_Z, SSM_WIDTH))
    mix_ref[:, GM_WIDTH:GM_WIDTH + SSM_WIDTH] = _group_rmsnorm(y, sng_ref[...]).astype(BF16)
    mix_ref[:, GM_WIDTH + SSM_WIDTH:MIX_WIDTH] = (att_sc[...] * _silu(col(OFF_MGATE, MEM_WIDTH))).astype(BF16)


def _sample_mix(proj, conv_state, ssm_state, mem_k, mem_v, gmg, gmb, gcoef, gbias, cw, cb, dtb, alog,
                dsk, sng, expand, seq):
    nb = conv_state.shape[0]
    bblk = SAMPLE_BATCH_BLOCK
    rows = bblk * seq
    assert nb % bblk == 0 and rows % (2 * SUBLANES) == 0

    def const(shape):
        return pl.BlockSpec(shape, lambda i: (0,) * len(shape))

    in_specs = [
        pl.BlockSpec((rows, IN_WIDTH), lambda i: (i, 0)),
        pl.BlockSpec((bblk, CONV_WIDTH - 1, CONV_DIM), lambda i: (i, 0, 0)),
        pl.BlockSpec((bblk, SSM_WIDTH, SSM_STATE), lambda i: (i, 0, 0)),
        pl.BlockSpec((1, bblk, MEM_LEN, MEM_HEADS, MEM_HEAD_DIM), lambda i: (0, i, 0, 0, 0)),
        pl.BlockSpec((1, bblk, MEM_LEN, MEM_HEADS, MEM_HEAD_DIM), lambda i: (0, i, 0, 0, 0)),
        const((1, GM_WIDTH)), const((1, GM_WIDTH)),
        const((seq, rows, GM_WIDTH)), const((rows, GM_WIDTH)),
        const((CONV_WIDTH, CONV_DIM)), const((1, CONV_DIM)),
        const((1, DT_PAD)), const((1, DT_PAD)), const((1, SSM_WIDTH)), const((1, SSM_WIDTH)),
        const((DT_PAD, SSM_WIDTH)),
    ]
    out_specs = [
        pl.BlockSpec((rows, MIX_WIDTH), lambda i: (i, 0)),
        pl.BlockSpec((bblk, SSM_WIDTH, SSM_STATE), lambda i: (i, 0, 0)),
        pl.BlockSpec((bblk, CONV_WIDTH - 1, CONV_DIM), lambda i: (i, 0, 0)),
        pl.BlockSpec((rows, GM_WIDTH), lambda i: (i, 0)),
    ]
    out_shape = [
        jax.ShapeDtypeStruct((nb * seq, MIX_WIDTH), BF16),
        jax.ShapeDtypeStruct((nb, SSM_WIDTH, SSM_STATE), F32),
        jax.ShapeDtypeStruct((nb, CONV_WIDTH - 1, CONV_DIM), F32),
        jax.ShapeDtypeStruct((nb * seq, GM_WIDTH), F32),
    ]
    scratch = [
        pltpu.VMEM((bblk, SUBLANES, CONV_DIM), F32),
        pltpu.VMEM((rows, CONV_DIM), F32),
        pltpu.VMEM((rows, MEM_WIDTH), F32),
        pltpu.VMEM((rows, SSM_WIDTH), F32),
        pltpu.VMEM((DT_PAD, LANES), F32),
    ]
    return pl.pallas_call(
        functools.partial(_sample_mix_kernel, bblk=bblk, seq=seq),
        grid=(nb // bblk,),
        in_specs=in_specs,
        out_specs=out_specs,
        out_shape=out_shape,
        scratch_shapes=scratch,
        compiler_params=pltpu.CompilerParams(dimension_semantics=("arbitrary",),
                                             vmem_limit_bytes=VMEM_LIMIT),
        name="sample_mix",
    )(proj, conv_state, ssm_state, mem_k, mem_v, gmg, gmb, gcoef, gbias, cw, cb, dtb, alog, dsk, sng, expand)


def _sample_out_kernel(mix_ref, x_ref, w_ref, g_ref, o_ref):
    nb, seq, d = x_ref.shape
    out = x_ref[...].reshape(nb * seq, d) + _dot(mix_ref[...], w_ref[...])
    o_ref[...] = _rms(out, g_ref[...]).reshape(nb, seq, d)


def _sample_out(mix, x, w_out, g):
    m = x.shape[0] * x.shape[1]
    full = lambda shape: pl.BlockSpec(shape, lambda i: (0,) * len(shape))
    return pl.pallas_call(
        _sample_out_kernel,
        grid=(1,),
        in_specs=[full((m, MIX_WIDTH)), full(x.shape), full((MIX_WIDTH, D_MODEL)), full((1, D_MODEL))],
        out_specs=full(x.shape),
        out_shape=jax.ShapeDtypeStruct(x.shape, F32),
        compiler_params=pltpu.CompilerParams(dimension_semantics=("arbitrary",),
                                             vmem_limit_bytes=VMEM_LIMIT),
        name="sample_out",
    )(mix, x, w_out, g)


def _pad_heads(a):
    return jnp.pad(a.astype(F32), (0, DT_PAD - SSM_HEADS)).reshape(1, DT_PAD)


def kernel(x_prompt, x_sample, mem_prompt, state_ssm, state_conv, cache_mem_k, cache_mem_v, norm_g, w_in,
           gm_norm_g, gm_norm_b, gm_w_spatial, gm_b_spatial, conv_w, conv_b, dt_bias, a_log, d_skip,
           ssm_norm_g, mem_norm_g, w_mem_k, w_mem_v, w_out, final_norm_g):
    assert norm_g.shape[0] == 1, "single layer"
    bp, seq_p, _ = x_prompt.shape
    bs, seq_s, _ = x_sample.shape
    row = lambda a: a.reshape(1, -1).astype(F32)

    w_out_b = w_out[0].astype(BF16)
    ng, gmg, gmb = row(norm_g[0]), row(gm_norm_g[0]), row(gm_norm_b[0])
    cw, cb = conv_w[0].astype(F32), row(conv_b[0])
    dtb, alog = _pad_heads(dt_bias[0]), _pad_heads(a_log[0])
    dsk = row(jnp.repeat(d_skip[0], SSM_HEAD_DIM))
    sng, fg = row(ssm_norm_g[0]), row(final_norm_g)
    w_sp = gm_w_spatial[0]
    tril_p = jnp.tril(jnp.ones((CHUNK, CHUNK), bool))
    ws_p = jnp.where(tril_p, w_sp, 0).astype(BF16)
    bsf_p = jnp.repeat(gm_b_spatial[0].T, GM_HEAD_DIM, axis=1).astype(F32)

    proj_s, w_in_t = _sample_proj(x_sample, ng, jnp.swapaxes(w_in[0], 0, 1))

    mk, mv, mkb, mvb = _memory_kv(mem_prompt, row(mem_norm_g[0]), w_mem_k[0].astype(BF16),
                                  w_mem_v[0].astype(BF16))
    y_p, ssm_p, conv_p = _prompt_layer(x_prompt, mkb, mvb, w_in_t, w_out_b, ng, gmg, gmb, ws_p, bsf_p,
                                       cw, cb, dtb, alog, dsk, sng, fg)

    rows = SAMPLE_BATCH_BLOCK * seq_s
    tpos = jnp.arange(rows) % seq_s
    w_corner = w_sp[:, :seq_s, :seq_s]
    gcoef = jnp.stack([
        jnp.where((tpos >= j)[:, None],
                  jnp.repeat(w_corner[:, tpos, jnp.maximum(tpos - j, 0)].T, GM_HEAD_DIM, axis=1), 0.0)
        for j in range(seq_s)]).astype(F32)
    gbias = jnp.repeat(gm_b_spatial[0][:, tpos].T, GM_HEAD_DIM, axis=1).astype(F32)
    expand = (jnp.arange(DT_PAD)[:, None] == (jnp.arange(SSM_WIDTH) // SSM_HEAD_DIM)[None, :]).astype(BF16)

    mix_s, ssm_s, conv_s, gv_s = _sample_mix(
        proj_s, state_conv[0], state_ssm[0].reshape(bs, SSM_WIDTH, SSM_STATE),
        cache_mem_k, cache_mem_v,
        gmg, gmb, gcoef, gbias, cw, cb, dtb, alog, dsk, sng, expand, seq_s)
    y_s = _sample_out(mix_s, x_sample, w_out_b, fg)

    return (y_p,
            y_s,
            ssm_p.reshape(1, bp, SSM_HEADS, SSM_HEAD_DIM, SSM_STATE),
            conv_p[None],
            mk,
            mv,
            ssm_s.reshape(1, bs, SSM_HEADS, SSM_HEAD_DIM, SSM_STATE),
            conv_s[None],
            gv_s.reshape(1, bs, seq_s, GM_WIDTH))
```

```python
import functools
import math

import jax
import jax.numpy as jnp
from jax import lax
from jax.experimental import pallas as pl
from jax.experimental.pallas import tpu as pltpu

F32 = jnp.float32
BF16 = jnp.bfloat16

D_MODEL = 1024
GM_WIDTH = 1024
GM_HEADS = 8
GM_HEAD_DIM = 128
CHUNK = 128
SSM_WIDTH = 1024
SSM_HEADS = 16
SSM_HEAD_DIM = 64
SSM_GROUPS = 2
SSM_STATE = 128
HEADS_PER_GROUP = SSM_HEADS // SSM_GROUPS
SSD_TABLES_PER_CHUNK = 4 + 2 * SSM_GROUPS
GROUP_WIDTH = SSM_WIDTH // SSM_GROUPS
CONV_WIDTH = 4
CONV_DIM = SSM_WIDTH + 2 * SSM_GROUPS * SSM_STATE
MEM_LEN = 256
MEM_HEADS = 4
MEM_HEAD_DIM = 256
MEM_WIDTH = 1024
MIX_WIDTH = GM_WIDTH + SSM_WIDTH + MEM_WIDTH
EPS = 1e-6

LANES = 128
SUBLANES = 8

DT_PAD = LANES
OFF_U = 0
OFF_V = OFF_U + GM_WIDTH
OFF_GATE = OFF_V + GM_WIDTH
OFF_Z = OFF_GATE + GM_WIDTH
OFF_XBC = OFF_Z + SSM_WIDTH
OFF_DT = OFF_XBC + CONV_DIM
OFF_Q = OFF_DT + SSM_HEADS
OFF_MGATE = OFF_Q + MEM_WIDTH
IN_WIDTH = OFF_MGATE + MEM_WIDTH
PROJ_BLOCK = 1024

PROMPT_TILE = 256
TILES_PER_STEP = 2
SAMPLE_BATCH_BLOCK = 4
VMEM_LIMIT = 56 * 1024 * 1024


def _rms(x, g):
    return x * lax.rsqrt(jnp.mean(x * x, axis=-1, keepdims=True) + EPS) * g


def _gelu(x):
    return 0.5 * x * (1.0 + lax.erf(x * math.sqrt(0.5)))


def _silu(x):
    h = 0.5 * x
    return h + h * jnp.tanh(h)


def _softplus(x):
    return jnp.maximum(x, 0.0) + jnp.log1p(jnp.exp(-jnp.abs(x)))


def _layernorm(x, g, b):
    mu = jnp.mean(x, axis=-1, keepdims=True)
    xc = x - mu
    var = jnp.mean(xc * xc, axis=-1, keepdims=True)
    return xc * lax.rsqrt(var + EPS) * g + b


def _dot(a, b):
    return jnp.dot(a, b, preferred_element_type=F32)


def _dot_nt(a, b):
    return lax.dot_general(a, b, (((1,), (1,)), ((), ())), preferred_element_type=F32)


def _dot_tn(a, b):
    return lax.dot_general(a, b, (((0,), (0,)), ((), ())), preferred_element_type=F32)


def _group_rmsnorm(y, g):
    halves = []
    for i in range(SSM_GROUPS):
        yg = y[:, i * GROUP_WIDTH:(i + 1) * GROUP_WIDTH]
        halves.append(yg * lax.rsqrt(jnp.mean(yg * yg, axis=-1, keepdims=True) + EPS))
    return jnp.concatenate(halves, axis=-1) * g


def _memory_kv_kernel(mem_ref, g_ref, wk_ref, wv_ref, k_ref, v_ref, kb_ref, vb_ref, wkb_sc, wvb_sc):
    @pl.when(pl.program_id(0) == 0)
    def _():
        wkb_sc[...] = wk_ref[...].astype(BF16)
        wvb_sc[...] = wv_ref[...].astype(BF16)

    m = _rms(mem_ref[0], g_ref[...]).astype(BF16)
    k = _dot(m, wkb_sc[...])
    v = _dot(m, wvb_sc[...])
    for h in range(MEM_HEADS):
        k_ref[0, 0, :, h, :] = k[:, _head_slice(h)]
        v_ref[0, 0, :, h, :] = v[:, _head_slice(h)]
    kb_ref[0] = k.astype(BF16)
    vb_ref[0] = v.astype(BF16)


def _memory_kv(mem, g, wk, wv):
    b = mem.shape[0]
    blk = pl.BlockSpec((1, MEM_LEN, D_MODEL), lambda i: (i, 0, 0))
    blk5 = pl.BlockSpec((1, 1, MEM_LEN, MEM_HEADS, MEM_HEAD_DIM), lambda i: (0, i, 0, 0, 0))
    const = lambda shape: pl.BlockSpec(shape, lambda i: (0,) * len(shape), pipeline_mode=pl.Buffered(1))
    return pl.pallas_call(
        _memory_kv_kernel,
        grid=(b,),
        in_specs=[blk, const((1, D_MODEL)), const((D_MODEL, MEM_WIDTH)), const((D_MODEL, MEM_WIDTH))],
        out_specs=[blk5, blk5, blk, blk],
        out_shape=[jax.ShapeDtypeStruct((1, b, MEM_LEN, MEM_HEADS, MEM_HEAD_DIM), F32)] * 2
        + [jax.ShapeDtypeStruct((b, MEM_LEN, MEM_WIDTH), BF16)] * 2,
        scratch_shapes=[pltpu.VMEM((D_MODEL, MEM_WIDTH), BF16)] * 2,
        compiler_params=pltpu.CompilerParams(dimension_semantics=("arbitrary",),
                                             vmem_limit_bytes=VMEM_LIMIT),
        name="memory_kv",
    )(mem, g, wk, wv)


def _head_slice(h):
    return slice(h * MEM_HEAD_DIM, (h + 1) * MEM_HEAD_DIM)


def _attention(q, k_head, v_head):
    qb = (q * (MEM_HEAD_DIM ** -0.5)).astype(BF16)
    outs = []
    for h in range(MEM_HEADS):
        s = _dot_nt(qb[:, _head_slice(h)], k_head(h))
        p = jnp.exp(s - jnp.max(s, axis=-1, keepdims=True))
        denom = jnp.sum(p, axis=-1, keepdims=True)
        outs.append(_dot(p.astype(BF16), v_head(h)) / denom)
    return jnp.concatenate(outs, axis=-1)


def _attention_interleaved(q, k2, v2):
    t = q.shape[0]
    qb = (q * (MEM_HEAD_DIM ** -0.5)).astype(BF16)
    q2 = jnp.concatenate([qb[:, _head_slice(h)] for h in range(MEM_HEADS)], axis=0)
    s = _dot_nt(q2, k2)
    row_head = lax.broadcasted_iota(jnp.int32, s.shape, 0) // t
    col_head = lax.broadcasted_iota(jnp.int32, s.shape, 1) % MEM_HEADS
    s = jnp.where(row_head == col_head, s, -jnp.inf)
    p = jnp.exp(s - jnp.max(s, axis=-1, keepdims=True))
    denom = jnp.sum(p, axis=-1, keepdims=True)
    return _dot(p.astype(BF16), v2) / denom


class _Handoff:
    N = 8

    def __init__(self, refs):
        (self.mix, self.ug, self.v, self.zs, self.xs, self.bm, self.cm, self.dt) = refs


def _prompt_step(x_ref, kb_ref, vb_ref, win_ref, wout_ref, ng_ref, gmg_ref, gmb_ref, ws_ref, bsf_ref,
                 cw_ref, cb_ref, dtb_ref, alog_ref, dsk_ref, sng_ref, fg_ref, y_ref,
                 hn_sc, xp_sc, ht_sc, ssd_sc, out_sc, yd_sc, st_sc, ea_sc, buf, tile, rows):
    a_row = -jnp.exp(alog_ref[...])
    row = lax.broadcasted_iota(jnp.int32, (CHUNK, CHUNK), 0)
    col = lax.broadcasted_iota(jnp.int32, (CHUNK, CHUNK), 1)
    causal = row >= col
    tril_b = causal.astype(BF16)
    head_lane = col < SSM_HEADS
    first_half = col < SSM_HEAD_DIM
    def tables(c, g=0):
        base = c * SSD_TABLES_PER_CHUNK
        return (ssd_sc.at[base], ssd_sc.at[base + 1], ssd_sc.at[base + 2], ssd_sc.at[base + 3],
                ssd_sc.at[base + 4 + 2 * g], ssd_sc.at[base + 5 + 2 * g])

    def proj(off, width):
        return _dot_nt(hn_sc[...], win_ref[off:off + width, :])

    def gmlp_mix(c):
        r = slice(c * CHUNK, (c + 1) * CHUNK)
        for h in range(GM_HEADS):
            hs = slice(h * GM_HEAD_DIM, (h + 1) * GM_HEAD_DIM)
            mixed = _dot(ws_ref[h], buf.v[r, hs]) + bsf_ref[:, hs]
            buf.mix[r, hs] = (buf.ug[r, hs] * mixed).astype(BF16)

    def seq_chunk_start(c):
        r = slice(c * CHUNK, (c + 1) * CHUNK)
        acum_sc, acum_t_sc, dt_t_sc, to_end_t_sc, _, _ = tables(c)
        dt = buf.dt[r, :]
        adt = jnp.where(head_lane, dt * a_row, 0.0)
        hi = adt.astype(BF16)
        rest = adt - hi.astype(F32)
        mid = rest.astype(BF16)
        lo = (rest - mid.astype(F32)).astype(BF16)
        acum = _dot(tril_b, hi) + _dot(tril_b, mid) + _dot(tril_b, lo)
        acum_t = acum.T
        dt_t = dt.T
        acum_sc[...] = acum
        acum_t_sc[...] = acum_t
        dt_t_sc[...] = dt_t
        to_end_t_sc[...] = jnp.exp(acum_t[:, CHUNK - 1:CHUNK] - acum_t) * dt_t
        for g in range(SSM_GROUPS):
            gs = slice(g * SSM_STATE, (g + 1) * SSM_STATE)
            _, _, _, _, cb_sc, bm_t_sc = tables(c, g)
            bm_g = buf.bm[r, gs]
            cb_sc[...] = _dot_nt(buf.cm[r, gs].astype(BF16), bm_g.astype(BF16))
            bm_t_sc[...] = bm_g.T

    def seq_group(c, g):
        r = slice(c * CHUNK, (c + 1) * CHUNK)
        acum_sc, acum_t_sc, dt_t_sc, to_end_t_sc, cb_sc, bm_t_sc = tables(c, g)
        for kk in range(0, HEADS_PER_GROUP, 2):
            k0 = g * HEADS_PER_GROUP + kk
            ps = slice(k0 * SSM_HEAD_DIM, (k0 + 2) * SSM_HEAD_DIM)
            m_parts, b_parts, ea = [], [], []
            for k in (k0, k0 + 1):
                a_col = jnp.broadcast_to(acum_sc[:, k:k + 1], (CHUNK, CHUNK))
                diff = a_col - acum_t_sc[k:k + 1, :]
                decay = jnp.exp(jnp.where(causal, diff, -jnp.inf))
                m_parts.append((cb_sc[...] * decay * dt_t_sc[k:k + 1, :]).astype(BF16))
                b_parts.append((bm_t_sc[...] * to_end_t_sc[k:k + 1, :]).astype(BF16))
                ea.append(jnp.exp(a_col))
            xs_pair = buf.xs[r, ps]
            zero = jnp.zeros_like(xs_pair)
            xs_lo = jnp.where(first_half, xs_pair, zero).astype(BF16)
            xs_hi = jnp.where(first_half, zero, xs_pair).astype(BF16)
            lhs = jnp.concatenate([jnp.concatenate(m_parts, axis=1), jnp.concatenate(b_parts, axis=1)], axis=0)
            both = _dot(lhs, jnp.concatenate([xs_lo, xs_hi], axis=0))
            yd_sc[c, :, ps] = both[0:CHUNK, :]
            st_sc[c, :, ps] = both[CHUNK:2 * CHUNK, :]
            ea_sc[c, :, ps] = jnp.where(first_half, ea[0], ea[1])

    def seq_chunk_end(c):
        r = slice(c * CHUNK, (c + 1) * CHUNK)
        h_b = ht_sc[...].astype(BF16)
        y_off = jnp.concatenate(
            [_dot(buf.cm[r, g * SSM_STATE:(g + 1) * SSM_STATE].astype(BF16), h_b[:, g * GROUP_WIDTH:(g + 1) * GROUP_WIDTH])
             for g in range(SSM_GROUPS)], axis=1)
        ea_c = ea_sc[c]
        y = yd_sc[c] + y_off * ea_c
        ht_sc[...] = ht_sc[...] * ea_c[CHUNK - 1:CHUNK, :] + st_sc[c]
        y = (y + dsk_ref[...] * buf.xs[r, :]) * buf.zs[r, :]
        buf.mix[r, GM_WIDTH:GM_WIDTH + SSM_WIDTH] = _group_rmsnorm(y, sng_ref[...]).astype(BF16)

    def out_piece(j, n):
        cs = slice(j * D_MODEL // n, (j + 1) * D_MODEL // n)
        out_sc[:, cs] = x_ref[0, rows, cs] + _dot(buf.mix[...], wout_ref[:, cs])

    assert tile // CHUNK == 2, "the emission order below is written for two chunks per tile"

    hn_sc[...] = _rms(x_ref[0, rows, :], ng_ref[...]).astype(BF16)

    buf.zs[...] = _silu(proj(OFF_Z, SSM_WIDTH))
    xp_sc[SUBLANES:SUBLANES + tile, :] = proj(OFF_XBC, CONV_DIM)
    ext = xp_sc[...].reshape(tile // SUBLANES + 1, SUBLANES, CONV_DIM)
    sub = lax.broadcasted_iota(jnp.int32, (1, SUBLANES, CONV_DIM), 1)
    acc = cb_ref[...] + cw_ref[CONV_WIDTH - 1:CONV_WIDTH, :] * ext[1:]
    for shift in range(1, CONV_WIDTH):
        rot = pltpu.roll(ext, shift, 1)
        shifted = jnp.where(sub >= shift, rot[1:], rot[:-1])
        acc = acc + cw_ref[CONV_WIDTH - 1 - shift:CONV_WIDTH - shift, :] * shifted
    acc = acc.reshape(tile, CONV_DIM)
    xbc = _silu(acc)
    buf.xs[...] = xbc[:, :SSM_WIDTH]
    buf.bm[...] = xbc[:, SSM_WIDTH:SSM_WIDTH + SSM_GROUPS * SSM_STATE]
    buf.cm[...] = xbc[:, SSM_WIDTH + SSM_GROUPS * SSM_STATE:]
    xp_sc[SUBLANES - (CONV_WIDTH - 1):SUBLANES, :] = xp_sc[tile + SUBLANES - (CONV_WIDTH - 1):tile + SUBLANES, :]
    buf.dt[...] = _softplus(proj(OFF_DT, DT_PAD) + dtb_ref[...])
    u = _gelu(proj(OFF_U, GM_WIDTH))
    buf.ug[...] = u * _silu(proj(OFF_GATE, GM_WIDTH))
    buf.v[...] = _layernorm(_gelu(proj(OFF_V, GM_WIDTH)), gmg_ref[...], gmb_ref[...]).astype(BF16)
    seq_chunk_start(0)
    seq_chunk_start(1)
    gmlp_mix(0)
    gmlp_mix(1)
    seq_group(0, 0)
    seq_group(0, 1)
    seq_chunk_end(0)
    att = _attention(proj(OFF_Q, MEM_WIDTH), lambda h: kb_ref[0, :, _head_slice(h)], lambda h: vb_ref[0, :, _head_slice(h)])
    buf.mix[:, GM_WIDTH + SSM_WIDTH:MIX_WIDTH] = (att * _silu(proj(OFF_MGATE, MEM_WIDTH))).astype(BF16)
    seq_group(1, 0)
    seq_group(1, 1)
    seq_chunk_end(1)

    n_out = 4
    for j in range(n_out):
        out_piece(j, n_out)
    y_ref[0, rows, :] = _rms(out_sc[...], fg_ref[...])


def _prompt_kernel(x_ref, kb_ref, vb_ref, win_ref, wout_ref, ng_ref, gmg_ref, gmb_ref,
                   ws_ref, bsf_ref, cw_ref, cb_ref, dtb_ref, alog_ref, dsk_ref, sng_ref, fg_ref,
                   y_ref, ssm_ref, conv_ref, hn_sc, xpad_sc, ht_sc, ssd_sc, out_sc, yd_sc, st_sc, ea_sc, *handoff, tile, nt, total):
    t = pl.program_id(0) % nt
    tile_set = _Handoff(handoff)

    @pl.when(t == 0)
    def _():
        xpad_sc[0:SUBLANES, :] = jnp.zeros((SUBLANES, CONV_DIM), F32)
        ht_sc[...] = jnp.zeros_like(ht_sc)

    for i in range(TILES_PER_STEP):
        _prompt_step(x_ref, kb_ref, vb_ref, win_ref, wout_ref, ng_ref, gmg_ref, gmb_ref, ws_ref, bsf_ref,
                     cw_ref, cb_ref, dtb_ref, alog_ref, dsk_ref, sng_ref, fg_ref, y_ref,
                     hn_sc, xpad_sc, ht_sc, ssd_sc, out_sc, yd_sc, st_sc, ea_sc, tile_set, tile,
                     slice(i * tile, (i + 1) * tile))

    @pl.when(t == nt - 1)
    def _():
        conv_ref[0] = xpad_sc[SUBLANES - (CONV_WIDTH - 1):SUBLANES, :]
        ssm_ref[0] = ht_sc[...].T


def _prompt_layer(x, kb, vb, w_in_t, w_out, ng, gmg, gmb, ws, bsf, cw, cb, dtb, alog, dsk, sng, fg):
    b, seq, _ = x.shape
    tile = PROMPT_TILE
    step_rows = tile * TILES_PER_STEP
    nt = seq // step_rows
    total = b * nt
    assert seq % step_rows == 0 and tile % CHUNK == 0

    def const(shape):
        return pl.BlockSpec(shape, lambda s: (0,) * len(shape), pipeline_mode=pl.Buffered(1))

    in_specs = [
        pl.BlockSpec((1, step_rows, D_MODEL), lambda s: (s // nt, s % nt, 0)),
        pl.BlockSpec((1, MEM_LEN, MEM_WIDTH), lambda s: (s // nt, 0, 0)),
        pl.BlockSpec((1, MEM_LEN, MEM_WIDTH), lambda s: (s // nt, 0, 0)),
        const((IN_WIDTH, D_MODEL)),
        const((MIX_WIDTH, D_MODEL)),
        const((1, D_MODEL)), const((1, GM_WIDTH)), const((1, GM_WIDTH)),
        const((GM_HEADS, CHUNK, CHUNK)), const((CHUNK, GM_WIDTH)),
        const((CONV_WIDTH, CONV_DIM)), const((1, CONV_DIM)),
        const((1, DT_PAD)), const((1, DT_PAD)), const((1, SSM_WIDTH)), const((1, SSM_WIDTH)),
        const((1, D_MODEL)),
    ]
    out_specs = [
        pl.BlockSpec((1, step_rows, D_MODEL), lambda s: (s // nt, s % nt, 0)),
        pl.BlockSpec((1, SSM_WIDTH, SSM_STATE), lambda s: (s // nt, 0, 0)),
        pl.BlockSpec((1, CONV_WIDTH - 1, CONV_DIM), lambda s: (s // nt, 0, 0)),
    ]
    out_shape = [
        jax.ShapeDtypeStruct((b, seq, D_MODEL), F32),
        jax.ShapeDtypeStruct((b, SSM_WIDTH, SSM_STATE), F32),
        jax.ShapeDtypeStruct((b, CONV_WIDTH - 1, CONV_DIM), F32),
    ]
    handoff = [
        pltpu.VMEM((tile, MIX_WIDTH), BF16),
        pltpu.VMEM((tile, GM_WIDTH), F32),
        pltpu.VMEM((tile, GM_WIDTH), BF16),
        pltpu.VMEM((tile, SSM_WIDTH), F32),
        pltpu.VMEM((tile, SSM_WIDTH), F32),
        pltpu.VMEM((tile, SSM_GROUPS * SSM_STATE), F32),
        pltpu.VMEM((tile, SSM_GROUPS * SSM_STATE), F32),
        pltpu.VMEM((tile, DT_PAD), F32),
    ]
    assert len(handoff) == _Handoff.N
    scratch = [
        pltpu.VMEM((tile, D_MODEL), BF16),
        pltpu.VMEM((tile + SUBLANES, CONV_DIM), F32),
        pltpu.VMEM((SSM_STATE, SSM_WIDTH), F32),
        pltpu.VMEM((tile // CHUNK * SSD_TABLES_PER_CHUNK, CHUNK, CHUNK), F32),
        pltpu.VMEM((tile, D_MODEL), F32),
        pltpu.VMEM((tile // CHUNK, CHUNK, SSM_WIDTH), F32),
        pltpu.VMEM((tile // CHUNK, SSM_STATE, SSM_WIDTH), F32),
        pltpu.VMEM((tile // CHUNK, CHUNK, SSM_WIDTH), F32),
    ] + handoff
    return pl.pallas_call(
        functools.partial(_prompt_kernel, tile=tile, nt=nt, total=total),
        grid=(total,),
        in_specs=in_specs,
        out_specs=out_specs,
        out_shape=out_shape,
        scratch_shapes=scratch,
        compiler_params=pltpu.CompilerParams(dimension_semantics=("arbitrary",),
                                             vmem_limit_bytes=VMEM_LIMIT),
        name="prompt_layer",
    )(x, kb, vb, w_in_t, w_out, ng, gmg, gmb, ws, bsf, cw, cb, dtb, alog, dsk, sng, fg)


def _sample_proj_kernel(x_ref, g_ref, w_ref, wout_ref, o_ref, wb_ref, woutb_ref):
    nb, seq, d = x_ref.shape
    hn = _rms(x_ref[...].reshape(nb * seq, d), g_ref[...]).astype(BF16)
    wb = w_ref[...].astype(BF16)
    wb_ref[...] = wb
    woutb_ref[...] = wout_ref[...].astype(BF16)
    o_ref[...] = _dot_nt(hn, wb)


def _sample_proj(x, g, w_t, w_out):
    m, n = x.shape[0] * x.shape[1], w_t.shape[0]
    steps = pl.cdiv(n, PROJ_BLOCK)
    out_rows = w_out.shape[0] // steps
    assert out_rows * steps == w_out.shape[0] and out_rows % (2 * SUBLANES) == 0
    return pl.pallas_call(
        _sample_proj_kernel,
        grid=(steps,),
        in_specs=[pl.BlockSpec(x.shape, lambda j: (0, 0, 0)),
                  pl.BlockSpec((1, D_MODEL), lambda j: (0, 0)),
                  pl.BlockSpec((PROJ_BLOCK, D_MODEL), lambda j: (j, 0)),
                  pl.BlockSpec((out_rows, D_MODEL), lambda j: (j, 0))],
        out_specs=[pl.BlockSpec((m, PROJ_BLOCK), lambda j: (0, j)),
                   pl.BlockSpec((PROJ_BLOCK, D_MODEL), lambda j: (j, 0)),
                   pl.BlockSpec((out_rows, D_MODEL), lambda j: (j, 0))],
        out_shape=[jax.ShapeDtypeStruct((m, n), F32), jax.ShapeDtypeStruct((n, D_MODEL), BF16),
                   jax.ShapeDtypeStruct(w_out.shape, BF16)],
        compiler_params=pltpu.CompilerParams(dimension_semantics=("arbitrary",),
                                             vmem_limit_bytes=VMEM_LIMIT),
        name="sample_proj",
    )(x, g, w_t, w_out)


def _sample_mix_kernel(p_ref, cst_ref, ssm_ref, k_ref, v_ref, gmg_ref, gmb_ref, gcoef_ref, gbias_ref,
                       cw_ref, cb_ref, dtb_ref, alog_ref, dsk_ref, sng_ref, expand_ref,
                       mix_ref, ssm_out_ref, conv_out_ref, gv_ref,
                       xp_sc, xbc_sc, att_sc, yoff_sc, dec_sc, *, bblk, seq):
    rows = bblk * seq

    def col(off, width):
        return p_ref[:, off:off + width]

    tpos = lax.broadcasted_iota(jnp.int32, (rows, 1), 0) % seq

    def back(a, j):
        return a if j == 0 else pltpu.roll(a, j, 0)

    u = _gelu(col(OFF_U, GM_WIDTH))
    v = _layernorm(_gelu(col(OFF_V, GM_WIDTH)), gmg_ref[...], gmb_ref[...])
    gv_ref[...] = v
    mixed = gbias_ref[...]
    for j in range(seq):
        mixed = mixed + gcoef_ref[j] * back(v, j)
    mix_ref[:, 0:GM_WIDTH] = (u * mixed * _silu(col(OFF_GATE, GM_WIDTH))).astype(BF16)

    xbc_raw = col(OFF_XBC, CONV_DIM)
    for b in range(bblk):
        xp_sc[b, 0:CONV_WIDTH - 1, :] = cst_ref[b]
        xp_sc[b, CONV_WIDTH - 1:CONV_WIDTH - 1 + seq, :] = xbc_raw[b * seq:(b + 1) * seq, :]
    for b in range(bblk):
        acc = jnp.broadcast_to(cb_ref[...], (seq, CONV_DIM))
        for j in range(CONV_WIDTH):
            acc = acc + cw_ref[j:j + 1, :] * xp_sc[b, j:j + seq, :]
        xbc_sc[b * seq:(b + 1) * seq, :] = _silu(acc)
        conv_out_ref[b] = xp_sc[b, seq:seq + CONV_WIDTH - 1, :]
    xs = xbc_sc[:, 0:SSM_WIDTH]
    bm = xbc_sc[:, SSM_WIDTH:SSM_WIDTH + SSM_GROUPS * SSM_STATE]
    cm = xbc_sc[:, SSM_WIDTH + SSM_GROUPS * SSM_STATE:CONV_DIM]

    lane = lax.broadcasted_iota(jnp.int32, (rows, DT_PAD), 1)
    dt = _softplus(col(OFF_DT, DT_PAD) + dtb_ref[...])
    adt = jnp.where(lane < SSM_HEADS, dt * (-jnp.exp(alog_ref[...])), 0.0)
    acum = adt
    for j in range(1, seq):
        acum = acum + jnp.where(tpos >= j, back(adt, j), 0.0)
    a_last = jnp.zeros_like(acum)
    for j in range(seq):
        a_last = a_last + jnp.where(tpos == seq - 1 - j, acum if j == 0 else pltpu.roll(acum, rows - j, 0), 0.0)
    coefs = []
    for j in range(seq):
        cbj = []
        for g in range(SSM_GROUPS):
            gs = slice(g * SSM_STATE, (g + 1) * SSM_STATE)
            cbj.append(jnp.sum(cm[:, gs] * back(bm[:, gs], j), axis=-1, keepdims=True))
        cb_l = jnp.where(lane < HEADS_PER_GROUP, cbj[0], cbj[1])
        valid = tpos >= j
        decay = jnp.exp(jnp.where(valid, acum - back(acum, j), 0.0))
        coefs.append(jnp.where(valid, cb_l * decay * back(dt, j), 0.0))
    coefs.append(jnp.exp(acum))
    coefs.append(dt * jnp.exp(a_last - acum))
    stack = jnp.concatenate(coefs, axis=0)
    hi = stack.astype(BF16)
    lo = (stack - hi.astype(F32)).astype(BF16)
    wide = _dot(hi, expand_ref[...]) + _dot(lo, expand_ref[...])
    y = dsk_ref[...] * xs
    for j in range(seq):
        y = y + wide[j * rows:(j + 1) * rows, :] * back(xs, j)
    e_wide = wide[seq * rows:(seq + 1) * rows, :]
    wx = (xs * wide[(seq + 1) * rows:(seq + 2) * rows, :]).astype(BF16)
    dec_rows = jnp.exp(a_last)

    q = col(OFF_Q, MEM_WIDTH)
    cmb = cm.astype(BF16)
    bmb = bm.astype(BF16)
    for b in range(bblk):
        rs = slice(b * seq, (b + 1) * seq)
        att = _attention_interleaved(
            q[rs, :],
            k_ref[0, b].reshape(MEM_LEN * MEM_HEADS, MEM_HEAD_DIM).astype(BF16),
            v_ref[0, b].reshape(MEM_LEN * MEM_HEADS, MEM_HEAD_DIM).astype(BF16))
        for h in range(MEM_HEADS):
            att_sc[rs, _head_slice(h)] = att[h * seq:(h + 1) * seq, :]
        h0 = ssm_ref[b]
        h0b = h0.astype(BF16)
        dec_sc[...] = jnp.broadcast_to(dec_rows[b * seq + seq - 1:b * seq + seq, :], (LANES, DT_PAD)).T
        for g in range(SSM_GROUPS):
            gs = slice(g * SSM_STATE, (g + 1) * SSM_STATE)
            ws_ = slice(g * GROUP_WIDTH, (g + 1) * GROUP_WIDTH)
            yoff_sc[rs, ws_] = _dot_nt(cmb[rs, gs], h0b[ws_, :])
            upd = _dot_tn(wx[rs, ws_], bmb[rs, gs])
            for kk in range(HEADS_PER_GROUP):
                k = g * HEADS_PER_GROUP + kk
                hs = slice(k * SSM_HEAD_DIM, (k + 1) * SSM_HEAD_DIM)
                ssm_out_ref[b, hs, :] = (h0[hs, :] * dec_sc[k:k + 1, :]
                                         + upd[kk * SSM_HEAD_DIM:(kk + 1) * SSM_HEAD_DIM, :])

    y = (y + yoff_sc[...] * e_wide) * _silu(col(OFF_Z, SSM_WIDTH))
    mix_ref[:, GM_WIDTH:GM_WIDTH + SSM_WIDTH] = _group_rmsnorm(y, sng_ref[...]).astype(BF16)
    mix_ref[:, GM_WIDTH + SSM_WIDTH:MIX_WIDTH] = (att_sc[...] * _silu(col(OFF_MGATE, MEM_WIDTH))).astype(BF16)


def _sample_mix(proj, conv_state, ssm_state, mem_k, mem_v, gmg, gmb, gcoef, gbias, cw, cb, dtb, alog,
                dsk, sng, expand, seq):
    nb = conv_state.shape[0]
    bblk = SAMPLE_BATCH_BLOCK
    rows = bblk * seq
    assert nb % bblk == 0 and rows % (2 * SUBLANES) == 0

    def const(shape):
        return pl.BlockSpec(shape, lambda i: (0,) * len(shape))

    in_specs = [
        pl.BlockSpec((rows, IN_WIDTH), lambda i: (i, 0)),
        pl.BlockSpec((bblk, CONV_WIDTH - 1, CONV_DIM), lambda i: (i, 0, 0)),
        pl.BlockSpec((bblk, SSM_WIDTH, SSM_STATE), lambda i: (i, 0, 0)),
        pl.BlockSpec((1, bblk, MEM_LEN, MEM_HEADS, MEM_HEAD_DIM), lambda i: (0, i, 0, 0, 0)),
        pl.BlockSpec((1, bblk, MEM_LEN, MEM_HEADS, MEM_HEAD_DIM), lambda i: (0, i, 0, 0, 0)),
        const((1, GM_WIDTH)), const((1, GM_WIDTH)),
        const((seq, rows, GM_WIDTH)), const((rows, GM_WIDTH)),
        const((CONV_WIDTH, CONV_DIM)), const((1, CONV_DIM)),
        const((1, DT_PAD)), const((1, DT_PAD)), const((1, SSM_WIDTH)), const((1, SSM_WIDTH)),
        const((DT_PAD, SSM_WIDTH)),
    ]
    out_specs = [
        pl.BlockSpec((rows, MIX_WIDTH), lambda i: (i, 0)),
        pl.BlockSpec((bblk, SSM_WIDTH, SSM_STATE), lambda i: (i, 0, 0)),
        pl.BlockSpec((bblk, CONV_WIDTH - 1, CONV_DIM), lambda i: (i, 0, 0)),
        pl.BlockSpec((rows, GM_WIDTH), lambda i: (i, 0)),
    ]
    out_shape = [
        jax.ShapeDtypeStruct((nb * seq, MIX_WIDTH), BF16),
        jax.ShapeDtypeStruct((nb, SSM_WIDTH, SSM_STATE), F32),
        jax.ShapeDtypeStruct((nb, CONV_WIDTH - 1, CONV_DIM), F32),
        jax.ShapeDtypeStruct((nb * seq, GM_WIDTH), F32),
    ]
    scratch = [
        pltpu.VMEM((bblk, SUBLANES, CONV_DIM), F32),
        pltpu.VMEM((rows, CONV_DIM), F32),
        pltpu.VMEM((rows, MEM_WIDTH), F32),
        pltpu.VMEM((rows, SSM_WIDTH), F32),
        pltpu.VMEM((DT_PAD, LANES), F32),
    ]
    return pl.pallas_call(
        functools.partial(_sample_mix_kernel, bblk=bblk, seq=seq),
        grid=(nb // bblk,),
        in_specs=in_specs,
        out_specs=out_specs,
        out_shape=out_shape,
        scratch_shapes=scratch,
        compiler_params=pltpu.CompilerParams(dimension_semantics=("arbitrary",),
                                             vmem_limit_bytes=VMEM_LIMIT),
        name="sample_mix",
    )(proj, conv_state, ssm_state, mem_k, mem_v, gmg, gmb, gcoef, gbias, cw, cb, dtb, alog, dsk, sng, expand)


def _sample_out_kernel(mix_ref, x_ref, w_ref, g_ref, o_ref):
    nb, seq, d = x_ref.shape
    out = x_ref[...].reshape(nb * seq, d) + _dot(mix_ref[...], w_ref[...])
    o_ref[...] = _rms(out, g_ref[...]).reshape(nb, seq, d)


def _sample_out(mix, x, w_out, g):
    m = x.shape[0] * x.shape[1]
    full = lambda shape: pl.BlockSpec(shape, lambda i: (0,) * len(shape))
    return pl.pallas_call(
        _sample_out_kernel,
        grid=(1,),
        in_specs=[full((m, MIX_WIDTH)), full(x.shape), full((MIX_WIDTH, D_MODEL)), full((1, D_MODEL))],
        out_specs=full(x.shape),
        out_shape=jax.ShapeDtypeStruct(x.shape, F32),
        compiler_params=pltpu.CompilerParams(dimension_semantics=("arbitrary",),
                                             vmem_limit_bytes=VMEM_LIMIT),
        name="sample_out",
    )(mix, x, w_out, g)


def _pad_heads(a):
    return jnp.pad(a.astype(F32), (0, DT_PAD - SSM_HEADS)).reshape(1, DT_PAD)


def kernel(x_prompt, x_sample, mem_prompt, state_ssm, state_conv, cache_mem_k, cache_mem_v, norm_g, w_in,
           gm_norm_g, gm_norm_b, gm_w_spatial, gm_b_spatial, conv_w, conv_b, dt_bias, a_log, d_skip,
           ssm_norm_g, mem_norm_g, w_mem_k, w_mem_v, w_out, final_norm_g):
    assert norm_g.shape[0] == 1, "single layer"
    bp, seq_p, _ = x_prompt.shape
    bs, seq_s, _ = x_sample.shape
    row = lambda a: a.reshape(1, -1).astype(F32)

    ng, gmg, gmb = row(norm_g[0]), row(gm_norm_g[0]), row(gm_norm_b[0])
    cw, cb = conv_w[0].astype(F32), row(conv_b[0])
    dtb, alog = _pad_heads(dt_bias[0]), _pad_heads(a_log[0])
    dsk = row(jnp.repeat(d_skip[0], SSM_HEAD_DIM))
    sng, fg = row(ssm_norm_g[0]), row(final_norm_g)
    w_sp = gm_w_spatial[0]
    tril_p = jnp.tril(jnp.ones((CHUNK, CHUNK), bool))
    ws_p = jnp.where(tril_p, w_sp, 0).astype(BF16)
    bsf_p = jnp.repeat(gm_b_spatial[0].T, GM_HEAD_DIM, axis=1).astype(F32)

    proj_s, w_in_t, w_out_b = _sample_proj(x_sample, ng, jnp.swapaxes(w_in[0], 0, 1), w_out[0])

    mk, mv, mkb, mvb = _memory_kv(mem_prompt, row(mem_norm_g[0]), w_mem_k[0], w_mem_v[0])
    y_p, ssm_p, conv_p = _prompt_layer(x_prompt, mkb, mvb, w_in_t, w_out_b, ng, gmg, gmb, ws_p, bsf_p,
                                       cw, cb, dtb, alog, dsk, sng, fg)

    rows = SAMPLE_BATCH_BLOCK * seq_s
    tpos = jnp.arange(rows) % seq_s
    w_corner = w_sp[:, :seq_s, :seq_s]
    gcoef = jnp.stack([
        jnp.where((tpos >= j)[:, None],
                  jnp.repeat(w_corner[:, tpos, jnp.maximum(tpos - j, 0)].T, GM_HEAD_DIM, axis=1), 0.0)
        for j in range(seq_s)]).astype(F32)
    gbias = jnp.repeat(gm_b_spatial[0][:, tpos].T, GM_HEAD_DIM, axis=1).astype(F32)
    expand = (jnp.arange(DT_PAD)[:, None] == (jnp.arange(SSM_WIDTH) // SSM_HEAD_DIM)[None, :]).astype(BF16)

    mix_s, ssm_s, conv_s, gv_s = _sample_mix(
        proj_s, state_conv[0], state_ssm[0].reshape(bs, SSM_WIDTH, SSM_STATE),
        cache_mem_k, cache_mem_v,
        gmg, gmb, gcoef, gbias, cw, cb, dtb, alog, dsk, sng, expand, seq_s)
    y_s = _sample_out(mix_s, x_sample, w_out_b, fg)

    return (y_p,
            y_s,
            ssm_p.reshape(1, bp, SSM_HEADS, SSM_HEAD_DIM, SSM_STATE),
            conv_p[None],
            mk,
            mv,
            ssm_s.reshape(1, bs, SSM_HEADS, SSM_HEAD_DIM, SSM_STATE),
            conv_s[None],
            gv_s.reshape(1, bs, seq_s, GM_WIDTH))
```

```python
import functools
import math

import jax
import jax.numpy as jnp
from jax import lax
from jax.experimental import pallas as pl
from jax.experimental.pallas import tpu as pltpu

F32 = jnp.float32
BF16 = jnp.bfloat16

D_MODEL = 1024
GM_WIDTH = 1024
GM_HEADS = 8
GM_HEAD_DIM = 128
CHUNK = 128
SSM_WIDTH = 1024
SSM_HEADS = 16
SSM_HEAD_DIM = 64
SSM_GROUPS = 2
SSM_STATE = 128
HEADS_PER_GROUP = SSM_HEADS // SSM_GROUPS
SSD_TABLES_PER_CHUNK = 4 + 2 * SSM_GROUPS
GROUP_WIDTH = SSM_WIDTH // SSM_GROUPS
CONV_WIDTH = 4
CONV_DIM = SSM_WIDTH + 2 * SSM_GROUPS * SSM_STATE
MEM_LEN = 256
MEM_HEADS = 4
MEM_HEAD_DIM = 256
MEM_WIDTH = 1024
MIX_WIDTH = GM_WIDTH + SSM_WIDTH + MEM_WIDTH
EPS = 1e-6

LANES = 128
SUBLANES = 8

DT_PAD = LANES
OFF_U = 0
OFF_V = OFF_U + GM_WIDTH
OFF_GATE = OFF_V + GM_WIDTH
OFF_Z = OFF_GATE + GM_WIDTH
OFF_XBC = OFF_Z + SSM_WIDTH
OFF_DT = OFF_XBC + CONV_DIM
OFF_Q = OFF_DT + SSM_HEADS
OFF_MGATE = OFF_Q + MEM_WIDTH
IN_WIDTH = OFF_MGATE + MEM_WIDTH
PROJ_BLOCK = 1024

PROMPT_TILE = 256
TILES_PER_STEP = 2
SAMPLE_BATCH_BLOCK = 4
VMEM_LIMIT = 56 * 1024 * 1024


def _rms(x, g):
    return x * lax.rsqrt(jnp.mean(x * x, axis=-1, keepdims=True) + EPS) * g


def _gelu(x):
    return 0.5 * x * (1.0 + lax.erf(x * math.sqrt(0.5)))


def _silu(x):
    h = 0.5 * x
    return h + h * jnp.tanh(h)


def _softplus(x):
    return jnp.maximum(x, 0.0) + jnp.log1p(jnp.exp(-jnp.abs(x)))


def _layernorm(x, g, b):
    mu = jnp.mean(x, axis=-1, keepdims=True)
    xc = x - mu
    var = jnp.mean(xc * xc, axis=-1, keepdims=True)
    return xc * lax.rsqrt(var + EPS) * g + b


def _dot(a, b):
    return jnp.dot(a, b, preferred_element_type=F32)


def _dot_nt(a, b):
    return lax.dot_general(a, b, (((1,), (1,)), ((), ())), preferred_element_type=F32)


def _dot_tn(a, b):
    return lax.dot_general(a, b, (((0,), (0,)), ((), ())), preferred_element_type=F32)


def _group_rmsnorm(y, g):
    halves = []
    for i in range(SSM_GROUPS):
        yg = y[:, i * GROUP_WIDTH:(i + 1) * GROUP_WIDTH]
        halves.append(yg * lax.rsqrt(jnp.mean(yg * yg, axis=-1, keepdims=True) + EPS))
    return jnp.concatenate(halves, axis=-1) * g


def _memory_kv_kernel(mem_ref, g_ref, wk_ref, wv_ref, k_ref, v_ref, kb_ref, vb_ref, wkb_sc, wvb_sc):
    @pl.when(pl.program_id(0) == 0)
    def _():
        wkb_sc[...] = wk_ref[...].astype(BF16)
        wvb_sc[...] = wv_ref[...].astype(BF16)

    m = _rms(mem_ref[0], g_ref[...]).astype(BF16)
    k = _dot(m, wkb_sc[...])
    v = _dot(m, wvb_sc[...])
    for h in range(MEM_HEADS):
        k_ref[0, 0, :, h, :] = k[:, _head_slice(h)]
        v_ref[0, 0, :, h, :] = v[:, _head_slice(h)]
    kb_ref[0] = k.astype(BF16)
    vb_ref[0] = v.astype(BF16)


def _memory_kv(mem, g, wk, wv):
    b = mem.shape[0]
    blk = pl.BlockSpec((1, MEM_LEN, D_MODEL), lambda i: (i, 0, 0))
    blk5 = pl.BlockSpec((1, 1, MEM_LEN, MEM_HEADS, MEM_HEAD_DIM), lambda i: (0, i, 0, 0, 0))
    const = lambda shape: pl.BlockSpec(shape, lambda i: (0,) * len(shape), pipeline_mode=pl.Buffered(1))
    return pl.pallas_call(
        _memory_kv_kernel,
        grid=(b,),
        in_specs=[blk, const((1, D_MODEL)), const((D_MODEL, MEM_WIDTH)), const((D_MODEL, MEM_WIDTH))],
        out_specs=[blk5, blk5, blk, blk],
        out_shape=[jax.ShapeDtypeStruct((1, b, MEM_LEN, MEM_HEADS, MEM_HEAD_DIM), F32)] * 2
        + [jax.ShapeDtypeStruct((b, MEM_LEN, MEM_WIDTH), BF16)] * 2,
        scratch_shapes=[pltpu.VMEM((D_MODEL, MEM_WIDTH), BF16)] * 2,
        compiler_params=pltpu.CompilerParams(dimension_semantics=("arbitrary",),
                                             vmem_limit_bytes=VMEM_LIMIT),
        name="memory_kv",
    )(mem, g, wk, wv)


def _head_slice(h):
    return slice(h * MEM_HEAD_DIM, (h + 1) * MEM_HEAD_DIM)


def _attention(q, k_head, v_head):
    qb = (q * (MEM_HEAD_DIM ** -0.5)).astype(BF16)
    outs = []
    for h in range(MEM_HEADS):
        s = _dot_nt(qb[:, _head_slice(h)], k_head(h))
        p = jnp.exp(s - jnp.max(s, axis=-1, keepdims=True))
        denom = jnp.sum(p, axis=-1, keepdims=True)
        outs.append(_dot(p.astype(BF16), v_head(h)) / denom)
    return jnp.concatenate(outs, axis=-1)


def _attention_interleaved(q, k2, v2):
    t = q.shape[0]
    qb = (q * (MEM_HEAD_DIM ** -0.5)).astype(BF16)
    q2 = jnp.concatenate([qb[:, _head_slice(h)] for h in range(MEM_HEADS)], axis=0)
    s = _dot_nt(q2, k2)
    row_head = lax.broadcasted_iota(jnp.int32, s.shape, 0) // t
    col_head = lax.broadcasted_iota(jnp.int32, s.shape, 1) % MEM_HEADS
    s = jnp.where(row_head == col_head, s, -jnp.inf)
    p = jnp.exp(s - jnp.max(s, axis=-1, keepdims=True))
    denom = jnp.sum(p, axis=-1, keepdims=True)
    return _dot(p.astype(BF16), v2) / denom


class _Handoff:
    N = 8

    def __init__(self, refs):
        (self.mix, self.ug, self.v, self.zs, self.xs, self.bm, self.cm, self.dt) = refs


def _prompt_step(x_ref, kb_ref, vb_ref, win_ref, wout_ref, ng_ref, gmg_ref, gmb_ref, ws_ref, bsf_ref,
                 cw_ref, cb_ref, dtb_ref, alog_ref, dsk_ref, sng_ref, fg_ref, y_ref,
                 hn_sc, xp_sc, ht_sc, ssd_sc, yd_sc, st_sc, ea_sc, buf, tile, rows):
    a_row = -jnp.exp(alog_ref[...])
    row = lax.broadcasted_iota(jnp.int32, (CHUNK, CHUNK), 0)
    col = lax.broadcasted_iota(jnp.int32, (CHUNK, CHUNK), 1)
    causal = row >= col
    tril_b = causal.astype(BF16)
    head_lane = col < SSM_HEADS
    first_half = col < SSM_HEAD_DIM
    def tables(c, g=0):
        base = c * SSD_TABLES_PER_CHUNK
        return (ssd_sc.at[base], ssd_sc.at[base + 1], ssd_sc.at[base + 2], ssd_sc.at[base + 3],
                ssd_sc.at[base + 4 + 2 * g], ssd_sc.at[base + 5 + 2 * g])

    def proj(off, width):
        return _dot_nt(hn_sc[...], win_ref[off:off + width, :])

    def gmlp_mix(c):
        r = slice(c * CHUNK, (c + 1) * CHUNK)
        for h in range(GM_HEADS):
            hs = slice(h * GM_HEAD_DIM, (h + 1) * GM_HEAD_DIM)
            mixed = _dot(ws_ref[h], buf.v[r, hs]) + bsf_ref[:, hs]
            buf.mix[r, hs] = (buf.ug[r, hs] * mixed).astype(BF16)

    def seq_chunk_start(c):
        r = slice(c * CHUNK, (c + 1) * CHUNK)
        acum_sc, acum_t_sc, dt_t_sc, to_end_t_sc, _, _ = tables(c)
        dt = buf.dt[r, :]
        adt = jnp.where(head_lane, dt * a_row, 0.0)
        hi = adt.astype(BF16)
        rest = adt - hi.astype(F32)
        mid = rest.astype(BF16)
        lo = (rest - mid.astype(F32)).astype(BF16)
        acum = _dot(tril_b, hi) + _dot(tril_b, mid) + _dot(tril_b, lo)
        acum_t = acum.T
        dt_t = dt.T
        acum_sc[...] = acum
        acum_t_sc[...] = acum_t
        dt_t_sc[...] = dt_t
        to_end_t_sc[...] = jnp.exp(acum_t[:, CHUNK - 1:CHUNK] - acum_t) * dt_t
        for g in range(SSM_GROUPS):
            gs = slice(g * SSM_STATE, (g + 1) * SSM_STATE)
            _, _, _, _, cb_sc, bm_t_sc = tables(c, g)
            bm_g = buf.bm[r, gs]
            cb_sc[...] = _dot_nt(buf.cm[r, gs].astype(BF16), bm_g.astype(BF16))
            bm_t_sc[...] = bm_g.T

    def seq_group(c, g):
        r = slice(c * CHUNK, (c + 1) * CHUNK)
        acum_sc, acum_t_sc, dt_t_sc, to_end_t_sc, cb_sc, bm_t_sc = tables(c, g)
        for kk in range(0, HEADS_PER_GROUP, 2):
            k0 = g * HEADS_PER_GROUP + kk
            ps = slice(k0 * SSM_HEAD_DIM, (k0 + 2) * SSM_HEAD_DIM)
            m_parts, b_parts, ea = [], [], []
            for k in (k0, k0 + 1):
                a_col = jnp.broadcast_to(acum_sc[:, k:k + 1], (CHUNK, CHUNK))
                diff = a_col - acum_t_sc[k:k + 1, :]
                decay = jnp.exp(jnp.where(causal, diff, -jnp.inf))
                m_parts.append((cb_sc[...] * decay * dt_t_sc[k:k + 1, :]).astype(BF16))
                b_parts.append((bm_t_sc[...] * to_end_t_sc[k:k + 1, :]).astype(BF16))
                ea.append(jnp.exp(a_col))
            xs_pair = buf.xs[r, ps]
            zero = jnp.zeros_like(xs_pair)
            xs_lo = jnp.where(first_half, xs_pair, zero).astype(BF16)
            xs_hi = jnp.where(first_half, zero, xs_pair).astype(BF16)
            lhs = jnp.concatenate([jnp.concatenate(m_parts, axis=1), jnp.concatenate(b_parts, axis=1)], axis=0)
            both = _dot(lhs, jnp.concatenate([xs_lo, xs_hi], axis=0))
            yd_sc[c, :, ps] = both[0:CHUNK, :]
            st_sc[c, :, ps] = both[CHUNK:2 * CHUNK, :]
            ea_sc[c, :, ps] = jnp.where(first_half, ea[0], ea[1])

    def seq_chunk_end(c):
        r = slice(c * CHUNK, (c + 1) * CHUNK)
        h_b = ht_sc[...].astype(BF16)
        y_off = jnp.concatenate(
            [_dot(buf.cm[r, g * SSM_STATE:(g + 1) * SSM_STATE].astype(BF16), h_b[:, g * GROUP_WIDTH:(g + 1) * GROUP_WIDTH])
             for g in range(SSM_GROUPS)], axis=1)
        ea_c = ea_sc[c]
        y = yd_sc[c] + y_off * ea_c
        ht_sc[...] = ht_sc[...] * ea_c[CHUNK - 1:CHUNK, :] + st_sc[c]
        y = (y + dsk_ref[...] * buf.xs[r, :]) * buf.zs[r, :]
        buf.mix[r, GM_WIDTH:GM_WIDTH + SSM_WIDTH] = _group_rmsnorm(y, sng_ref[...]).astype(BF16)

    assert tile // CHUNK == 2, "the emission order below is written for two chunks per tile"

    hn_sc[...] = _rms(x_ref[0, rows, :], ng_ref[...]).astype(BF16)

    buf.zs[...] = _silu(proj(OFF_Z, SSM_WIDTH))
    xp_sc[SUBLANES:SUBLANES + tile, :] = proj(OFF_XBC, CONV_DIM)
    ext = xp_sc[...].reshape(tile // SUBLANES + 1, SUBLANES, CONV_DIM)
    sub = lax.broadcasted_iota(jnp.int32, (1, SUBLANES, CONV_DIM), 1)
    acc = cb_ref[...] + cw_ref[CONV_WIDTH - 1:CONV_WIDTH, :] * ext[1:]
    for shift in range(1, CONV_WIDTH):
        rot = pltpu.roll(ext, shift, 1)
        shifted = jnp.where(sub >= shift, rot[1:], rot[:-1])
        acc = acc + cw_ref[CONV_WIDTH - 1 - shift:CONV_WIDTH - shift, :] * shifted
    acc = acc.reshape(tile, CONV_DIM)
    xbc = _silu(acc)
    buf.xs[...] = xbc[:, :SSM_WIDTH]
    buf.bm[...] = xbc[:, SSM_WIDTH:SSM_WIDTH + SSM_GROUPS * SSM_STATE]
    buf.cm[...] = xbc[:, SSM_WIDTH + SSM_GROUPS * SSM_STATE:]
    xp_sc[SUBLANES - (CONV_WIDTH - 1):SUBLANES, :] = xp_sc[tile + SUBLANES - (CONV_WIDTH - 1):tile + SUBLANES, :]
    buf.dt[...] = _softplus(proj(OFF_DT, DT_PAD) + dtb_ref[...])
    u = _gelu(proj(OFF_U, GM_WIDTH))
    buf.ug[...] = u * _silu(proj(OFF_GATE, GM_WIDTH))
    buf.v[...] = _layernorm(_gelu(proj(OFF_V, GM_WIDTH)), gmg_ref[...], gmb_ref[...]).astype(BF16)
    seq_chunk_start(0)
    seq_chunk_start(1)
    gmlp_mix(0)
    gmlp_mix(1)
    seq_group(0, 0)
    seq_group(0, 1)
    seq_chunk_end(0)
    att = _attention(proj(OFF_Q, MEM_WIDTH), lambda h: kb_ref[0, :, _head_slice(h)], lambda h: vb_ref[0, :, _head_slice(h)])
    buf.mix[:, GM_WIDTH + SSM_WIDTH:MIX_WIDTH] = (att * _silu(proj(OFF_MGATE, MEM_WIDTH))).astype(BF16)
    seq_group(1, 0)
    seq_group(1, 1)
    seq_chunk_end(1)

    y_ref[0, rows, :] = _rms(x_ref[0, rows, :] + _dot(buf.mix[...], wout_ref[...]), fg_ref[...])


def _prompt_kernel(x_ref, kb_ref, vb_ref, win_ref, wout_ref, ng_ref, gmg_ref, gmb_ref,
                   ws_ref, bsf_ref, cw_ref, cb_ref, dtb_ref, alog_ref, dsk_ref, sng_ref, fg_ref,
                   y_ref, ssm_ref, conv_ref, hn_sc, xpad_sc, ht_sc, ssd_sc, yd_sc, st_sc, ea_sc, *handoff, tile, nt, total):
    t = pl.program_id(0) % nt
    tile_set = _Handoff(handoff)

    @pl.when(t == 0)
    def _():
        xpad_sc[0:SUBLANES, :] = jnp.zeros((SUBLANES, CONV_DIM), F32)
        ht_sc[...] = jnp.zeros_like(ht_sc)

    for i in range(TILES_PER_STEP):
        _prompt_step(x_ref, kb_ref, vb_ref, win_ref, wout_ref, ng_ref, gmg_ref, gmb_ref, ws_ref, bsf_ref,
                     cw_ref, cb_ref, dtb_ref, alog_ref, dsk_ref, sng_ref, fg_ref, y_ref,
                     hn_sc, xpad_sc, ht_sc, ssd_sc, yd_sc, st_sc, ea_sc, tile_set, tile,
                     slice(i * tile, (i + 1) * tile))

    @pl.when(t == nt - 1)
    def _():
        conv_ref[0] = xpad_sc[SUBLANES - (CONV_WIDTH - 1):SUBLANES, :]
        ssm_ref[0] = ht_sc[...].T


def _prompt_layer(x, kb, vb, w_in_t, w_out, ng, gmg, gmb, ws, bsf, cw, cb, dtb, alog, dsk, sng, fg):
    b, seq, _ = x.shape
    tile = PROMPT_TILE
    step_rows = tile * TILES_PER_STEP
    nt = seq // step_rows
    total = b * nt
    assert seq % step_rows == 0 and tile % CHUNK == 0

    def const(shape):
        return pl.BlockSpec(shape, lambda s: (0,) * len(shape), pipeline_mode=pl.Buffered(1))

    in_specs = [
        pl.BlockSpec((1, step_rows, D_MODEL), lambda s: (s // nt, s % nt, 0)),
        pl.BlockSpec((1, MEM_LEN, MEM_WIDTH), lambda s: (s // nt, 0, 0)),
        pl.BlockSpec((1, MEM_LEN, MEM_WIDTH), lambda s: (s // nt, 0, 0)),
        const((IN_WIDTH, D_MODEL)),
        const((MIX_WIDTH, D_MODEL)),
        const((1, D_MODEL)), const((1, GM_WIDTH)), const((1, GM_WIDTH)),
        const((GM_HEADS, CHUNK, CHUNK)), const((CHUNK, GM_WIDTH)),
        const((CONV_WIDTH, CONV_DIM)), const((1, CONV_DIM)),
        const((1, DT_PAD)), const((1, DT_PAD)), const((1, SSM_WIDTH)), const((1, SSM_WIDTH)),
        const((1, D_MODEL)),
    ]
    out_specs = [
        pl.BlockSpec((1, step_rows, D_MODEL), lambda s: (s // nt, s % nt, 0)),
        pl.BlockSpec((1, SSM_WIDTH, SSM_STATE), lambda s: (s // nt, 0, 0)),
        pl.BlockSpec((1, CONV_WIDTH - 1, CONV_DIM), lambda s: (s // nt, 0, 0)),
    ]
    out_shape = [
        jax.ShapeDtypeStruct((b, seq, D_MODEL), F32),
        jax.ShapeDtypeStruct((b, SSM_WIDTH, SSM_STATE), F32),
        jax.ShapeDtypeStruct((b, CONV_WIDTH - 1, CONV_DIM), F32),
    ]
    handoff = [
        pltpu.VMEM((tile, MIX_WIDTH), BF16),
        pltpu.VMEM((tile, GM_WIDTH), F32),
        pltpu.VMEM((tile, GM_WIDTH), BF16),
        pltpu.VMEM((tile, SSM_WIDTH), F32),
        pltpu.VMEM((tile, SSM_WIDTH), F32),
        pltpu.VMEM((tile, SSM_GROUPS * SSM_STATE), F32),
        pltpu.VMEM((tile, SSM_GROUPS * SSM_STATE), F32),
        pltpu.VMEM((tile, DT_PAD), F32),
    ]
    assert len(handoff) == _Handoff.N
    scratch = [
        pltpu.VMEM((tile, D_MODEL), BF16),
        pltpu.VMEM((tile + SUBLANES, CONV_DIM), F32),
        pltpu.VMEM((SSM_STATE, SSM_WIDTH), F32),
        pltpu.VMEM((tile // CHUNK * SSD_TABLES_PER_CHUNK, CHUNK, CHUNK), F32),
        pltpu.VMEM((tile // CHUNK, CHUNK, SSM_WIDTH), F32),
        pltpu.VMEM((tile // CHUNK, SSM_STATE, SSM_WIDTH), F32),
        pltpu.VMEM((tile // CHUNK, CHUNK, SSM_WIDTH), F32),
    ] + handoff
    return pl.pallas_call(
        functools.partial(_prompt_kernel, tile=tile, nt=nt, total=total),
        grid=(total,),
        in_specs=in_specs,
        out_specs=out_specs,
        out_shape=out_shape,
        scratch_shapes=scratch,
        compiler_params=pltpu.CompilerParams(dimension_semantics=("arbitrary",),
                                             vmem_limit_bytes=VMEM_LIMIT),
        name="prompt_layer",
    )(x, kb, vb, w_in_t, w_out, ng, gmg, gmb, ws, bsf, cw, cb, dtb, alog, dsk, sng, fg)


def _sample_proj_kernel(x_ref, g_ref, w_ref, wout_ref, o_ref, wb_ref, woutb_ref):
    nb, seq, d = x_ref.shape
    hn = _rms(x_ref[...].reshape(nb * seq, d), g_ref[...]).astype(BF16)
    wb = w_ref[...].astype(BF16)
    wb_ref[...] = wb
    woutb_ref[...] = wout_ref[...].astype(BF16)
    o_ref[...] = _dot_nt(hn, wb)


def _sample_proj(x, g, w_t, w_out):
    m, n = x.shape[0] * x.shape[1], w_t.shape[0]
    steps = pl.cdiv(n, PROJ_BLOCK)
    out_rows = w_out.shape[0] // steps
    assert out_rows * steps == w_out.shape[0] and out_rows % (2 * SUBLANES) == 0
    return pl.pallas_call(
        _sample_proj_kernel,
        grid=(steps,),
        in_specs=[pl.BlockSpec(x.shape, lambda j: (0, 0, 0)),
                  pl.BlockSpec((1, D_MODEL), lambda j: (0, 0)),
                  pl.BlockSpec((PROJ_BLOCK, D_MODEL), lambda j: (j, 0)),
                  pl.BlockSpec((out_rows, D_MODEL), lambda j: (j, 0))],
        out_specs=[pl.BlockSpec((m, PROJ_BLOCK), lambda j: (0, j)),
                   pl.BlockSpec((PROJ_BLOCK, D_MODEL), lambda j: (j, 0)),
                   pl.BlockSpec((out_rows, D_MODEL), lambda j: (j, 0))],
        out_shape=[jax.ShapeDtypeStruct((m, n), F32), jax.ShapeDtypeStruct((n, D_MODEL), BF16),
                   jax.ShapeDtypeStruct(w_out.shape, BF16)],
        compiler_params=pltpu.CompilerParams(dimension_semantics=("arbitrary",),
                                             vmem_limit_bytes=VMEM_LIMIT),
        name="sample_proj",
    )(x, g, w_t, w_out)


def _sample_mix_kernel(p_ref, cst_ref, ssm_ref, k_ref, v_ref, gmg_ref, gmb_ref, gcoef_ref, gbias_ref,
                       cw_ref, cb_ref, dtb_ref, alog_ref, dsk_ref, sng_ref, expand_ref,
                       mix_ref, ssm_out_ref, conv_out_ref, gv_ref,
                       xp_sc, xbc_sc, att_sc, yoff_sc, dec_sc, *, bblk, seq):
    rows = bblk * seq

    def col(off, width):
        return p_ref[:, off:off + width]

    tpos = lax.broadcasted_iota(jnp.int32, (rows, 1), 0) % seq

    def back(a, j):
        return a if j == 0 else pltpu.roll(a, j, 0)

    u = _gelu(col(OFF_U, GM_WIDTH))
    v = _layernorm(_gelu(col(OFF_V, GM_WIDTH)), gmg_ref[...], gmb_ref[...])
    gv_ref[...] = v
    mixed = gbias_ref[...]
    for j in range(seq):
        mixed = mixed + gcoef_ref[j] * back(v, j)
    mix_ref[:, 0:GM_WIDTH] = (u * mixed * _silu(col(OFF_GATE, GM_WIDTH))).astype(BF16)

    xbc_raw = col(OFF_XBC, CONV_DIM)
    for b in range(bblk):
        xp_sc[b, 0:CONV_WIDTH - 1, :] = cst_ref[b]
        xp_sc[b, CONV_WIDTH - 1:CONV_WIDTH - 1 + seq, :] = xbc_raw[b * seq:(b + 1) * seq, :]
    for b in range(bblk):
        acc = jnp.broadcast_to(cb_ref[...], (seq, CONV_DIM))
        for j in range(CONV_WIDTH):
            acc = acc + cw_ref[j:j + 1, :] * xp_sc[b, j:j + seq, :]
        xbc_sc[b * seq:(b + 1) * seq, :] = _silu(acc)
        conv_out_ref[b] = xp_sc[b, seq:seq + CONV_WIDTH - 1, :]
    xs = xbc_sc[:, 0:SSM_WIDTH]
    bm = xbc_sc[:, SSM_WIDTH:SSM_WIDTH + SSM_GROUPS * SSM_STATE]
    cm = xbc_sc[:, SSM_WIDTH + SSM_GROUPS * SSM_STATE:CONV_DIM]

    lane = lax.broadcasted_iota(jnp.int32, (rows, DT_PAD), 1)
    dt = _softplus(col(OFF_DT, DT_PAD) + dtb_ref[...])
    adt = jnp.where(lane < SSM_HEADS, dt * (-jnp.exp(alog_ref[...])), 0.0)
    acum = adt
    for j in range(1, seq):
        acum = acum + jnp.where(tpos >= j, back(adt, j), 0.0)
    a_last = jnp.zeros_like(acum)
    for j in range(seq):
        a_last = a_last + jnp.where(tpos == seq - 1 - j, acum if j == 0 else pltpu.roll(acum, rows - j, 0), 0.0)
    coefs = []
    for j in range(seq):
        cbj = []
        for g in range(SSM_GROUPS):
            gs = slice(g * SSM_STATE, (g + 1) * SSM_STATE)
            cbj.append(jnp.sum(cm[:, gs] * back(bm[:, gs], j), axis=-1, keepdims=True))
        cb_l = jnp.where(lane < HEADS_PER_GROUP, cbj[0], cbj[1])
        valid = tpos >= j
        decay = jnp.exp(jnp.where(valid, acum - back(acum, j), 0.0))
        coefs.append(jnp.where(valid, cb_l * decay * back(dt, j), 0.0))
    coefs.append(jnp.exp(acum))
    coefs.append(dt * jnp.exp(a_last - acum))
    stack = jnp.concatenate(coefs, axis=0)
    hi = stack.astype(BF16)
    lo = (stack - hi.astype(F32)).astype(BF16)
    wide = _dot(hi, expand_ref[...]) + _dot(lo, expand_ref[...])
    y = dsk_ref[...] * xs
    for j in range(seq):
        y = y + wide[j * rows:(j + 1) * rows, :] * back(xs, j)
    e_wide = wide[seq * rows:(seq + 1) * rows, :]
    wx = (xs * wide[(seq + 1) * rows:(seq + 2) * rows, :]).astype(BF16)
    dec_rows = jnp.exp(a_last)

    q = col(OFF_Q, MEM_WIDTH)
    cmb = cm.astype(BF16)
    bmb = bm.astype(BF16)
    for b in range(bblk):
        rs = slice(b * seq, (b + 1) * seq)
        att = _attention_interleaved(
            q[rs, :],
            k_ref[0, b].reshape(MEM_LEN * MEM_HEADS, MEM_HEAD_DIM).astype(BF16),
            v_ref[0, b].reshape(MEM_LEN * MEM_HEADS, MEM_HEAD_DIM).astype(BF16))
        for h in range(MEM_HEADS):
            att_sc[rs, _head_slice(h)] = att[h * seq:(h + 1) * seq, :]
        h0 = ssm_ref[b]
        h0b = h0.astype(BF16)
        dec_sc[...] = jnp.broadcast_to(dec_rows[b * seq + seq - 1:b * seq + seq, :], (LANES, DT_PAD)).T
        for g in range(SSM_GROUPS):
            gs = slice(g * SSM_STATE, (g + 1) * SSM_STATE)
            ws_ = slice(g * GROUP_WIDTH, (g + 1) * GROUP_WIDTH)
            yoff_sc[rs, ws_] = _dot_nt(cmb[rs, gs], h0b[ws_, :])
            upd = _dot_tn(wx[rs, ws_], bmb[rs, gs])
            for kk in range(HEADS_PER_GROUP):
                k = g * HEADS_PER_GROUP + kk
                hs = slice(k * SSM_HEAD_DIM, (k + 1) * SSM_HEAD_DIM)
                ssm_out_ref[b, hs, :] = (h0[hs, :] * dec_sc[k:k + 1, :]
                                         + upd[kk * SSM_HEAD_DIM:(kk + 1) * SSM_HEAD_DIM, :])

    y = (y + yoff_sc[...] * e_wide) * _silu(col(OFF_Z, SSM_WIDTH))
    mix_ref[:, GM_WIDTH:GM_WIDTH + SSM_WIDTH] = _group_rmsnorm(y, sng_ref[...]).astype(BF16)
    mix_ref[:, GM_WIDTH + SSM_WIDTH:MIX_WIDTH] = (att_sc[...] * _silu(col(OFF_MGATE, MEM_WIDTH))).astype(BF16)


def _sample_mix(proj, conv_state, ssm_state, mem_k, mem_v, gmg, gmb, gcoef, gbias, cw, cb, dtb, alog,
                dsk, sng, expand, seq):
    nb = conv_state.shape[0]
    bblk = SAMPLE_BATCH_BLOCK
    rows = bblk * seq
    assert nb % bblk == 0 and rows % (2 * SUBLANES) == 0

    def const(shape):
        return pl.BlockSpec(shape, lambda i: (0,) * len(shape))

    in_specs = [
        pl.BlockSpec((rows, IN_WIDTH), lambda i: (i, 0)),
        pl.BlockSpec((bblk, CONV_WIDTH - 1, CONV_DIM), lambda i: (i, 0, 0)),
        pl.BlockSpec((bblk, SSM_WIDTH, SSM_STATE), lambda i: (i, 0, 0)),
        pl.BlockSpec((1, bblk, MEM_LEN, MEM_HEADS, MEM_HEAD_DIM), lambda i: (0, i, 0, 0, 0)),
        pl.BlockSpec((1, bblk, MEM_LEN, MEM_HEADS, MEM_HEAD_DIM), lambda i: (0, i, 0, 0, 0)),
        const((1, GM_WIDTH)), const((1, GM_WIDTH)),
        const((seq, rows, GM_WIDTH)), const((rows, GM_WIDTH)),
        const((CONV_WIDTH, CONV_DIM)), const((1, CONV_DIM)),
        const((1, DT_PAD)), const((1, DT_PAD)), const((1, SSM_WIDTH)), const((1, SSM_WIDTH)),
        const((DT_PAD, SSM_WIDTH)),
    ]
    out_specs = [
        pl.BlockSpec((rows, MIX_WIDTH), lambda i: (i, 0)),
        pl.BlockSpec((bblk, SSM_WIDTH, SSM_STATE), lambda i: (i, 0, 0)),
        pl.BlockSpec((bblk, CONV_WIDTH - 1, CONV_DIM), lambda i: (i, 0, 0)),
        pl.BlockSpec((rows, GM_WIDTH), lambda i: (i, 0)),
    ]
    out_shape = [
        jax.ShapeDtypeStruct((nb * seq, MIX_WIDTH), BF16),
        jax.ShapeDtypeStruct((nb, SSM_WIDTH, SSM_STATE), F32),
        jax.ShapeDtypeStruct((nb, CONV_WIDTH - 1, CONV_DIM), F32),
        jax.ShapeDtypeStruct((nb * seq, GM_WIDTH), F32),
    ]
    scratch = [
        pltpu.VMEM((bblk, SUBLANES, CONV_DIM), F32),
        pltpu.VMEM((rows, CONV_DIM), F32),
        pltpu.VMEM((rows, MEM_WIDTH), F32),
        pltpu.VMEM((rows, SSM_WIDTH), F32),
        pltpu.VMEM((DT_PAD, LANES), F32),
    ]
    return pl.pallas_call(
        functools.partial(_sample_mix_kernel, bblk=bblk, seq=seq),
        grid=(nb // bblk,),
        in_specs=in_specs,
        out_specs=out_specs,
        out_shape=out_shape,
        scratch_shapes=scratch,
        compiler_params=pltpu.CompilerParams(dimension_semantics=("arbitrary",),
                                             vmem_limit_bytes=VMEM_LIMIT),
        name="sample_mix",
    )(proj, conv_state, ssm_state, mem_k, mem_v, gmg, gmb, gcoef, gbias, cw, cb, dtb, alog, dsk, sng, expand)


def _sample_out_kernel(mix_ref, x_ref, w_ref, g_ref, o_ref):
    nb, seq, d = x_ref.shape
    out = x_ref[...].reshape(nb * seq, d) + _dot(mix_ref[...], w_ref[...])
    o_ref[...] = _rms(out, g_ref[...]).reshape(nb, seq, d)


def _sample_out(mix, x, w_out, g):
    m = x.shape[0] * x.shape[1]
    full = lambda shape: pl.BlockSpec(shape, lambda i: (0,) * len(shape))
    return pl.pallas_call(
        _sample_out_kernel,
        grid=(1,),
        in_specs=[full((m, MIX_WIDTH)), full(x.shape), full((MIX_WIDTH, D_MODEL)), full((1, D_MODEL))],
        out_specs=full(x.shape),
        out_shape=jax.ShapeDtypeStruct(x.shape, F32),
        compiler_params=pltpu.CompilerParams(dimension_semantics=("arbitrary",),
                                             vmem_limit_bytes=VMEM_LIMIT),
        name="sample_out",
    )(mix, x, w_out, g)


def _pad_heads(a):
    return jnp.pad(a.astype(F32), (0, DT_PAD - SSM_HEADS)).reshape(1, DT_PAD)


def kernel(x_prompt, x_sample, mem_prompt, state_ssm, state_conv, cache_mem_k, cache_mem_v, norm_g, w_in,
           gm_norm_g, gm_norm_b, gm_w_spatial, gm_b_spatial, conv_w, conv_b, dt_bias, a_log, d_skip,
           ssm_norm_g, mem_norm_g, w_mem_k, w_mem_v, w_out, final_norm_g):
    assert norm_g.shape[0] == 1, "single layer"
    bp, seq_p, _ = x_prompt.shape
    bs, seq_s, _ = x_sample.shape
    row = lambda a: a.reshape(1, -1).astype(F32)

    ng, gmg, gmb = row(norm_g[0]), row(gm_norm_g[0]), row(gm_norm_b[0])
    cw, cb = conv_w[0].astype(F32), row(conv_b[0])
    dtb, alog = _pad_heads(dt_bias[0]), _pad_heads(a_log[0])
    dsk = row(jnp.repeat(d_skip[0], SSM_HEAD_DIM))
    sng, fg = row(ssm_norm_g[0]), row(final_norm_g)
    w_sp = gm_w_spatial[0]
    tril_p = jnp.tril(jnp.ones((CHUNK, CHUNK), bool))
    ws_p = jnp.where(tril_p, w_sp, 0).astype(BF16)
    bsf_p = jnp.repeat(gm_b_spatial[0].T, GM_HEAD_DIM, axis=1).astype(F32)

    proj_s, w_in_t, w_out_b = _sample_proj(x_sample, ng, jnp.swapaxes(w_in[0], 0, 1), w_out[0])

    mk, mv, mkb, mvb = _memory_kv(mem_prompt, row(mem_norm_g[0]), w_mem_k[0], w_mem_v[0])
    y_p, ssm_p, conv_p = _prompt_layer(x_prompt, mkb, mvb, w_in_t, w_out_b, ng, gmg, gmb, ws_p, bsf_p,
                                       cw, cb, dtb, alog, dsk, sng, fg)

    rows = SAMPLE_BATCH_BLOCK * seq_s
    tpos = jnp.arange(rows) % seq_s
    w_corner = w_sp[:, :seq_s, :seq_s]
    gcoef = jnp.stack([
        jnp.where((tpos >= j)[:, None],
                  jnp.repeat(w_corner[:, tpos, jnp.maximum(tpos - j, 0)].T, GM_HEAD_DIM, axis=1), 0.0)
        for j in range(seq_s)]).astype(F32)
    gbias = jnp.repeat(gm_b_spatial[0][:, tpos].T, GM_HEAD_DIM, axis=1).astype(F32)
    expand = (jnp.arange(DT_PAD)[:, None] == (jnp.arange(SSM_WIDTH) // SSM_HEAD_DIM)[None, :]).astype(BF16)

    mix_s, ssm_s, conv_s, gv_s = _sample_mix(
        proj_s, state_conv[0], state_ssm[0].reshape(bs, SSM_WIDTH, SSM_STATE),
        cache_mem_k, cache_mem_v,
        gmg, gmb, gcoef, gbias, cw, cb, dtb, alog, dsk, sng, expand, seq_s)
    y_s = _sample_out(mix_s, x_sample, w_out_b, fg)

    return (y_p,
            y_s,
            ssm_p.reshape(1, bp, SSM_HEADS, SSM_HEAD_DIM, SSM_STATE),
            conv_p[None],
            mk,
            mv,
            ssm_s.reshape(1, bs, SSM_HEADS, SSM_HEAD_DIM, SSM_STATE),
            conv_s[None],
            gv_s.reshape(1, bs, seq_s, GM_WIDTH))
```
